```python
import jax, jax.numpy as jnp
from jax import lax
import numpy as np

D_MODEL = 2048
BATCH = 8
SEQ = 4096
DEPTH = 4

N_EVEN = (DEPTH + 1) // 2
N_ODD = DEPTH // 2

MLA_HEADS = 8
QK_NOPE_DIM = 128
QK_ROPE_DIM = 64
QK_HEAD_DIM = QK_NOPE_DIM + QK_ROPE_DIM
V_HEAD_DIM = 128
Q_LORA_RANK = D_MODEL // 4
KV_LORA_RANK = D_MODEL // 4
ROPE_THETA = 10000.0
MLA_WIDTH = MLA_HEADS * V_HEAD_DIM
Q_BLOCK = 128

POOL_WINDOWS = (2, 4, 8, 16)
POOL_GROUPS = len(POOL_WINDOWS)
POOL_GROUP_DIM = D_MODEL // 8
POOL_WIDTH = POOL_GROUPS * POOL_GROUP_DIM

EVEN_IN_SIZES = (Q_LORA_RANK, KV_LORA_RANK, QK_ROPE_DIM, POOL_WIDTH)
EVEN_IN_DIM = sum(EVEN_IN_SIZES)
EVEN_MIX_DIM = MLA_WIDTH + POOL_WIDTH

CONV_WIDTH = 3
CONV_DIM = D_MODEL

D_FF = 4 * D_MODEL
RMS_EPS = 1e-6

kernel_name = "hybrid_mla_pool_shortconv_trunk"


def _rmsnorm(x, g):
    xf = x.astype(jnp.float32)
    y = xf * lax.rsqrt(jnp.mean(jnp.square(xf), axis=-1, keepdims=True) + RMS_EPS)
    return (y * g.astype(jnp.float32)).astype(x.dtype)


def _rope_tables(positions):
    inv_freq = 1.0 / (ROPE_THETA ** (jnp.arange(0, QK_ROPE_DIM, 2, dtype=jnp.float32) / QK_ROPE_DIM))
    ang = positions.astype(jnp.float32)[..., None] * inv_freq
    return jnp.cos(ang)[:, :, None, :], jnp.sin(ang)[:, :, None, :]


def _rope(x, cos, sin):
    xf = x.astype(jnp.float32)
    x1, x2 = jnp.split(xf, 2, axis=-1)
    return jnp.concatenate([x1 * cos - x2 * sin, x2 * cos + x1 * sin], axis=-1).astype(x.dtype)


def _causal_attention(q, k, v):
    B, S, H, Dh = q.shape
    Dv = v.shape[-1]
    n_blocks = S // Q_BLOCK
    scale = Dh ** -0.5
    qb = q.reshape(B, n_blocks, Q_BLOCK, H, Dh).transpose(1, 0, 2, 3, 4)
    key_pos = jnp.arange(S)
    neg = jnp.finfo(jnp.float32).min

    def one_block(args):
        q_blk, blk = args
        s = jnp.einsum('bqhd,bkhd->bhqk', q_blk, k, preferred_element_type=jnp.float32) * scale
        q_pos = blk * Q_BLOCK + jnp.arange(Q_BLOCK)
        s = jnp.where(key_pos[None, :] <= q_pos[:, None], s, neg)
        p = jax.nn.softmax(s, axis=-1).astype(v.dtype)
        return jnp.einsum('bhqk,bkhe->bqhe', p, v)

    out = lax.map(one_block, (qb, jnp.arange(n_blocks)))
    return out.transpose(1, 0, 2, 3, 4).reshape(B, S, H, Dv)


def _mla(c_q, c_kv, k_rope, cos, sin, q_a_g, kv_a_g, w_uq, w_ukv, q_norm_g, k_norm_g):
    B, S, _ = c_q.shape
    c_q = _rmsnorm(c_q, q_a_g)
    c_kv = _rmsnorm(c_kv, kv_a_g)
    q = (c_q @ w_uq).reshape(B, S, MLA_HEADS, QK_HEAD_DIM)
    kv = (c_kv @ w_ukv).reshape(B, S, MLA_HEADS, QK_NOPE_DIM + V_HEAD_DIM)
    k_nope, v = kv[..., :QK_NOPE_DIM], kv[..., QK_NOPE_DIM:]
    k_r = jnp.broadcast_to(k_rope[:, :, None, :], (B, S, MLA_HEADS, QK_ROPE_DIM))
    k = jnp.concatenate([k_nope, k_r], axis=-1)
    q = _rmsnorm(q, q_norm_g)
    k = _rmsnorm(k, k_norm_g)
    q = jnp.concatenate([q[..., :QK_NOPE_DIM], _rope(q[..., QK_NOPE_DIM:], cos, sin)], axis=-1)
    k = jnp.concatenate([k[..., :QK_NOPE_DIM], _rope(k[..., QK_NOPE_DIM:], cos, sin)], axis=-1)
    out = _causal_attention(q, k, v)
    return out.reshape(B, S, MLA_WIDTH)


def _pool_mixer(u, pool_w, pool_scale):
    B, S, _ = u.shape
    uf = u.astype(jnp.float32).reshape(B, S, POOL_GROUPS, POOL_GROUP_DIM)
    cs = jnp.pad(jnp.cumsum(uf, axis=1), ((0, 0), (1, 0), (0, 0), (0, 0)))
    t = jnp.arange(S)[:, None]
    w = jnp.array(POOL_WINDOWS, dtype=jnp.int32)[None, :]
    start = jnp.maximum(t + 1 - w, 0)
    count = jnp.minimum(t + 1, w).astype(jnp.float32)
    lagged = cs[:, start, jnp.arange(POOL_GROUPS)[None, :]]
    mean = (cs[:, 1:] - lagged) / count[None, :, :, None]
    pooled = (mean - uf).astype(u.dtype)
    y = jnp.einsum('bsgc,gcd->bsgd', pooled, pool_w)
    y = y * pool_scale.reshape(POOL_GROUPS, POOL_GROUP_DIM)
    return y.reshape(B, S, POOL_WIDTH)


def _short_conv(x_normed, w_in, conv_w, w_out):
    S = x_normed.shape[1]
    gate_b, gate_c, u = jnp.split(x_normed @ w_in, 3, axis=-1)
    v = gate_c * u
    v_pad = jnp.pad(v, ((0, 0), (CONV_WIDTH - 1, 0), (0, 0)))
    conv = conv_w[0] * v_pad[:, 0:S]
    for j in range(1, CONV_WIDTH):
        conv = conv + conv_w[j] * v_pad[:, j:j + S]
    return (gate_b * conv) @ w_out


def _mlp(h, w_up, w_down):
    return jnp.square(jax.nn.relu(h @ w_up)) @ w_down


def setup_inputs(seed: int = 0) -> dict:
    key = jax.random.key(seed)
    ks = jax.random.split(key, 20)
    D = D_MODEL

    def nrm(k, shape, fan_in):
        return jax.random.normal(k, shape, jnp.float32) * (fan_in ** -0.5)

    def gain(k, shape):
        return 1.0 + 0.02 * jax.random.normal(k, shape, jnp.float32)

    x = jax.random.normal(ks[0], (BATCH, SEQ, D), jnp.float32)
    positions = jnp.broadcast_to(jnp.arange(SEQ, dtype=jnp.int32)[None, :], (BATCH, SEQ))
    return {
        "x": x,
        "positions": positions,
        "mix_norm_g": gain(ks[1], (DEPTH, D)),
        "mlp_norm_g": gain(ks[2], (DEPTH, D)),
        "w_mlp_up": nrm(ks[3], (DEPTH, D, D_FF), D),
        "w_mlp_down": nrm(ks[4], (DEPTH, D_FF, D), D_FF),
        "even_w_in": nrm(ks[5], (N_EVEN, D, EVEN_IN_DIM), D),
        "even_q_a_norm_g": gain(ks[6], (N_EVEN, Q_LORA_RANK)),
        "even_kv_a_norm_g": gain(ks[7], (N_EVEN, KV_LORA_RANK)),
        "even_w_uq": nrm(ks[8], (N_EVEN, Q_LORA_RANK, MLA_HEADS * QK_HEAD_DIM), Q_LORA_RANK),
        "even_w_ukv": nrm(ks[9], (N_EVEN, KV_LORA_RANK, MLA_HEADS * (QK_NOPE_DIM + V_HEAD_DIM)), KV_LORA_RANK),
        "even_q_norm_g": gain(ks[10], (N_EVEN, QK_HEAD_DIM)),
        "even_k_norm_g": gain(ks[11], (N_EVEN, QK_HEAD_DIM)),
        "even_pool_w": nrm(ks[12], (N_EVEN, POOL_GROUPS, POOL_GROUP_DIM, POOL_GROUP_DIM), POOL_GROUP_DIM),
        "even_pool_scale": gain(ks[13], (N_EVEN, POOL_WIDTH)),
        "even_w_out": nrm(ks[14], (N_EVEN, EVEN_MIX_DIM, D), EVEN_MIX_DIM),
        "odd_w_in": nrm(ks[15], (N_ODD, D, 3 * CONV_DIM), D),
        "odd_conv_w": nrm(ks[16], (N_ODD, CONV_WIDTH, CONV_DIM), CONV_WIDTH),
        "odd_w_out": nrm(ks[17], (N_ODD, CONV_DIM, D), CONV_DIM),
    }


def reference(x, positions, mix_norm_g, mlp_norm_g, w_mlp_up, w_mlp_down,
              even_w_in, even_q_a_norm_g, even_kv_a_norm_g, even_w_uq, even_w_ukv,
              even_q_norm_g, even_k_norm_g, even_pool_w, even_pool_scale, even_w_out,
              odd_w_in, odd_conv_w, odd_w_out):
    cos, sin = _rope_tables(positions)
    split_at = list(np.cumsum(EVEN_IN_SIZES)[:-1])
    for layer in range(DEPTH):
        h = _rmsnorm(x, mix_norm_g[layer])
        if layer % 2 == 0:
            e = layer // 2
            c_q, c_kv, k_rope, u_pool = jnp.split(h @ even_w_in[e], split_at, axis=-1)
            a = _mla(c_q, c_kv, k_rope, cos, sin, even_q_a_norm_g[e], even_kv_a_norm_g[e],
                     even_w_uq[e], even_w_ukv[e], even_q_norm_g[e], even_k_norm_g[e])
            b = _pool_mixer(u_pool, even_pool_w[e], even_pool_scale[e])
            x = x + jnp.concatenate([a, b], axis=-1) @ even_w_out[e]
        else:
            o = layer // 2
            x = x + _short_conv(h, odd_w_in[o], odd_conv_w[o], odd_w_out[o])
        x = x + _mlp(_rmsnorm(x, mlp_norm_g[layer]), w_mlp_up[layer], w_mlp_down[layer])
    return x
```

```python
import functools

import jax
import jax.numpy as jnp
import numpy as np
from jax import lax
from jax.experimental import pallas as pl
from jax.experimental.pallas import tpu as pltpu

F32 = jnp.float32
BF16 = jnp.bfloat16

RMS_EPS = 1e-6
ROPE_THETA = 10000.0
MLA_HEADS = 8
NOPE_DIM = 128
ROPE_DIM = 64
QK_DIM = NOPE_DIM + ROPE_DIM
V_DIM = 128
POOL_WINDOWS = (2, 4, 8, 16)
POOL_HISTORY = 16
CONV_WIDTH = 3
CONV_HISTORY = 8
LANES = 128

VMEM_LIMIT_BYTES = 56 * 1024 * 1024


def _rms(xf, g):
    ms = jnp.mean(xf * xf, axis=-1, keepdims=True)
    return xf * lax.rsqrt(ms + RMS_EPS) * g


def _dot(a, b):
    return jnp.dot(a, b, preferred_element_type=F32)


def _params(*sem):
    return pltpu.CompilerParams(dimension_semantics=sem, vmem_limit_bytes=VMEM_LIMIT_BYTES)


def _resident(block_shape, index_map):
    return pl.BlockSpec(block_shape, index_map, pipeline_mode=pl.Buffered(1))


def _rope_table_kernel(pos_ref, freq_ref, tab_ref):
    ang = pos_ref[...].astype(F32) * freq_ref[...]
    lane = lax.broadcasted_iota(jnp.int32, ang.shape, 1)
    sign = jnp.where((lane % ROPE_DIM) < ROPE_DIM // 2, -1.0, 1.0)
    tab_ref[:, :LANES] = jnp.cos(ang)
    tab_ref[:, LANES:] = jnp.sin(ang) * sign


def _rope_table_call(pos_col, tm):
    T = pos_col.shape[0]
    inv_freq = 1.0 / (ROPE_THETA ** (jnp.arange(0, ROPE_DIM, 2, dtype=F32) / ROPE_DIM))
    freq = jnp.tile(inv_freq, LANES // (ROPE_DIM // 2))[None, :]
    return pl.pallas_call(
        _rope_table_kernel,
        grid=(T // tm,),
        in_specs=[pl.BlockSpec((tm, 1), lambda i: (i, 0)),
                  pl.BlockSpec((1, LANES), lambda i: (0, 0))],
        out_specs=pl.BlockSpec((tm, 2 * LANES), lambda i: (i, 0)),
        out_shape=jax.ShapeDtypeStruct((T, 2 * LANES), F32),
        compiler_params=_params("parallel"),
        name="rope_table",
    )(pos_col, freq)


def _even_in_kernel(x_ref, g_ref, win_ref, qag_ref, kvag_ref, wuq_ref, wukv_ref, gains_ref,
                    poolw_ref, pscale_ref, tab_ref,
                    q_ref, k_ref, v_ref, b_ref, carry_ref, *, tm, lora, pool_width, scale):
    s = pl.program_id(1)
    hn = _rms(x_ref[...], g_ref[...]).astype(BF16)
    proj = _dot(hn, win_ref[...])
    off_pool = 2 * lora
    off_kr = off_pool + pool_width
    cqn = _rms(proj[:, :lora], qag_ref[...]).astype(BF16)
    ckvn = _rms(proj[:, lora:2 * lora], kvag_ref[...]).astype(BF16)
    u = proj[:, off_pool:off_kr]
    kr2 = proj[:, off_kr:off_kr + LANES]
    krs2 = proj[:, off_kr + LANES:off_kr + 2 * LANES]

    qall = _dot(cqn, wuq_ref[...])
    kvall = _dot(ckvn, wukv_ref[...])

    cos2 = tab_ref[:, :LANES]
    sin2 = tab_ref[:, LANES:]
    gains = gains_ref[...]
    qg_nope, qg_rope2, qg_ropes2 = gains[0:1], gains[1:2], gains[2:3]
    kg_nope, kg_rope2, kg_ropes2 = gains[3:4], gains[4:5], gains[5:6]
    lane = lax.broadcasted_iota(jnp.int32, (tm, LANES), 1)
    low_half = lane < ROPE_DIM
    nheads = MLA_HEADS
    nope_w = nheads * NOPE_DIM

    kr_sq = jnp.where(low_half, kr2 * kr2, 0.0)
    k_roped2 = kr2 * kg_rope2 * cos2 + krs2 * kg_ropes2 * sin2
    for h in range(nheads):
        kn = kvall[:, h * NOPE_DIM:(h + 1) * NOPE_DIM]
        ss = jnp.sum(kn * kn + kr_sq, axis=-1, keepdims=True)
        r = lax.rsqrt(ss * (1.0 / QK_DIM) + RMS_EPS)
        k_ref[0, h, :, 0:NOPE_DIM] = (kn * r * kg_nope).astype(BF16)
        k_ref[0, h, :, NOPE_DIM:QK_DIM] = (k_roped2 * r)[:, :ROPE_DIM].astype(BF16)
        v_ref[0, h] = kvall[:, nope_w + h * V_DIM:nope_w + (h + 1) * V_DIM].astype(BF16)

    rope_w = nheads * ROPE_DIM
    for p in range(nheads // 2):
        qr2 = qall[:, nope_w + p * LANES:nope_w + (p + 1) * LANES]
        qrs2 = qall[:, nope_w + rope_w + p * LANES:nope_w + rope_w + (p + 1) * LANES]
        sq = qr2 * qr2
        roped = qr2 * qg_rope2 * cos2 + qrs2 * qg_ropes2 * sin2
        for e in range(2):
            h = 2 * p + e
            qn = qall[:, h * NOPE_DIM:(h + 1) * NOPE_DIM]
            mine = low_half if e == 0 else jnp.logical_not(low_half)
            ss = jnp.sum(qn * qn + jnp.where(mine, sq, 0.0), axis=-1, keepdims=True)
            r = lax.rsqrt(ss * (1.0 / QK_DIM) + RMS_EPS) * scale
            q_ref[0, h, :, 0:NOPE_DIM] = (qn * r * qg_nope).astype(BF16)
            rp = roped * r
            if e == 1:
                rp = pltpu.roll(rp, ROPE_DIM, axis=1)
            q_ref[0, h, :, NOPE_DIM:QK_DIM] = rp[:, :ROPE_DIM].astype(BF16)

    prev = jnp.where(s == 0, 0.0, carry_ref[...])
    carry_ref[...] = u[tm - POOL_HISTORY:tm]
    ext = jnp.concatenate([prev, u], axis=0)
    t_in_seq = s * tm + lax.broadcasted_iota(jnp.int32, (tm, 1), 0)
    gd = pool_width // len(POOL_WINDOWS)
    pscale = pscale_ref[...]
    for gi, w in enumerate(POOL_WINDOWS):
        e = ext[:, gi * gd:(gi + 1) * gd]
        sh = 1
        while sh < w:
            e = e + pltpu.roll(e, sh, axis=0)
            sh *= 2
        inv_cnt = 1.0 / jnp.minimum(t_in_seq + 1, w).astype(F32)
        pooled = (e[POOL_HISTORY:] * inv_cnt - u[:, gi * gd:(gi + 1) * gd]).astype(BF16)
        y = _dot(pooled, poolw_ref[gi]) * pscale[:, gi * gd:(gi + 1) * gd]
        b_ref[:, gi * gd:(gi + 1) * gd] = y.astype(BF16)


def _even_in_call(xf, g, win, qag, kvag, wuq, wukv, gains, poolw, pscale, tab, *, B, S, tm):
    T, D = xf.shape
    lora = qag.shape[1]
    pool_width = pscale.shape[1]
    n_s = S // tm
    H = MLA_HEADS
    gd = pool_width // len(POOL_WINDOWS)
    tok = lambda b, s: (b * n_s + s, 0)
    const2 = lambda b, s: (0, 0)
    kern = functools.partial(_even_in_kernel, tm=tm, lora=lora, pool_width=pool_width,
                             scale=QK_DIM ** -0.5)
    return pl.pallas_call(
        kern,
        grid=(B, n_s),
        in_specs=[
            pl.BlockSpec((tm, D), tok),
            _resident((1, D), const2),
            _resident(win.shape, const2),
            _resident((1, lora), const2),
            _resident((1, lora), const2),
            _resident(wuq.shape, const2),
            _resident(wukv.shape, const2),
            _resident(gains.shape, const2),
            _resident(poolw.shape, lambda b, s: (0, 0, 0)),
            _resident((1, pool_width), const2),
            pl.BlockSpec((tm, 2 * LANES), tok),
        ],
        out_specs=[
            pl.BlockSpec((1, H, tm, QK_DIM), lambda b, s: (b, 0, s, 0)),
            pl.BlockSpec((1, H, tm, QK_DIM), lambda b, s: (b, 0, s, 0)),
            pl.BlockSpec((1, H, tm, V_DIM), lambda b, s: (b, 0, s, 0)),
            pl.BlockSpec((tm, pool_width), tok),
        ],
        out_shape=[
            jax.ShapeDtypeStruct((B, H, S, QK_DIM), BF16),
            jax.ShapeDtypeStruct((B, H, S, QK_DIM), BF16),
            jax.ShapeDtypeStruct((B, H, S, V_DIM), BF16),
            jax.ShapeDtypeStruct((T, pool_width), BF16),
        ],
        scratch_shapes=[pltpu.VMEM((POOL_HISTORY, pool_width), F32)],
        compiler_params=_params("arbitrary", "arbitrary"),
        name="even_in",
    )(xf, g, win, qag, kvag, wuq, wukv, gains, poolw, pscale, tab)


def _attn_kernel(q_ref, k_ref, v_ref, o_ref, m_ref, l_ref, acc_ref, *, tq):
    i = pl.program_id(2)
    q = q_ref[0, 0]
    m_ref[...] = jnp.full(m_ref.shape, -jnp.inf, F32)
    l_ref[...] = jnp.zeros(l_ref.shape, F32)
    acc_ref[...] = jnp.zeros(acc_ref.shape, F32)

    def block(j, masked):
        start = pl.multiple_of(j * tq, tq)
        kb = k_ref[0, 0, pl.ds(start, tq), :]
        vb = v_ref[0, 0, pl.ds(start, tq), :]
        s = lax.dot_general(q, kb, (((1,), (1,)), ((), ())), preferred_element_type=F32)
        if masked:
            row = lax.broadcasted_iota(jnp.int32, s.shape, 0)
            col = lax.broadcasted_iota(jnp.int32, s.shape, 1)
            s = jnp.where(col <= row, s, -1e30)
        m_prev = m_ref[...]
        m_new = jnp.maximum(m_prev, jnp.max(s, axis=-1, keepdims=True))
        alpha = jnp.exp(m_prev - m_new)
        p = jnp.exp(s - m_new)
        l_ref[...] = alpha * l_ref[...] + jnp.sum(p, axis=-1, keepdims=True)
        acc_ref[...] = alpha * acc_ref[...] + _dot(p.astype(BF16), vb)
        m_ref[...] = m_new

    def body(j, carry):
        block(j, False)
        return carry

    lax.fori_loop(0, i, body, 0)
    block(i, True)
    o_ref[0] = (acc_ref[...] / l_ref[...]).astype(BF16)


def _attn_call(q, k, v, *, tq):
    B, H, S, _ = q.shape
    return pl.pallas_call(
        functools.partial(_attn_kernel, tq=tq),
        grid=(B, H, S // tq),
        in_specs=[
            pl.BlockSpec((1, 1, tq, QK_DIM), lambda b, h, i: (b, h, i, 0)),
            pl.BlockSpec((1, 1, S, QK_DIM), lambda b, h, i: (b, h, 0, 0)),
            pl.BlockSpec((1, 1, S, V_DIM), lambda b, h, i: (b, h, 0, 0)),
        ],
        out_specs=pl.BlockSpec((1, tq, V_DIM), lambda b, h, i: (b, i, h)),
        out_shape=jax.ShapeDtypeStruct((B, S, H * V_DIM), BF16),
        scratch_shapes=[pltpu.VMEM((tq, 1), F32), pltpu.VMEM((tq, 1), F32),
                        pltpu.VMEM((tq, V_DIM), F32)],
        compiler_params=_params("parallel", "parallel", "arbitrary"),
        name="attn",
    )(q, k, v)


def _outproj_kernel(x_ref, a_ref, b_ref, wa_ref, wb_ref, o_ref):
    o_ref[...] = x_ref[...] + _dot(a_ref[...], wa_ref[...]) + _dot(b_ref[...], wb_ref[...])


def _outproj_call(xf, a, b, wout, e, *, tm):
    T, D = xf.shape
    wa_rows = a.shape[1]
    wb_rows = b.shape[1]
    assert wa_rows == wb_rows
    return pl.pallas_call(
        _outproj_kernel,
        grid=(T // tm,),
        in_specs=[
            pl.BlockSpec((tm, D), lambda i: (i, 0)),
            pl.BlockSpec((tm, wa_rows), lambda i: (i, 0)),
            pl.BlockSpec((tm, wb_rows), lambda i: (i, 0)),
            _resident((None, wa_rows, D), lambda i: (e, 0, 0)),
            _resident((None, wb_rows, D), lambda i: (e, 1, 0)),
        ],
        out_specs=pl.BlockSpec((tm, D), lambda i: (i, 0)),
        out_shape=jax.ShapeDtypeStruct((T, D), F32),
        compiler_params=_params("parallel"),
        name="outproj",
    )(xf, a, b, wout, wout)


def _conv_kernel(x_ref, g_ref, wb_ref, wc_ref, wu_ref, cw_ref, wo_ref, o_ref, hn_ref, carry_ref,
                 *, tm, tiles_per_seq):
    i = pl.program_id(0)
    c = pl.program_id(1)

    @pl.when(c == 0)
    def _():
        x = x_ref[...]
        hn_ref[...] = _rms(x, g_ref[...]).astype(BF16)
        o_ref[...] = x

    hn = hn_ref[...]
    gate_b = _dot(hn, wb_ref[...])
    v = _dot(hn, wc_ref[...]) * _dot(hn, wu_ref[...])
    prev = jnp.where(i % tiles_per_seq == 0, 0.0, carry_ref[c])
    carry_ref[c] = v[tm - CONV_HISTORY:tm]
    ext = jnp.concatenate([prev, v], axis=0)
    cw = cw_ref[...]
    conv = cw[CONV_WIDTH - 1:CONV_WIDTH] * v
    for back in range(1, CONV_WIDTH):
        tap = CONV_WIDTH - 1 - back
        conv = conv + cw[tap:tap + 1] * pltpu.roll(ext, back, axis=0)[CONV_HISTORY:]
    o_ref[...] += _dot((gate_b * conv).astype(BF16), wo_ref[...])


def _conv_call(xf, g, win, convw, wout, o, *, S, tm, tc):
    T, D = xf.shape
    C = wout.shape[1]
    n_c = C // tc
    kern = functools.partial(_conv_kernel, tm=tm, tiles_per_seq=S // tm)
    return pl.pallas_call(
        kern,
        grid=(T // tm, n_c),
        in_specs=[
            pl.BlockSpec((tm, D), lambda i, c: (i, 0)),
            pl.BlockSpec((1, D), lambda i, c: (0, 0)),
            pl.BlockSpec((None, D, tc), lambda i, c: (o, 0, c)),
            pl.BlockSpec((None, D, tc), lambda i, c: (o, 0, n_c + c)),
            pl.BlockSpec((None, D, tc), lambda i, c: (o, 0, 2 * n_c + c)),
            pl.BlockSpec((None, CONV_WIDTH, tc), lambda i, c: (o, 0, c)),
            pl.BlockSpec((None, tc, D), lambda i, c: (o, c, 0)),
        ],
        out_specs=pl.BlockSpec((tm, D), lambda i, c: (i, 0)),
        out_shape=jax.ShapeDtypeStruct((T, D), F32),
        scratch_shapes=[pltpu.VMEM((tm, D), BF16),
                        pltpu.VMEM((n_c, CONV_HISTORY, tc), F32)],
        compiler_params=_params("arbitrary", "arbitrary"),
        name="conv_mixer",
    )(xf, g, win, win, win, convw, wout)


def _mlp_kernel(x_ref, g_ref, wup_ref, wdn_ref, o_ref, hn_ref):
    @pl.when(pl.program_id(1) == 0)
    def _():
        x = x_ref[...]
        hn_ref[...] = _rms(x, g_ref[...]).astype(BF16)
        o_ref[...] = x

    up = _dot(hn_ref[...], wup_ref[...])
    act = jnp.square(jnp.maximum(up, 0.0)).astype(BF16)
    o_ref[...] += _dot(act, wdn_ref[...])


def _mlp_call(xf, g, wup, wdn, layer, *, tm, tf):
    T, D = xf.shape
    F = wup.shape[2]
    return pl.pallas_call(
        _mlp_kernel,
        grid=(T // tm, F // tf),
        in_specs=[
            pl.BlockSpec((tm, D), lambda i, f: (i, 0)),
            pl.BlockSpec((1, D), lambda i, f: (0, 0)),
            pl.BlockSpec((None, D, tf), lambda i, f: (layer, 0, f)),
            pl.BlockSpec((None, tf, D), lambda i, f: (layer, f, 0)),
        ],
        out_specs=pl.BlockSpec((tm, D), lambda i, f: (i, 0)),
        out_shape=jax.ShapeDtypeStruct((T, D), F32),
        scratch_shapes=[pltpu.VMEM((tm, D), BF16)],
        compiler_params=_params("parallel", "arbitrary"),
        name="mlp",
    )(xf, g, wup, wdn)


def _swap_halves_idx(n):
    return (np.arange(n) + n // 2) % n


def _prep_even_weights(even_w_in, even_w_uq, even_w_ukv, even_q_norm_g, even_k_norm_g, lora, pool_width):
    H = MLA_HEADS
    off_kr = 2 * lora
    off_pool = off_kr + ROPE_DIM
    kr = off_kr + np.arange(ROPE_DIM)
    krs = off_kr + _swap_halves_idx(ROPE_DIM)
    in_perm = np.concatenate([np.arange(off_kr), off_pool + np.arange(pool_width), kr, kr, krs, krs])
    win = even_w_in[:, :, in_perm].astype(BF16)

    head0 = np.arange(H)[:, None] * QK_DIM
    q_nope = (head0 + np.arange(NOPE_DIM)[None, :]).reshape(-1)
    q_rope = (head0 + NOPE_DIM + np.arange(ROPE_DIM)[None, :]).reshape(-1)
    q_ropes = (head0 + NOPE_DIM + _swap_halves_idx(ROPE_DIM)[None, :]).reshape(-1)
    wuq = even_w_uq[:, :, np.concatenate([q_nope, q_rope, q_ropes])].astype(BF16)

    kv0 = np.arange(H)[:, None] * (NOPE_DIM + V_DIM)
    k_nope = (kv0 + np.arange(NOPE_DIM)[None, :]).reshape(-1)
    v_cols = (kv0 + NOPE_DIM + np.arange(V_DIM)[None, :]).reshape(-1)
    wukv = even_w_ukv[:, :, np.concatenate([k_nope, v_cols])].astype(BF16)

    def gain_rows(gvec):
        rope = gvec[:, NOPE_DIM:]
        ropes = rope[:, _swap_halves_idx(ROPE_DIM)]
        return [gvec[:, :NOPE_DIM], jnp.concatenate([rope, rope], -1), jnp.concatenate([ropes, ropes], -1)]

    rows = gain_rows(even_q_norm_g) + gain_rows(even_k_norm_g)
    rows += [jnp.zeros_like(rows[0])] * 2
    gains = jnp.stack(rows, axis=1).astype(F32)
    return win, wuq, wukv, gains


def _pick_tile(n, want):
    t = min(n, want)
    while n % t:
        t //= 2
    return t


def kernel(x, positions, mix_norm_g, mlp_norm_g, w_mlp_up, w_mlp_down, even_w_in, even_q_a_norm_g,
           even_kv_a_norm_g, even_w_uq, even_w_ukv, even_q_norm_g, even_k_norm_g, even_pool_w,
           even_pool_scale, even_w_out, odd_w_in, odd_conv_w, odd_w_out):
    B, S, D = x.shape
    T = B * S
    depth = mix_norm_g.shape[0]
    lora = even_q_a_norm_g.shape[1]
    pool_width = even_pool_scale.shape[1]

    tm_mlp = _pick_tile(S, 512)
    tf_mlp = _pick_tile(w_mlp_up.shape[2], 1024)
    tm_even = _pick_tile(S, 256)
    tq = _pick_tile(S, 512)
    tc_conv = _pick_tile(odd_w_out.shape[1], 512)

    wup = w_mlp_up.astype(BF16)
    wdn = w_mlp_down.astype(BF16)
    win_e, wuq, wukv, gains = _prep_even_weights(even_w_in, even_w_uq, even_w_ukv, even_q_norm_g,
                                                 even_k_norm_g, lora, pool_width)
    poolw = even_pool_w.astype(BF16)
    wout_e = even_w_out.astype(BF16)
    win_o = odd_w_in.astype(BF16)
    wout_o = odd_w_out.astype(BF16)

    tab = _rope_table_call(positions.reshape(T, 1), _pick_tile(T, 1024))

    xf = x.reshape(T, D)
    for layer in range(depth):
        g_mix = mix_norm_g[layer][None, :]
        if layer % 2 == 0:
            e = layer // 2
            q, k, v, b = _even_in_call(
                xf, g_mix, win_e[e], even_q_a_norm_g[e][None, :], even_kv_a_norm_g[e][None, :],
                wuq[e], wukv[e], gains[e], poolw[e], even_pool_scale[e][None, :], tab,
                B=B, S=S, tm=tm_even)
            a = _attn_call(q, k, v, tq=tq).reshape(T, MLA_HEADS * V_DIM)
            xf = _outproj_call(xf, a, b, wout_e, e, tm=tm_mlp)
        else:
            o = layer // 2
            xf = _conv_call(xf, g_mix, win_o, odd_conv_w, wout_o, o, S=S, tm=tm_mlp, tc=tc_conv)
        xf = _mlp_call(xf, mlp_norm_g[layer][None, :], wup, wdn, layer, tm=tm_mlp, tf=tf_mlp)
    return xf.reshape(B, S, D)
```

```python
import functools

import jax
import jax.numpy as jnp
import numpy as np
from jax import lax
from jax.experimental import pallas as pl
from jax.experimental.pallas import tpu as pltpu

F32 = jnp.float32
BF16 = jnp.bfloat16

RMS_EPS = 1e-6
ROPE_THETA = 10000.0
MLA_HEADS = 8
NOPE_DIM = 128
ROPE_DIM = 64
QK_DIM = NOPE_DIM + ROPE_DIM
V_DIM = 128
VT_ROWS = V_DIM + 16
LOG2_E = 1.4426950408889634
POOL_WINDOWS = (2, 4, 8, 16)
POOL_HISTORY = 16
CONV_WIDTH = 3
CONV_HISTORY = 8
LANES = 128

VMEM_LIMIT_BYTES = 56 * 1024 * 1024


def _rms(xf, g):
    ms = jnp.mean(xf * xf, axis=-1, keepdims=True)
    return xf * lax.rsqrt(ms + RMS_EPS) * g


def _dot(a, b):
    return jnp.dot(a, b, preferred_element_type=F32)


def _params(*sem):
    return pltpu.CompilerParams(dimension_semantics=sem, vmem_limit_bytes=VMEM_LIMIT_BYTES)


def _resident(block_shape, index_map):
    return pl.BlockSpec(block_shape, index_map, pipeline_mode=pl.Buffered(1))


def _rope_table_kernel(pos_ref, freq_ref, tab_ref):
    ang = pos_ref[...].astype(F32) * freq_ref[...]
    lane = lax.broadcasted_iota(jnp.int32, ang.shape, 1)
    sign = jnp.where((lane % ROPE_DIM) < ROPE_DIM // 2, -1.0, 1.0)
    tab_ref[:, :LANES] = jnp.cos(ang)
    tab_ref[:, LANES:] = jnp.sin(ang) * sign


def _rope_table_call(pos_col, tm):
    T = pos_col.shape[0]
    inv_freq = 1.0 / (ROPE_THETA ** (jnp.arange(0, ROPE_DIM, 2, dtype=F32) / ROPE_DIM))
    freq = jnp.tile(inv_freq, LANES // (ROPE_DIM // 2))[None, :]
    return pl.pallas_call(
        _rope_table_kernel,
        grid=(T // tm,),
        in_specs=[pl.BlockSpec((tm, 1), lambda i: (i, 0)),
                  pl.BlockSpec((1, LANES), lambda i: (0, 0))],
        out_specs=pl.BlockSpec((tm, 2 * LANES), lambda i: (i, 0)),
        out_shape=jax.ShapeDtypeStruct((T, 2 * LANES), F32),
        compiler_params=_params("parallel"),
        name="rope_table",
    )(pos_col, freq)


def _even_in_kernel(x_ref, g_ref, win_ref, qag_ref, kvag_ref, wuq_ref, wuk_ref, wuvt_ref, gains_ref,
                    poolw_ref, pscale_ref, tab_ref,
                    q_ref, k_ref, vt_ref, b_ref, carry_ref, *, tm, lora, pool_width, scale):
    s = pl.program_id(1)
    hn = _rms(x_ref[...], g_ref[...]).astype(BF16)
    proj = _dot(hn, win_ref[...])
    off_pool = 2 * lora
    off_kr = off_pool + pool_width
    cqn = _rms(proj[:, :lora], qag_ref[...]).astype(BF16)
    ckvn = _rms(proj[:, lora:2 * lora], kvag_ref[...]).astype(BF16)
    u = proj[:, off_pool:off_kr]
    kr2 = proj[:, off_kr:off_kr + LANES]
    krs2 = proj[:, off_kr + LANES:off_kr + 2 * LANES]

    qall = _dot(cqn, wuq_ref[...])
    knall = _dot(ckvn, wuk_ref[...])
    vt_all = lax.dot_general(wuvt_ref[...], ckvn, (((1,), (1,)), ((), ())), preferred_element_type=F32)

    cos2 = tab_ref[:, :LANES]
    sin2 = tab_ref[:, LANES:]
    gains = gains_ref[...]
    qg_nope, qg_rope2, qg_ropes2 = gains[0:1], gains[1:2], gains[2:3]
    kg_nope, kg_rope2, kg_ropes2 = gains[3:4], gains[4:5], gains[5:6]
    lane = lax.broadcasted_iota(jnp.int32, (tm, LANES), 1)
    low_half = lane < ROPE_DIM
    nheads = MLA_HEADS
    nope_w = nheads * NOPE_DIM

    kr_sq = jnp.where(low_half, kr2 * kr2, 0.0)
    k_roped2 = kr2 * kg_rope2 * cos2 + krs2 * kg_ropes2 * sin2
    for h in range(nheads):
        kn = knall[:, h * NOPE_DIM:(h + 1) * NOPE_DIM]
        ss = jnp.sum(kn * kn + kr_sq, axis=-1, keepdims=True)
        r = lax.rsqrt(ss * (1.0 / QK_DIM) + RMS_EPS)
        k_ref[0, h, :, 0:NOPE_DIM] = (kn * r * kg_nope).astype(BF16)
        k_ref[0, h, :, NOPE_DIM:QK_DIM] = (k_roped2 * r)[:, :ROPE_DIM].astype(BF16)
        vt_ref[0, h, 0, 0:V_DIM, :] = vt_all[h * V_DIM:(h + 1) * V_DIM].astype(BF16)
        vt_ref[0, h, 0, V_DIM:VT_ROWS, :] = jnp.ones((VT_ROWS - V_DIM, tm), BF16)

    rope_w = nheads * ROPE_DIM
    for p in range(nheads // 2):
        qr2 = qall[:, nope_w + p * LANES:nope_w + (p + 1) * LANES]
        qrs2 = qall[:, nope_w + rope_w + p * LANES:nope_w + rope_w + (p + 1) * LANES]
        sq = qr2 * qr2
        roped = qr2 * qg_rope2 * cos2 + qrs2 * qg_ropes2 * sin2
        for e in range(2):
            h = 2 * p + e
            qn = qall[:, h * NOPE_DIM:(h + 1) * NOPE_DIM]
            mine = low_half if e == 0 else jnp.logical_not(low_half)
            ss = jnp.sum(qn * qn + jnp.where(mine, sq, 0.0), axis=-1, keepdims=True)
            r = lax.rsqrt(ss * (1.0 / QK_DIM) + RMS_EPS) * scale
            q_ref[0, h, :, 0:NOPE_DIM] = (qn * r * qg_nope).astype(BF16)
            rp = roped * r
            if e == 1:
                rp = pltpu.roll(rp, ROPE_DIM, axis=1)
            q_ref[0, h, :, NOPE_DIM:QK_DIM] = rp[:, :ROPE_DIM].astype(BF16)

    prev = jnp.where(s == 0, 0.0, carry_ref[...])
    carry_ref[...] = u[tm - POOL_HISTORY:tm]
    ext = jnp.concatenate([prev, u], axis=0)
    t_in_seq = s * tm + lax.broadcasted_iota(jnp.int32, (tm, 1), 0)
    gd = pool_width // len(POOL_WINDOWS)
    pscale = pscale_ref[...]
    for gi, w in enumerate(POOL_WINDOWS):
        e = ext[:, gi * gd:(gi + 1) * gd]
        sh = 1
        while sh < w:
            e = e + pltpu.roll(e, sh, axis=0)
            sh *= 2
        inv_cnt = 1.0 / jnp.minimum(t_in_seq + 1, w).astype(F32)
        pooled = (e[POOL_HISTORY:] * inv_cnt - u[:, gi * gd:(gi + 1) * gd]).astype(BF16)
        y = _dot(pooled, poolw_ref[gi]) * pscale[:, gi * gd:(gi + 1) * gd]
        b_ref[:, gi * gd:(gi + 1) * gd] = y.astype(BF16)


def _even_in_call(xf, g, win, qag, kvag, wuq, wuk, wuvt, gains, poolw, pscale, tab, *, B, S, tm, tk):
    T, D = xf.shape
    lora = qag.shape[1]
    pool_width = pscale.shape[1]
    n_s = S // tm
    H = MLA_HEADS
    per_kv_block = tk // tm
    tok = lambda b, s: (b * n_s + s, 0)
    const2 = lambda b, s: (0, 0)
    kern = functools.partial(_even_in_kernel, tm=tm, lora=lora, pool_width=pool_width,
                             scale=QK_DIM ** -0.5 * LOG2_E)
    return pl.pallas_call(
        kern,
        grid=(B, n_s),
        in_specs=[
            pl.BlockSpec((tm, D), tok),
            _resident((1, D), const2),
            _resident(win.shape, const2),
            _resident((1, lora), const2),
            _resident((1, lora), const2),
            _resident(wuq.shape, const2),
            _resident(wuk.shape, const2),
            _resident(wuvt.shape, const2),
            _resident(gains.shape, const2),
            _resident(poolw.shape, lambda b, s: (0, 0, 0)),
            _resident((1, pool_width), const2),
            pl.BlockSpec((tm, 2 * LANES), tok),
        ],
        out_specs=[
            pl.BlockSpec((1, H, tm, QK_DIM), lambda b, s: (b, 0, s, 0)),
            pl.BlockSpec((1, H, tm, QK_DIM), lambda b, s: (b, 0, s, 0)),
            pl.BlockSpec((1, H, 1, VT_ROWS, tm),
                         lambda b, s: (b, 0, s // per_kv_block, 0, s % per_kv_block)),
            pl.BlockSpec((tm, pool_width), tok),
        ],
        out_shape=[
            jax.ShapeDtypeStruct((B, H, S, QK_DIM), BF16),
            jax.ShapeDtypeStruct((B, H, S, QK_DIM), BF16),
            jax.ShapeDtypeStruct((B, H, S // tk, VT_ROWS, tk), BF16),
            jax.ShapeDtypeStruct((T, pool_width), BF16),
        ],
        scratch_shapes=[pltpu.VMEM((POOL_HISTORY, pool_width), F32)],
        compiler_params=_params("arbitrary", "arbitrary"),
        name="even_in",
    )(xf, g, win, qag, kvag, wuq, wuk, wuvt, gains, poolw, pscale, tab)


def _attn_kernel(q_ref, k_ref, vt_ref, o_ref, m_ref, acc_ref, *, tq):
    i = pl.program_id(2)
    q = q_ref[0, 0]
    m_ref[...] = jnp.full(m_ref.shape, -jnp.inf, F32)
    acc_ref[...] = jnp.zeros(acc_ref.shape, F32)

    def scores(j):
        start = pl.multiple_of(j * tq, tq)
        kb = k_ref[0, 0, pl.ds(start, tq), :]
        return lax.dot_general(kb, q, (((1,), (1,)), ((), ())), preferred_element_type=F32)

    def accumulate(s, j, masked):
        if masked:
            key = lax.broadcasted_iota(jnp.int32, s.shape, 0)
            qry = lax.broadcasted_iota(jnp.int32, s.shape, 1)
            s = jnp.where(key <= qry, s, -1e30)
        m_prev = m_ref[...]
        m_new = jnp.maximum(m_prev, jnp.max(s, axis=0, keepdims=True))
        alpha = jnp.exp2(m_prev - m_new)
        p = jnp.exp2(s - m_new).astype(BF16)
        acc_ref[...] = alpha * acc_ref[...] + _dot(vt_ref[0, 0, j], p)
        m_ref[...] = m_new

    def body(j, s_cur):
        s_next = scores(j + 1)
        accumulate(s_cur, j, False)
        return s_next

    s_last = lax.fori_loop(0, i, body, scores(0))
    accumulate(s_last, i, True)
    acc = acc_ref[...]
    out_t = acc[0:V_DIM] / acc[V_DIM:V_DIM + 1]
    o_ref[0] = out_t.T.astype(BF16)


def _attn_call(q, k, vt, *, tq):
    B, H, S, _ = q.shape
    n_kv = vt.shape[2]
    assert vt.shape[4] == tq
    return pl.pallas_call(
        functools.partial(_attn_kernel, tq=tq),
        grid=(B, H, S // tq),
        in_specs=[
            pl.BlockSpec((1, 1, tq, QK_DIM), lambda b, h, i: (b, h, i, 0)),
            pl.BlockSpec((1, 1, S, QK_DIM), lambda b, h, i: (b, h, 0, 0)),
            pl.BlockSpec((1, 1, n_kv, VT_ROWS, tq), lambda b, h, i: (b, h, 0, 0, 0)),
        ],
        out_specs=pl.BlockSpec((1, tq, V_DIM), lambda b, h, i: (b, i, h)),
        out_shape=jax.ShapeDtypeStruct((B, S, H * V_DIM), BF16),
        scratch_shapes=[pltpu.VMEM((1, tq), F32), pltpu.VMEM((VT_ROWS, tq), F32)],
        compiler_params=_params("parallel", "parallel", "arbitrary"),
        name="attn",
    )(q, k, vt)


def _outproj_kernel(x_ref, a_ref, b_ref, wa_ref, wb_ref, o_ref):
    o_ref[...] = x_ref[...] + _dot(a_ref[...], wa_ref[...]) + _dot(b_ref[...], wb_ref[...])


def _outproj_call(xf, a, b, wout, e, *, tm):
    T, D = xf.shape
    wa_rows = a.shape[1]
    wb_rows = b.shape[1]
    assert wa_rows == wb_rows
    return pl.pallas_call(
        _outproj_kernel,
        grid=(T // tm,),
        in_specs=[
            pl.BlockSpec((tm, D), lambda i: (i, 0)),
            pl.BlockSpec((tm, wa_rows), lambda i: (i, 0)),
            pl.BlockSpec((tm, wb_rows), lambda i: (i, 0)),
            _resident((None, wa_rows, D), lambda i: (e, 0, 0)),
            _resident((None, wb_rows, D), lambda i: (e, 1, 0)),
        ],
        out_specs=pl.BlockSpec((tm, D), lambda i: (i, 0)),
        out_shape=jax.ShapeDtypeStruct((T, D), F32),
        compiler_params=_params("parallel"),
        name="outproj",
    )(xf, a, b, wout, wout)


def _conv_kernel(x_ref, g_ref, wb_ref, wc_ref, wu_ref, cw_ref, wo_ref, o_ref, hn_ref, carry_ref,
                 *, tm, tiles_per_seq):
    i = pl.program_id(0)
    c = pl.program_id(1)

    @pl.when(c == 0)
    def _():
        x = x_ref[...]
        hn_ref[...] = _rms(x, g_ref[...]).astype(BF16)
        o_ref[...] = x

    hn = hn_ref[...]
    gate_b = _dot(hn, wb_ref[...])
    v = _dot(hn, wc_ref[...]) * _dot(hn, wu_ref[...])
    prev = jnp.where(i % tiles_per_seq == 0, 0.0, carry_ref[c])
    carry_ref[c] = v[tm - CONV_HISTORY:tm]
    ext = jnp.concatenate([prev, v], axis=0)
    cw = cw_ref[...]
    conv = cw[CONV_WIDTH - 1:CONV_WIDTH] * v
    for back in range(1, CONV_WIDTH):
        tap = CONV_WIDTH - 1 - back
        conv = conv + cw[tap:tap + 1] * pltpu.roll(ext, back, axis=0)[CONV_HISTORY:]
    o_ref[...] += _dot((gate_b * conv).astype(BF16), wo_ref[...])


def _conv_call(xf, g, win, convw, wout, o, *, S, tm, tc):
    T, D = xf.shape
    C = wout.shape[1]
    n_c = C // tc
    kern = functools.partial(_conv_kernel, tm=tm, tiles_per_seq=S // tm)
    return pl.pallas_call(
        kern,
        grid=(T // tm, n_c),
        in_specs=[
            pl.BlockSpec((tm, D), lambda i, c: (i, 0)),
            pl.BlockSpec((1, D), lambda i, c: (0, 0)),
            pl.BlockSpec((None, D, tc), lambda i, c: (o, 0, c)),
            pl.BlockSpec((None, D, tc), lambda i, c: (o, 0, n_c + c)),
            pl.BlockSpec((None, D, tc), lambda i, c: (o, 0, 2 * n_c + c)),
            pl.BlockSpec((None, CONV_WIDTH, tc), lambda i, c: (o, 0, c)),
            pl.BlockSpec((None, tc, D), lambda i, c: (o, c, 0)),
        ],
        out_specs=pl.BlockSpec((tm, D), lambda i, c: (i, 0)),
        out_shape=jax.ShapeDtypeStruct((T, D), F32),
        scratch_shapes=[pltpu.VMEM((tm, D), BF16),
                        pltpu.VMEM((n_c, CONV_HISTORY, tc), F32)],
        compiler_params=_params("arbitrary", "arbitrary"),
        name="conv_mixer",
    )(xf, g, win, win, win, convw, wout)


def _mlp_kernel(x_ref, g_ref, wup_ref, wdn_ref, o_ref, hn_ref):
    @pl.when(pl.program_id(1) == 0)
    def _():
        x = x_ref[...]
        hn_ref[...] = _rms(x, g_ref[...]).astype(BF16)
        o_ref[...] = x

    up = _dot(hn_ref[...], wup_ref[...])
    act = jnp.square(jnp.maximum(up, 0.0)).astype(BF16)
    o_ref[...] += _dot(act, wdn_ref[...])


def _mlp_call(xf, g, wup, wdn, layer, *, tm, tf):
    T, D = xf.shape
    F = wup.shape[2]
    return pl.pallas_call(
        _mlp_kernel,
        grid=(T // tm, F // tf),
        in_specs=[
            pl.BlockSpec((tm, D), lambda i, f: (i, 0)),
            pl.BlockSpec((1, D), lambda i, f: (0, 0)),
            pl.BlockSpec((None, D, tf), lambda i, f: (layer, 0, f)),
            pl.BlockSpec((None, tf, D), lambda i, f: (layer, f, 0)),
        ],
        out_specs=pl.BlockSpec((tm, D), lambda i, f: (i, 0)),
        out_shape=jax.ShapeDtypeStruct((T, D), F32),
        scratch_shapes=[pltpu.VMEM((tm, D), BF16)],
        compiler_params=_params("parallel", "arbitrary"),
        name="mlp",
    )(xf, g, wup, wdn)


def _swap_halves_idx(n):
    return (np.arange(n) + n // 2) % n


def _prep_even_weights(even_w_in, even_w_uq, even_w_ukv, even_q_norm_g, even_k_norm_g, lora, pool_width):
    H = MLA_HEADS
    off_kr = 2 * lora
    off_pool = off_kr + ROPE_DIM
    kr = off_kr + np.arange(ROPE_DIM)
    krs = off_kr + _swap_halves_idx(ROPE_DIM)
    in_perm = np.concatenate([np.arange(off_kr), off_pool + np.arange(pool_width), kr, kr, krs, krs])
    win = even_w_in[:, :, in_perm].astype(BF16)

    head0 = np.arange(H)[:, None] * QK_DIM
    q_nope = (head0 + np.arange(NOPE_DIM)[None, :]).reshape(-1)
    q_rope = (head0 + NOPE_DIM + np.arange(ROPE_DIM)[None, :]).reshape(-1)
    q_ropes = (head0 + NOPE_DIM + _swap_halves_idx(ROPE_DIM)[None, :]).reshape(-1)
    wuq = even_w_uq[:, :, np.concatenate([q_nope, q_rope, q_ropes])].astype(BF16)

    kv0 = np.arange(H)[:, None] * (NOPE_DIM + V_DIM)
    k_nope = (kv0 + np.arange(NOPE_DIM)[None, :]).reshape(-1)
    v_cols = (kv0 + NOPE_DIM + np.arange(V_DIM)[None, :]).reshape(-1)
    wuk = even_w_ukv[:, :, k_nope].astype(BF16)
    wuvt = jnp.swapaxes(even_w_ukv[:, :, v_cols], 1, 2).astype(BF16)

    def gain_rows(gvec):
        rope = gvec[:, NOPE_DIM:]
        ropes = rope[:, _swap_halves_idx(ROPE_DIM)]
        return [gvec[:, :NOPE_DIM], jnp.concatenate([rope, rope], -1), jnp.concatenate([ropes, ropes], -1)]

    rows = gain_rows(even_q_norm_g) + gain_rows(even_k_norm_g)
    rows += [jnp.zeros_like(rows[0])] * 2
    gains = jnp.stack(rows, axis=1).astype(F32)
    return win, wuq, wuk, wuvt, gains


def _pick_tile(n, want):
    t = min(n, want)
    while n % t:
        t //= 2
    return t


def kernel(x, positions, mix_norm_g, mlp_norm_g, w_mlp_up, w_mlp_down, even_w_in, even_q_a_norm_g,
           even_kv_a_norm_g, even_w_uq, even_w_ukv, even_q_norm_g, even_k_norm_g, even_pool_w,
           even_pool_scale, even_w_out, odd_w_in, odd_conv_w, odd_w_out):
    B, S, D = x.shape
    T = B * S
    depth = mix_norm_g.shape[0]
    lora = even_q_a_norm_g.shape[1]
    pool_width = even_pool_scale.shape[1]

    tm_mlp = _pick_tile(S, 512)
    tf_mlp = _pick_tile(w_mlp_up.shape[2], 1024)
    tm_even = _pick_tile(S, 256)
    tq = _pick_tile(S, 512)
    tc_conv = _pick_tile(odd_w_out.shape[1], 512)

    wup = w_mlp_up.astype(BF16)
    wdn = w_mlp_down.astype(BF16)
    win_e, wuq, wuk, wuvt, gains = _prep_even_weights(even_w_in, even_w_uq, even_w_ukv, even_q_norm_g,
                                                      even_k_norm_g, lora, pool_width)
    poolw = even_pool_w.astype(BF16)
    wout_e = even_w_out.astype(BF16)
    win_o = odd_w_in.astype(BF16)
    wout_o = odd_w_out.astype(BF16)

    tab = _rope_table_call(positions.reshape(T, 1), _pick_tile(T, 1024))

    xf = x.reshape(T, D)
    for layer in range(depth):
        g_mix = mix_norm_g[layer][None, :]
        if layer % 2 == 0:
            e = layer // 2
            q, k, vt, b = _even_in_call(
                xf, g_mix, win_e[e], even_q_a_norm_g[e][None, :], even_kv_a_norm_g[e][None, :],
                wuq[e], wuk[e], wuvt[e], gains[e], poolw[e], even_pool_scale[e][None, :], tab,
                B=B, S=S, tm=tm_even, tk=tq)
            a = _attn_call(q, k, vt, tq=tq).reshape(T, MLA_HEADS * V_DIM)
            xf = _outproj_call(xf, a, b, wout_e, e, tm=tm_mlp)
        else:
            o = layer // 2
            xf = _conv_call(xf, g_mix, win_o, odd_conv_w, wout_o, o, S=S, tm=tm_mlp, tc=tc_conv)
        xf = _mlp_call(xf, mlp_norm_g[layer][None, :], wup, wdn, layer, tm=tm_mlp, tf=tf_mlp)
    return xf.reshape(B, S, D)
```

```python
import functools

import jax
import jax.numpy as jnp
import numpy as np
from jax import lax
from jax.experimental import pallas as pl
from jax.experimental.pallas import tpu as pltpu

F32 = jnp.float32
BF16 = jnp.bfloat16

RMS_EPS = 1e-6
ROPE_THETA = 10000.0
MLA_HEADS = 8
NOPE_DIM = 128
ROPE_DIM = 64
QK_DIM = NOPE_DIM + ROPE_DIM
V_DIM = 128
VT_ROWS = V_DIM + 16
LOG2_E = 1.4426950408889634
POOL_WINDOWS = (2, 4, 8, 16)
POOL_HISTORY = 16
CONV_WIDTH = 3
CONV_HISTORY = 8
LANES = 128

VMEM_LIMIT_BYTES = 56 * 1024 * 1024


def _rms(xf, g):
    ms = jnp.mean(xf * xf, axis=-1, keepdims=True)
    return xf * lax.rsqrt(ms + RMS_EPS) * g


def _dot(a, b):
    return jnp.dot(a, b, preferred_element_type=F32)


def _params(*sem):
    return pltpu.CompilerParams(dimension_semantics=sem, vmem_limit_bytes=VMEM_LIMIT_BYTES)


def _resident(block_shape, index_map):
    return pl.BlockSpec(block_shape, index_map, pipeline_mode=pl.Buffered(1))


def _rope_table_kernel(pos_ref, freq_ref, tab_ref):
    ang = pos_ref[...].astype(F32) * freq_ref[...]
    lane = lax.broadcasted_iota(jnp.int32, ang.shape, 1)
    sign = jnp.where((lane % ROPE_DIM) < ROPE_DIM // 2, -1.0, 1.0)
    tab_ref[:, :LANES] = jnp.cos(ang)
    tab_ref[:, LANES:] = jnp.sin(ang) * sign


def _rope_table_call(pos_col, tm):
    T = pos_col.shape[0]
    inv_freq = 1.0 / (ROPE_THETA ** (jnp.arange(0, ROPE_DIM, 2, dtype=F32) / ROPE_DIM))
    freq = jnp.tile(inv_freq, LANES // (ROPE_DIM // 2))[None, :]
    return pl.pallas_call(
        _rope_table_kernel,
        grid=(T // tm,),
        in_specs=[pl.BlockSpec((tm, 1), lambda i: (i, 0)),
                  pl.BlockSpec((1, LANES), lambda i: (0, 0))],
        out_specs=pl.BlockSpec((tm, 2 * LANES), lambda i: (i, 0)),
        out_shape=jax.ShapeDtypeStruct((T, 2 * LANES), F32),
        compiler_params=_params("parallel"),
        name="rope_table",
    )(pos_col, freq)


def _even_in_kernel(x_ref, g_ref, win_ref, qag_ref, kvag_ref, wuq_ref, wuk_ref, wuvt_ref, gains_ref,
                    poolw_ref, pscale_ref, tab_ref,
                    q_ref, k_ref, vt_ref, b_ref, carry_ref, *, tm, lora, pool_width, scale):
    s = pl.program_id(1)
    hn = _rms(x_ref[...], g_ref[...]).astype(BF16)
    proj = _dot(hn, win_ref[...])
    off_pool = 2 * lora
    off_kr = off_pool + pool_width
    cqn = _rms(proj[:, :lora], qag_ref[...]).astype(BF16)
    ckvn = _rms(proj[:, lora:2 * lora], kvag_ref[...]).astype(BF16)
    u = proj[:, off_pool:off_kr]
    kr2 = proj[:, off_kr:off_kr + LANES]
    krs2 = proj[:, off_kr + LANES:off_kr + 2 * LANES]

    qall = _dot(cqn, wuq_ref[...])
    knall = _dot(ckvn, wuk_ref[...])
    vt_all = lax.dot_general(wuvt_ref[...], ckvn, (((1,), (1,)), ((), ())), preferred_element_type=F32)

    cos2 = tab_ref[:, :LANES]
    sin2 = tab_ref[:, LANES:]
    gains = gains_ref[...]
    qg_nope, qg_rope2, qg_ropes2 = gains[0:1], gains[1:2], gains[2:3]
    kg_nope, kg_rope2, kg_ropes2 = gains[3:4], gains[4:5], gains[5:6]
    lane = lax.broadcasted_iota(jnp.int32, (tm, LANES), 1)
    low_half = lane < ROPE_DIM
    nheads = MLA_HEADS
    nope_w = nheads * NOPE_DIM

    kr_sq = jnp.where(low_half, kr2 * kr2, 0.0)
    k_roped2 = kr2 * kg_rope2 * cos2 + krs2 * kg_ropes2 * sin2
    for h in range(nheads):
        kn = knall[:, h * NOPE_DIM:(h + 1) * NOPE_DIM]
        ss = jnp.sum(kn * kn + kr_sq, axis=-1, keepdims=True)
        r = lax.rsqrt(ss * (1.0 / QK_DIM) + RMS_EPS)
        k_ref[0, h, :, 0:NOPE_DIM] = (kn * r * kg_nope).astype(BF16)
        k_ref[0, h, :, NOPE_DIM:QK_DIM] = (k_roped2 * r)[:, :ROPE_DIM].astype(BF16)
        vt_ref[0, h, 0, 0:V_DIM, :] = vt_all[h * V_DIM:(h + 1) * V_DIM].astype(BF16)
        vt_ref[0, h, 0, V_DIM:VT_ROWS, :] = jnp.ones((VT_ROWS - V_DIM, tm), BF16)

    rope_w = nheads * ROPE_DIM
    for p in range(nheads // 2):
        qr2 = qall[:, nope_w + p * LANES:nope_w + (p + 1) * LANES]
        qrs2 = qall[:, nope_w + rope_w + p * LANES:nope_w + rope_w + (p + 1) * LANES]
        sq = qr2 * qr2
        roped = qr2 * qg_rope2 * cos2 + qrs2 * qg_ropes2 * sin2
        for e in range(2):
            h = 2 * p + e
            qn = qall[:, h * NOPE_DIM:(h + 1) * NOPE_DIM]
            mine = low_half if e == 0 else jnp.logical_not(low_half)
            ss = jnp.sum(qn * qn + jnp.where(mine, sq, 0.0), axis=-1, keepdims=True)
            r = lax.rsqrt(ss * (1.0 / QK_DIM) + RMS_EPS) * scale
            q_ref[0, h, :, 0:NOPE_DIM] = (qn * r * qg_nope).astype(BF16)
            rp = roped * r
            if e == 1:
                rp = pltpu.roll(rp, ROPE_DIM, axis=1)
            q_ref[0, h, :, NOPE_DIM:QK_DIM] = rp[:, :ROPE_DIM].astype(BF16)

    prev = jnp.where(s == 0, 0.0, carry_ref[...])
    carry_ref[...] = u[tm - POOL_HISTORY:tm]
    ext = jnp.concatenate([prev, u], axis=0)
    t_in_seq = s * tm + lax.broadcasted_iota(jnp.int32, (tm, 1), 0)
    gd = pool_width // len(POOL_WINDOWS)
    pscale = pscale_ref[...]
    for gi, w in enumerate(POOL_WINDOWS):
        e = ext[:, gi * gd:(gi + 1) * gd]
        sh = 1
        while sh < w:
            e = e + pltpu.roll(e, sh, axis=0)
            sh *= 2
        inv_cnt = 1.0 / jnp.minimum(t_in_seq + 1, w).astype(F32)
        pooled = (e[POOL_HISTORY:] * inv_cnt - u[:, gi * gd:(gi + 1) * gd]).astype(BF16)
        y = _dot(pooled, poolw_ref[gi]) * pscale[:, gi * gd:(gi + 1) * gd]
        b_ref[:, gi * gd:(gi + 1) * gd] = y.astype(BF16)


def _even_in_call(xf, g, win, qag, kvag, wuq, wuk, wuvt, gains, poolw, pscale, tab, *, B, S, tm, tk):
    T, D = xf.shape
    lora = qag.shape[1]
    pool_width = pscale.shape[1]
    n_s = S // tm
    H = MLA_HEADS
    per_kv_block = tk // tm
    tok = lambda b, s: (b * n_s + s, 0)
    const2 = lambda b, s: (0, 0)
    kern = functools.partial(_even_in_kernel, tm=tm, lora=lora, pool_width=pool_width,
                             scale=QK_DIM ** -0.5 * LOG2_E)
    return pl.pallas_call(
        kern,
        grid=(B, n_s),
        in_specs=[
            pl.BlockSpec((tm, D), tok),
            _resident((1, D), const2),
            _resident(win.shape, const2),
            _resident((1, lora), const2),
            _resident((1, lora), const2),
            _resident(wuq.shape, const2),
            _resident(wuk.shape, const2),
            _resident(wuvt.shape, const2),
            _resident(gains.shape, const2),
            _resident(poolw.shape, lambda b, s: (0, 0, 0)),
            _resident((1, pool_width), const2),
            pl.BlockSpec((tm, 2 * LANES), tok),
        ],
        out_specs=[
            pl.BlockSpec((1, H, tm, QK_DIM), lambda b, s: (b, 0, s, 0)),
            pl.BlockSpec((1, H, tm, QK_DIM), lambda b, s: (b, 0, s, 0)),
            pl.BlockSpec((1, H, 1, VT_ROWS, tm),
                         lambda b, s: (b, 0, s // per_kv_block, 0, s % per_kv_block)),
            pl.BlockSpec((tm, pool_width), tok),
        ],
        out_shape=[
            jax.ShapeDtypeStruct((B, H, S, QK_DIM), BF16),
            jax.ShapeDtypeStruct((B, H, S, QK_DIM), BF16),
            jax.ShapeDtypeStruct((B, H, S // tk, VT_ROWS, tk), BF16),
            jax.ShapeDtypeStruct((T, pool_width), BF16),
        ],
        scratch_shapes=[pltpu.VMEM((POOL_HISTORY, pool_width), F32)],
        compiler_params=_params("arbitrary", "arbitrary"),
        name="even_in",
    )(xf, g, win, qag, kvag, wuq, wuk, wuvt, gains, poolw, pscale, tab)


def _attn_kernel(q_ref, k_ref, vt_ref, o_ref, m_ref, acc_ref, s0_ref, s1_ref, p0_ref, p1_ref,
                 a0_ref, a1_ref, *, tq):
    i = pl.program_id(2)
    q = q_ref[0, 0]
    s_refs = (s0_ref, s1_ref)
    p_refs = (p0_ref, p1_ref)
    a_refs = (a0_ref, a1_ref)

    def scores(j):
        start = pl.multiple_of(j * tq, tq)
        kb = k_ref[0, 0, pl.ds(start, tq), :]
        return lax.dot_general(kb, q, (((1,), (1,)), ((), ())), preferred_element_type=F32)

    def softmax(slot, masked):
        s = s_refs[slot][...]
        if masked:
            key = lax.broadcasted_iota(jnp.int32, s.shape, 0)
            qry = lax.broadcasted_iota(jnp.int32, s.shape, 1)
            s = jnp.where(key <= qry, s, -1e30)
        m_prev = m_ref[...]
        m_new = jnp.maximum(m_prev, jnp.max(s, axis=0, keepdims=True))
        m_ref[...] = m_new
        a_refs[slot][...] = jnp.exp2(m_prev - m_new)
        p_refs[slot][...] = jnp.exp2(s - m_new).astype(BF16)

    def accumulate(j, slot):
        acc_ref[...] = a_refs[slot][...] * acc_ref[...] + _dot(vt_ref[0, 0, j], p_refs[slot][...])

    def step(j, slot):
        s_refs[1 - slot][...] = scores(j + 1)
        accumulate(jnp.maximum(j - 1, 0), 1 - slot)
        softmax(slot, False)

    def last(slot):
        accumulate(jnp.maximum(i - 1, 0), 1 - slot)
        softmax(slot, True)
        accumulate(i, slot)
        acc = acc_ref[...]
        out_t = acc[0:V_DIM] / acc[V_DIM:V_DIM + 1]
        o_ref[0] = out_t.T.astype(BF16)

    m_ref[...] = jnp.full(m_ref.shape, -jnp.inf, F32)
    acc_ref[...] = jnp.zeros(acc_ref.shape, F32)
    s0_ref[...] = scores(0)
    p1_ref[...] = jnp.zeros(p1_ref.shape, BF16)
    a1_ref[...] = jnp.ones(a1_ref.shape, F32)

    def pair(jj, carry):
        step(2 * jj, 0)
        step(2 * jj + 1, 1)
        return carry

    lax.fori_loop(0, i // 2, pair, 0)

    @pl.when(i % 2 == 1)
    def _():
        step(i - 1, 0)
        last(1)

    @pl.when(i % 2 == 0)
    def _():
        last(0)


def _attn_call(q, k, vt, *, tq):
    B, H, S, _ = q.shape
    n_kv = vt.shape[2]
    assert vt.shape[4] == tq
    return pl.pallas_call(
        functools.partial(_attn_kernel, tq=tq),
        grid=(B, H, S // tq),
        in_specs=[
            pl.BlockSpec((1, 1, tq, QK_DIM), lambda b, h, i: (b, h, i, 0)),
            pl.BlockSpec((1, 1, S, QK_DIM), lambda b, h, i: (b, h, 0, 0)),
            pl.BlockSpec((1, 1, n_kv, VT_ROWS, tq), lambda b, h, i: (b, h, 0, 0, 0)),
        ],
        out_specs=pl.BlockSpec((1, tq, V_DIM), lambda b, h, i: (b, i, h)),
        out_shape=jax.ShapeDtypeStruct((B, S, H * V_DIM), BF16),
        scratch_shapes=[pltpu.VMEM((1, tq), F32), pltpu.VMEM((VT_ROWS, tq), F32),
                        pltpu.VMEM((tq, tq), F32), pltpu.VMEM((tq, tq), F32),
                        pltpu.VMEM((tq, tq), BF16), pltpu.VMEM((tq, tq), BF16),
                        pltpu.VMEM((1, tq), F32), pltpu.VMEM((1, tq), F32)],
        compiler_params=_params("parallel", "parallel", "arbitrary"),
        name="attn",
    )(q, k, vt)


def _outproj_kernel(x_ref, a_ref, b_ref, wa_ref, wb_ref, o_ref):
    o_ref[...] = x_ref[...] + _dot(a_ref[...], wa_ref[...]) + _dot(b_ref[...], wb_ref[...])


def _outproj_call(xf, a, b, wout, e, *, tm):
    T, D = xf.shape
    wa_rows = a.shape[1]
    wb_rows = b.shape[1]
    assert wa_rows == wb_rows
    return pl.pallas_call(
        _outproj_kernel,
        grid=(T // tm,),
        in_specs=[
            pl.BlockSpec((tm, D), lambda i: (i, 0)),
            pl.BlockSpec((tm, wa_rows), lambda i: (i, 0)),
            pl.BlockSpec((tm, wb_rows), lambda i: (i, 0)),
            _resident((None, wa_rows, D), lambda i: (e, 0, 0)),
            _resident((None, wb_rows, D), lambda i: (e, 1, 0)),
        ],
        out_specs=pl.BlockSpec((tm, D), lambda i: (i, 0)),
        out_shape=jax.ShapeDtypeStruct((T, D), F32),
        compiler_params=_params("parallel"),
        name="outproj",
    )(xf, a, b, wout, wout)


def _conv_kernel(x_ref, g_ref, wb_ref, wc_ref, wu_ref, cw_ref, wo_ref, o_ref, hn_ref, carry_ref,
                 *, tm, tiles_per_seq):
    i = pl.program_id(0)
    c = pl.program_id(1)

    @pl.when(c == 0)
    def _():
        x = x_ref[...]
        hn_ref[...] = _rms(x, g_ref[...]).astype(BF16)
        o_ref[...] = x

    hn = hn_ref[...]
    gate_b = _dot(hn, wb_ref[...])
    v = _dot(hn, wc_ref[...]) * _dot(hn, wu_ref[...])
    prev = jnp.where(i % tiles_per_seq == 0, 0.0, carry_ref[c])
    carry_ref[c] = v[tm - CONV_HISTORY:tm]
    ext = jnp.concatenate([prev, v], axis=0)
    cw = cw_ref[...]
    conv = cw[CONV_WIDTH - 1:CONV_WIDTH] * v
    for back in range(1, CONV_WIDTH):
        tap = CONV_WIDTH - 1 - back
        conv = conv + cw[tap:tap + 1] * pltpu.roll(ext, back, axis=0)[CONV_HISTORY:]
    o_ref[...] += _dot((gate_b * conv).astype(BF16), wo_ref[...])


def _conv_call(xf, g, win, convw, wout, o, *, S, tm, tc):
    T, D = xf.shape
    C = wout.shape[1]
    n_c = C // tc
    kern = functools.partial(_conv_kernel, tm=tm, tiles_per_seq=S // tm)
    return pl.pallas_call(
        kern,
        grid=(T // tm, n_c),
        in_specs=[
            pl.BlockSpec((tm, D), lambda i, c: (i, 0)),
            pl.BlockSpec((1, D), lambda i, c: (0, 0)),
            pl.BlockSpec((None, D, tc), lambda i, c: (o, 0, c)),
            pl.BlockSpec((None, D, tc), lambda i, c: (o, 0, n_c + c)),
            pl.BlockSpec((None, D, tc), lambda i, c: (o, 0, 2 * n_c + c)),
            pl.BlockSpec((None, CONV_WIDTH, tc), lambda i, c: (o, 0, c)),
            pl.BlockSpec((None, tc, D), lambda i, c: (o, c, 0)),
        ],
        out_specs=pl.BlockSpec((tm, D), lambda i, c: (i, 0)),
        out_shape=jax.ShapeDtypeStruct((T, D), F32),
        scratch_shapes=[pltpu.VMEM((tm, D), BF16),
                        pltpu.VMEM((n_c, CONV_HISTORY, tc), F32)],
        compiler_params=_params("arbitrary", "arbitrary"),
        name="conv_mixer",
    )(xf, g, win, win, win, convw, wout)


def _mlp_kernel(x_ref, g_ref, wup_ref, wdn_ref, o_ref, hn_ref):
    @pl.when(pl.program_id(1) == 0)
    def _():
        x = x_ref[...]
        hn_ref[...] = _rms(x, g_ref[...]).astype(BF16)
        o_ref[...] = x

    up = _dot(hn_ref[...], wup_ref[...])
    act = jnp.square(jnp.maximum(up, 0.0)).astype(BF16)
    o_ref[...] += _dot(act, wdn_ref[...])


def _mlp_call(xf, g, wup, wdn, layer, *, tm, tf):
    T, D = xf.shape
    F = wup.shape[2]
    return pl.pallas_call(
        _mlp_kernel,
        grid=(T // tm, F // tf),
        in_specs=[
            pl.BlockSpec((tm, D), lambda i, f: (i, 0)),
            pl.BlockSpec((1, D), lambda i, f: (0, 0)),
            pl.BlockSpec((None, D, tf), lambda i, f: (layer, 0, f)),
            pl.BlockSpec((None, tf, D), lambda i, f: (layer, f, 0)),
        ],
        out_specs=pl.BlockSpec((tm, D), lambda i, f: (i, 0)),
        out_shape=jax.ShapeDtypeStruct((T, D), F32),
        scratch_shapes=[pltpu.VMEM((tm, D), BF16)],
        compiler_params=_params("parallel", "arbitrary"),
        name="mlp",
    )(xf, g, wup, wdn)


def _swap_halves_idx(n):
    return (np.arange(n) + n // 2) % n


def _prep_even_weights(even_w_in, even_w_uq, even_w_ukv, even_q_norm_g, even_k_norm_g, lora, pool_width):
    H = MLA_HEADS
    off_kr = 2 * lora
    off_pool = off_kr + ROPE_DIM
    kr = off_kr + np.arange(ROPE_DIM)
    krs = off_kr + _swap_halves_idx(ROPE_DIM)
    in_perm = np.concatenate([np.arange(off_kr), off_pool + np.arange(pool_width), kr, kr, krs, krs])
    win = even_w_in[:, :, in_perm].astype(BF16)

    head0 = np.arange(H)[:, None] * QK_DIM
    q_nope = (head0 + np.arange(NOPE_DIM)[None, :]).reshape(-1)
    q_rope = (head0 + NOPE_DIM + np.arange(ROPE_DIM)[None, :]).reshape(-1)
    q_ropes = (head0 + NOPE_DIM + _swap_halves_idx(ROPE_DIM)[None, :]).reshape(-1)
    wuq = even_w_uq[:, :, np.concatenate([q_nope, q_rope, q_ropes])].astype(BF16)

    kv0 = np.arange(H)[:, None] * (NOPE_DIM + V_DIM)
    k_nope = (kv0 + np.arange(NOPE_DIM)[None, :]).reshape(-1)
    v_cols = (kv0 + NOPE_DIM + np.arange(V_DIM)[None, :]).reshape(-1)
    wuk = even_w_ukv[:, :, k_nope].astype(BF16)
    wuvt = jnp.swapaxes(even_w_ukv[:, :, v_cols], 1, 2).astype(BF16)

    def gain_rows(gvec):
        rope = gvec[:, NOPE_DIM:]
        ropes = rope[:, _swap_halves_idx(ROPE_DIM)]
        return [gvec[:, :NOPE_DIM], jnp.concatenate([rope, rope], -1), jnp.concatenate([ropes, ropes], -1)]

    rows = gain_rows(even_q_norm_g) + gain_rows(even_k_norm_g)
    rows += [jnp.zeros_like(rows[0])] * 2
    gains = jnp.stack(rows, axis=1).astype(F32)
    return win, wuq, wuk, wuvt, gains


def _pick_tile(n, want):
    t = min(n, want)
    while n % t:
        t //= 2
    return t


def kernel(x, positions, mix_norm_g, mlp_norm_g, w_mlp_up, w_mlp_down, even_w_in, even_q_a_norm_g,
           even_kv_a_norm_g, even_w_uq, even_w_ukv, even_q_norm_g, even_k_norm_g, even_pool_w,
           even_pool_scale, even_w_out, odd_w_in, odd_conv_w, odd_w_out):
    B, S, D = x.shape
    T = B * S
    depth = mix_norm_g.shape[0]
    lora = even_q_a_norm_g.shape[1]
    pool_width = even_pool_scale.shape[1]

    tm_mlp = _pick_tile(S, 512)
    tf_mlp = _pick_tile(w_mlp_up.shape[2], 1024)
    tm_even = _pick_tile(S, 256)
    tq = _pick_tile(S, 512)
    tc_conv = _pick_tile(odd_w_out.shape[1], 512)

    wup = w_mlp_up.astype(BF16)
    wdn = w_mlp_down.astype(BF16)
    win_e, wuq, wuk, wuvt, gains = _prep_even_weights(even_w_in, even_w_uq, even_w_ukv, even_q_norm_g,
                                                      even_k_norm_g, lora, pool_width)
    poolw = even_pool_w.astype(BF16)
    wout_e = even_w_out.astype(BF16)
    win_o = odd_w_in.astype(BF16)
    wout_o = odd_w_out.astype(BF16)

    tab = _rope_table_call(positions.reshape(T, 1), _pick_tile(T, 1024))

    xf = x.reshape(T, D)
    for layer in range(depth):
        g_mix = mix_norm_g[layer][None, :]
        if layer % 2 == 0:
            e = layer // 2
            q, k, vt, b = _even_in_call(
                xf, g_mix, win_e[e], even_q_a_norm_g[e][None, :], even_kv_a_norm_g[e][None, :],
                wuq[e], wuk[e], wuvt[e], gains[e], poolw[e], even_pool_scale[e][None, :], tab,
                B=B, S=S, tm=tm_even, tk=tq)
            a = _attn_call(q, k, vt, tq=tq).reshape(T, MLA_HEADS * V_DIM)
            xf = _outproj_call(xf, a, b, wout_e, e, tm=tm_mlp)
        else:
            o = layer // 2
            xf = _conv_call(xf, g_mix, win_o, odd_conv_w, wout_o, o, S=S, tm=tm_mlp, tc=tc_conv)
        xf = _mlp_call(xf, mlp_norm_g[layer][None, :], wup, wdn, layer, tm=tm_mlp, tf=tf_mlp)
    return xf.reshape(B, S, D)
```

```python
import functools

import jax
import jax.numpy as jnp
import numpy as np
from jax import lax
from jax.experimental import pallas as pl
from jax.experimental.pallas import tpu as pltpu

F32 = jnp.float32
BF16 = jnp.bfloat16

RMS_EPS = 1e-6
ROPE_THETA = 10000.0
MLA_HEADS = 8
NOPE_DIM = 128
ROPE_DIM = 64
QK_DIM = NOPE_DIM + ROPE_DIM
V_DIM = 128
VT_ROWS = V_DIM + 16
LOG2_E = 1.4426950408889634
POOL_WINDOWS = (2, 4, 8, 16)
POOL_HISTORY = 16
CONV_WIDTH = 3
CONV_HISTORY = 8
LANES = 128

VMEM_LIMIT_BYTES = 56 * 1024 * 1024


def _rms(xf, g):
    ms = jnp.mean(xf * xf, axis=-1, keepdims=True)
    return xf * lax.rsqrt(ms + RMS_EPS) * g


def _dot(a, b):
    return jnp.dot(a, b, preferred_element_type=F32)


def _params(*sem):
    return pltpu.CompilerParams(dimension_semantics=sem, vmem_limit_bytes=VMEM_LIMIT_BYTES)


def _resident(block_shape, index_map):
    return pl.BlockSpec(block_shape, index_map, pipeline_mode=pl.Buffered(1))


def _rope_table_kernel(pos_ref, freq_ref, tab_ref):
    ang = pos_ref[...].astype(F32) * freq_ref[...]
    lane = lax.broadcasted_iota(jnp.int32, ang.shape, 1)
    sign = jnp.where((lane % ROPE_DIM) < ROPE_DIM // 2, -1.0, 1.0)
    tab_ref[:, :LANES] = jnp.cos(ang)
    tab_ref[:, LANES:] = jnp.sin(ang) * sign


def _rope_table_call(pos_col, tm):
    T = pos_col.shape[0]
    inv_freq = 1.0 / (ROPE_THETA ** (jnp.arange(0, ROPE_DIM, 2, dtype=F32) / ROPE_DIM))
    freq = jnp.tile(inv_freq, LANES // (ROPE_DIM // 2))[None, :]
    return pl.pallas_call(
        _rope_table_kernel,
        grid=(T // tm,),
        in_specs=[pl.BlockSpec((tm, 1), lambda i: (i, 0)),
                  pl.BlockSpec((1, LANES), lambda i: (0, 0))],
        out_specs=pl.BlockSpec((tm, 2 * LANES), lambda i: (i, 0)),
        out_shape=jax.ShapeDtypeStruct((T, 2 * LANES), F32),
        compiler_params=_params("parallel"),
        name="rope_table",
    )(pos_col, freq)


def _even_in_kernel(x_ref, g_ref, win_ref, qag_ref, kvag_ref, wuq_ref, wuk_ref, wuvt_ref, gains_ref,
                    poolw_ref, pscale_ref, tab_ref,
                    q_ref, k_ref, vt_ref, b_ref, carry_ref, *, tm, lora, pool_width, scale):
    s = pl.program_id(1)
    hn = _rms(x_ref[...], g_ref[...]).astype(BF16)
    proj = _dot(hn, win_ref[...])
    off_pool = 2 * lora
    off_kr = off_pool + pool_width
    cqn = _rms(proj[:, :lora], qag_ref[...]).astype(BF16)
    ckvn = _rms(proj[:, lora:2 * lora], kvag_ref[...]).astype(BF16)
    u = proj[:, off_pool:off_kr]
    kr2 = proj[:, off_kr:off_kr + LANES]
    krs2 = proj[:, off_kr + LANES:off_kr + 2 * LANES]

    qall = _dot(cqn, wuq_ref[...])
    knall = _dot(ckvn, wuk_ref[...])
    vt_all = lax.dot_general(wuvt_ref[...], ckvn, (((1,), (1,)), ((), ())), preferred_element_type=F32)

    cos2 = tab_ref[:, :LANES]
    sin2 = tab_ref[:, LANES:]
    gains = gains_ref[...]
    qg_nope, qg_rope2, qg_ropes2 = gains[0:1], gains[1:2], gains[2:3]
    kg_nope, kg_rope2, kg_ropes2 = gains[3:4], gains[4:5], gains[5:6]
    lane = lax.broadcasted_iota(jnp.int32, (tm, LANES), 1)
    low_half = lane < ROPE_DIM
    nheads = MLA_HEADS
    nope_w = nheads * NOPE_DIM

    kr_sq = jnp.where(low_half, kr2 * kr2, 0.0)
    k_roped2 = kr2 * kg_rope2 * cos2 + krs2 * kg_ropes2 * sin2
    for h in range(nheads):
        kn = knall[:, h * NOPE_DIM:(h + 1) * NOPE_DIM]
        ss = jnp.sum(kn * kn + kr_sq, axis=-1, keepdims=True)
        r = lax.rsqrt(ss * (1.0 / QK_DIM) + RMS_EPS)
        k_ref[0, h, :, 0:NOPE_DIM] = (kn * r * kg_nope).astype(BF16)
        k_ref[0, h, :, NOPE_DIM:QK_DIM] = (k_roped2 * r)[:, :ROPE_DIM].astype(BF16)
        vt_ref[0, h, 0, 0:V_DIM, :] = vt_all[h * V_DIM:(h + 1) * V_DIM].astype(BF16)
        vt_ref[0, h, 0, V_DIM:VT_ROWS, :] = jnp.ones((VT_ROWS - V_DIM, tm), BF16)

    rope_w = nheads * ROPE_DIM
    for p in range(nheads // 2):
        qr2 = qall[:, nope_w + p * LANES:nope_w + (p + 1) * LANES]
        qrs2 = qall[:, nope_w + rope_w + p * LANES:nope_w + rope_w + (p + 1) * LANES]
        sq = qr2 * qr2
        roped = qr2 * qg_rope2 * cos2 + qrs2 * qg_ropes2 * sin2
        for e in range(2):
            h = 2 * p + e
            qn = qall[:, h * NOPE_DIM:(h + 1) * NOPE_DIM]
            mine = low_half if e == 0 else jnp.logical_not(low_half)
            ss = jnp.sum(qn * qn + jnp.where(mine, sq, 0.0), axis=-1, keepdims=True)
            r = lax.rsqrt(ss * (1.0 / QK_DIM) + RMS_EPS) * scale
            q_ref[0, h, :, 0:NOPE_DIM] = (qn * r * qg_nope).astype(BF16)
            rp = roped * r
            if e == 1:
                rp = pltpu.roll(rp, ROPE_DIM, axis=1)
            q_ref[0, h, :, NOPE_DIM:QK_DIM] = rp[:, :ROPE_DIM].astype(BF16)

    prev = jnp.where(s == 0, 0.0, carry_ref[...])
    carry_ref[...] = u[tm - POOL_HISTORY:tm]
    ext = jnp.concatenate([prev, u], axis=0)
    t_in_seq = s * tm + lax.broadcasted_iota(jnp.int32, (tm, 1), 0)
    gd = pool_width // len(POOL_WINDOWS)
    pscale = pscale_ref[...]
    for gi, w in enumerate(POOL_WINDOWS):
        e = ext[:, gi * gd:(gi + 1) * gd]
        sh = 1
        while sh < w:
            e = e + pltpu.roll(e, sh, axis=0)
            sh *= 2
        inv_cnt = 1.0 / jnp.minimum(t_in_seq + 1, w).astype(F32)
        pooled = (e[POOL_HISTORY:] * inv_cnt - u[:, gi * gd:(gi + 1) * gd]).astype(BF16)
        y = _dot(pooled, poolw_ref[gi]) * pscale[:, gi * gd:(gi + 1) * gd]
        b_ref[:, gi * gd:(gi + 1) * gd] = y.astype(BF16)


def _even_in_call(xf, g, win, qag, kvag, wuq, wuk, wuvt, gains, poolw, pscale, tab, *, B, S, tm, tk):
    T, D = xf.shape
    lora = qag.shape[1]
    pool_width = pscale.shape[1]
    n_s = S // tm
    H = MLA_HEADS
    per_kv_block = tk // tm
    tok = lambda b, s: (b * n_s + s, 0)
    const2 = lambda b, s: (0, 0)
    kern = functools.partial(_even_in_kernel, tm=tm, lora=lora, pool_width=pool_width,
                             scale=QK_DIM ** -0.5 * LOG2_E)
    return pl.pallas_call(
        kern,
        grid=(B, n_s),
        in_specs=[
            pl.BlockSpec((tm, D), tok),
            _resident((1, D), const2),
            _resident(win.shape, const2),
            _resident((1, lora), const2),
            _resident((1, lora), const2),
            _resident(wuq.shape, const2),
            _resident(wuk.shape, const2),
            _resident(wuvt.shape, const2),
            _resident(gains.shape, const2),
            _resident(poolw.shape, lambda b, s: (0, 0, 0)),
            _resident((1, pool_width), const2),
            pl.BlockSpec((tm, 2 * LANES), tok),
        ],
        out_specs=[
            pl.BlockSpec((1, H, tm, QK_DIM), lambda b, s: (b, 0, s, 0)),
            pl.BlockSpec((1, H, tm, QK_DIM), lambda b, s: (b, 0, s, 0)),
            pl.BlockSpec((1, H, 1, VT_ROWS, tm),
                         lambda b, s: (b, 0, s // per_kv_block, 0, s % per_kv_block)),
            pl.BlockSpec((tm, pool_width), tok),
        ],
        out_shape=[
            jax.ShapeDtypeStruct((B, H, S, QK_DIM), BF16),
            jax.ShapeDtypeStruct((B, H, S, QK_DIM), BF16),
            jax.ShapeDtypeStruct((B, H, S // tk, VT_ROWS, tk), BF16),
            jax.ShapeDtypeStruct((T, pool_width), BF16),
        ],
        scratch_shapes=[pltpu.VMEM((POOL_HISTORY, pool_width), F32)],
        compiler_params=_params("arbitrary", "arbitrary"),
        name="even_in",
    )(xf, g, win, qag, kvag, wuq, wuk, wuvt, gains, poolw, pscale, tab)


ATTN_HEADS_PER_STEP = 2
ATTN_SCRATCH_PER_HEAD = 8


def _attn_kernel(q_ref, k_ref, vt_ref, o_ref, *scratch, tq):
    i = pl.program_id(2)
    heads = range(ATTN_HEADS_PER_STEP)
    per_head = [scratch[h * ATTN_SCRATCH_PER_HEAD:(h + 1) * ATTN_SCRATCH_PER_HEAD] for h in heads]
    m_refs = [r[0] for r in per_head]
    acc_refs = [r[1] for r in per_head]
    s_refs = [r[2:4] for r in per_head]
    p_refs = [r[4:6] for r in per_head]
    a_refs = [r[6:8] for r in per_head]

    def scores(h, j):
        start = pl.multiple_of(j * tq, tq)
        kb = k_ref[0, h, pl.ds(start, tq), :]
        return lax.dot_general(kb, q_ref[0, h], (((1,), (1,)), ((), ())), preferred_element_type=F32)

    def softmax(h, slot, masked):
        s = s_refs[h][slot][...]
        if masked:
            key = lax.broadcasted_iota(jnp.int32, s.shape, 0)
            qry = lax.broadcasted_iota(jnp.int32, s.shape, 1)
            s = jnp.where(key <= qry, s, -1e30)
        m_prev = m_refs[h][...]
        m_new = jnp.maximum(m_prev, jnp.max(s, axis=0, keepdims=True))
        m_refs[h][...] = m_new
        a_refs[h][slot][...] = jnp.exp2(m_prev - m_new)
        p_refs[h][slot][...] = jnp.exp2(s - m_new).astype(BF16)

    def accumulate(h, j, slot):
        acc_refs[h][...] = (a_refs[h][slot][...] * acc_refs[h][...]
                            + _dot(vt_ref[0, h, j], p_refs[h][slot][...]))

    def step(j, slot):
        for h in heads:
            s_refs[h][1 - slot][...] = scores(h, j + 1)
            accumulate(h, jnp.maximum(j - 1, 0), 1 - slot)
            softmax(h, slot, False)

    def last(slot):
        for h in heads:
            accumulate(h, jnp.maximum(i - 1, 0), 1 - slot)
            softmax(h, slot, True)
            accumulate(h, i, slot)
            acc = acc_refs[h][...]
            out_t = acc[0:V_DIM] / acc[V_DIM:V_DIM + 1]
            o_ref[0, :, h * V_DIM:(h + 1) * V_DIM] = out_t.T.astype(BF16)

    for h in heads:
        m_refs[h][...] = jnp.full(m_refs[h].shape, -jnp.inf, F32)
        acc_refs[h][...] = jnp.zeros(acc_refs[h].shape, F32)
        s_refs[h][0][...] = scores(h, 0)
        p_refs[h][1][...] = jnp.zeros(p_refs[h][1].shape, BF16)
        a_refs[h][1][...] = jnp.ones(a_refs[h][1].shape, F32)

    def pair(jj, carry):
        step(2 * jj, 0)
        step(2 * jj + 1, 1)
        return carry

    lax.fori_loop(0, i // 2, pair, 0)

    @pl.when(i % 2 == 1)
    def _():
        step(i - 1, 0)
        last(1)

    @pl.when(i % 2 == 0)
    def _():
        last(0)


def _attn_call(q, k, vt, *, tq):
    B, H, S, _ = q.shape
    n_kv = vt.shape[2]
    assert vt.shape[4] == tq
    hps = ATTN_HEADS_PER_STEP
    head_scratch = [pltpu.VMEM((1, tq), F32), pltpu.VMEM((VT_ROWS, tq), F32),
                    pltpu.VMEM((tq, tq), F32), pltpu.VMEM((tq, tq), F32),
                    pltpu.VMEM((tq, tq), BF16), pltpu.VMEM((tq, tq), BF16),
                    pltpu.VMEM((1, tq), F32), pltpu.VMEM((1, tq), F32)]
    assert len(head_scratch) == ATTN_SCRATCH_PER_HEAD
    return pl.pallas_call(
        functools.partial(_attn_kernel, tq=tq),
        grid=(B, H // hps, S // tq),
        in_specs=[
            pl.BlockSpec((1, hps, tq, QK_DIM), lambda b, h, i: (b, h, i, 0)),
            pl.BlockSpec((1, hps, S, QK_DIM), lambda b, h, i: (b, h, 0, 0)),
            pl.BlockSpec((1, hps, n_kv, VT_ROWS, tq), lambda b, h, i: (b, h, 0, 0, 0)),
        ],
        out_specs=pl.BlockSpec((1, tq, hps * V_DIM), lambda b, h, i: (b, i, h)),
        out_shape=jax.ShapeDtypeStruct((B, S, H * V_DIM), BF16),
        scratch_shapes=head_scratch * hps,
        compiler_params=_params("parallel", "parallel", "arbitrary"),
        name="attn",
    )(q, k, vt)


def _outproj_kernel(x_ref, a_ref, b_ref, wa_ref, wb_ref, o_ref):
    o_ref[...] = x_ref[...] + _dot(a_ref[...], wa_ref[...]) + _dot(b_ref[...], wb_ref[...])


def _outproj_call(xf, a, b, wout, e, *, tm):
    T, D = xf.shape
    wa_rows = a.shape[1]
    wb_rows = b.shape[1]
    assert wa_rows == wb_rows
    return pl.pallas_call(
        _outproj_kernel,
        grid=(T // tm,),
        in_specs=[
            pl.BlockSpec((tm, D), lambda i: (i, 0)),
            pl.BlockSpec((tm, wa_rows), lambda i: (i, 0)),
            pl.BlockSpec((tm, wb_rows), lambda i: (i, 0)),
            _resident((None, wa_rows, D), lambda i: (e, 0, 0)),
            _resident((None, wb_rows, D), lambda i: (e, 1, 0)),
        ],
        out_specs=pl.BlockSpec((tm, D), lambda i: (i, 0)),
        out_shape=jax.ShapeDtypeStruct((T, D), F32),
        compiler_params=_params("parallel"),
        name="outproj",
    )(xf, a, b, wout, wout)


def _conv_kernel(x_ref, g_ref, wb_ref, wc_ref, wu_ref, cw_ref, wo_ref, o_ref, hn_ref, carry_ref,
                 *, tm, tiles_per_seq):
    i = pl.program_id(0)
    c = pl.program_id(1)

    @pl.when(c == 0)
    def _():
        x = x_ref[...]
        hn_ref[...] = _rms(x, g_ref[...]).astype(BF16)
        o_ref[...] = x

    hn = hn_ref[...]
    gate_b = _dot(hn, wb_ref[...])
    v = _dot(hn, wc_ref[...]) * _dot(hn, wu_ref[...])
    prev = jnp.where(i % tiles_per_seq == 0, 0.0, carry_ref[c])
    carry_ref[c] = v[tm - CONV_HISTORY:tm]
    ext = jnp.concatenate([prev, v], axis=0)
    cw = cw_ref[...]
    conv = cw[CONV_WIDTH - 1:CONV_WIDTH] * v
    for back in range(1, CONV_WIDTH):
        tap = CONV_WIDTH - 1 - back
        conv = conv + cw[tap:tap + 1] * pltpu.roll(ext, back, axis=0)[CONV_HISTORY:]
    o_ref[...] += _dot((gate_b * conv).astype(BF16), wo_ref[...])


def _conv_call(xf, g, win, convw, wout, o, *, S, tm, tc):
    T, D = xf.shape
    C = wout.shape[1]
    n_c = C // tc
    kern = functools.partial(_conv_kernel, tm=tm, tiles_per_seq=S // tm)
    return pl.pallas_call(
        kern,
        grid=(T // tm, n_c),
        in_specs=[
            pl.BlockSpec((tm, D), lambda i, c: (i, 0)),
            pl.BlockSpec((1, D), lambda i, c: (0, 0)),
            pl.BlockSpec((None, D, tc), lambda i, c: (o, 0, c)),
            pl.BlockSpec((None, D, tc), lambda i, c: (o, 0, n_c + c)),
            pl.BlockSpec((None, D, tc), lambda i, c: (o, 0, 2 * n_c + c)),
            pl.BlockSpec((None, CONV_WIDTH, tc), lambda i, c: (o, 0, c)),
            pl.BlockSpec((None, tc, D), lambda i, c: (o, c, 0)),
        ],
        out_specs=pl.BlockSpec((tm, D), lambda i, c: (i, 0)),
        out_shape=jax.ShapeDtypeStruct((T, D), F32),
        scratch_shapes=[pltpu.VMEM((tm, D), BF16),
                        pltpu.VMEM((n_c, CONV_HISTORY, tc), F32)],
        compiler_params=_params("arbitrary", "arbitrary"),
        name="conv_mixer",
    )(xf, g, win, win, win, convw, wout)


def _mlp_kernel(x_ref, g_ref, wup_ref, wdn_ref, o_ref, hn_ref):
    @pl.when(pl.program_id(1) == 0)
    def _():
        x = x_ref[...]
        hn_ref[...] = _rms(x, g_ref[...]).astype(BF16)
        o_ref[...] = x

    up = _dot(hn_ref[...], wup_ref[...])
    act = jnp.square(jnp.maximum(up, 0.0)).astype(BF16)
    o_ref[...] += _dot(act, wdn_ref[...])


def _mlp_call(xf, g, wup, wdn, layer, *, tm, tf):
    T, D = xf.shape
    F = wup.shape[2]
    return pl.pallas_call(
        _mlp_kernel,
        grid=(T // tm, F // tf),
        in_specs=[
            pl.BlockSpec((tm, D), lambda i, f: (i, 0)),
            pl.BlockSpec((1, D), lambda i, f: (0, 0)),
            pl.BlockSpec((None, D, tf), lambda i, f: (layer, 0, f)),
            pl.BlockSpec((None, tf, D), lambda i, f: (layer, f, 0)),
        ],
        out_specs=pl.BlockSpec((tm, D), lambda i, f: (i, 0)),
        out_shape=jax.ShapeDtypeStruct((T, D), F32),
        scratch_shapes=[pltpu.VMEM((tm, D), BF16)],
        compiler_params=_params("parallel", "arbitrary"),
        name="mlp",
    )(xf, g, wup, wdn)


def _swap_halves_idx(n):
    return (np.arange(n) + n // 2) % n


def _prep_even_weights(even_w_in, even_w_uq, even_w_ukv, even_q_norm_g, even_k_norm_g, lora, pool_width):
    H = MLA_HEADS
    off_kr = 2 * lora
    off_pool = off_kr + ROPE_DIM
    kr = off_kr + np.arange(ROPE_DIM)
    krs = off_kr + _swap_halves_idx(ROPE_DIM)
    in_perm = np.concatenate([np.arange(off_kr), off_pool + np.arange(pool_width), kr, kr, krs, krs])
    win = even_w_in[:, :, in_perm].astype(BF16)

    head0 = np.arange(H)[:, None] * QK_DIM
    q_nope = (head0 + np.arange(NOPE_DIM)[None, :]).reshape(-1)
    q_rope = (head0 + NOPE_DIM + np.arange(ROPE_DIM)[None, :]).reshape(-1)
    q_ropes = (head0 + NOPE_DIM + _swap_halves_idx(ROPE_DIM)[None, :]).reshape(-1)
    wuq = even_w_uq[:, :, np.concatenate([q_nope, q_rope, q_ropes])].astype(BF16)

    kv0 = np.arange(H)[:, None] * (NOPE_DIM + V_DIM)
    k_nope = (kv0 + np.arange(NOPE_DIM)[None, :]).reshape(-1)
    v_cols = (kv0 + NOPE_DIM + np.arange(V_DIM)[None, :]).reshape(-1)
    wuk = even_w_ukv[:, :, k_nope].astype(BF16)
    wuvt = jnp.swapaxes(even_w_ukv[:, :, v_cols], 1, 2).astype(BF16)

    def gain_rows(gvec):
        rope = gvec[:, NOPE_DIM:]
        ropes = rope[:, _swap_halves_idx(ROPE_DIM)]
        return [gvec[:, :NOPE_DIM], jnp.concatenate([rope, rope], -1), jnp.concatenate([ropes, ropes], -1)]

    rows = gain_rows(even_q_norm_g) + gain_rows(even_k_norm_g)
    rows += [jnp.zeros_like(rows[0])] * 2
    gains = jnp.stack(rows, axis=1).astype(F32)
    return win, wuq, wuk, wuvt, gains


def _pick_tile(n, want):
    t = min(n, want)
    while n % t:
        t //= 2
    return t


def kernel(x, positions, mix_norm_g, mlp_norm_g, w_mlp_up, w_mlp_down, even_w_in, even_q_a_norm_g,
           even_kv_a_norm_g, even_w_uq, even_w_ukv, even_q_norm_g, even_k_norm_g, even_pool_w,
           even_pool_scale, even_w_out, odd_w_in, odd_conv_w, odd_w_out):
    B, S, D = x.shape
    T = B * S
    depth = mix_norm_g.shape[0]
    lora = even_q_a_norm_g.shape[1]
    pool_width = even_pool_scale.shape[1]

    tm_mlp = _pick_tile(S, 512)
    tf_mlp = _pick_tile(w_mlp_up.shape[2], 1024)
    tm_even = _pick_tile(S, 256)
    tq = _pick_tile(S, 512)
    tc_conv = _pick_tile(odd_w_out.shape[1], 512)

    wup = w_mlp_up.astype(BF16)
    wdn = w_mlp_down.astype(BF16)
    win_e, wuq, wuk, wuvt, gains = _prep_even_weights(even_w_in, even_w_uq, even_w_ukv, even_q_norm_g,
                                                      even_k_norm_g, lora, pool_width)
    poolw = even_pool_w.astype(BF16)
    wout_e = even_w_out.astype(BF16)
    win_o = odd_w_in.astype(BF16)
    wout_o = odd_w_out.astype(BF16)

    tab = _rope_table_call(positions.reshape(T, 1), _pick_tile(T, 1024))

    xf = x.reshape(T, D)
    for layer in range(depth):
        g_mix = mix_norm_g[layer][None, :]
        if layer % 2 == 0:
            e = layer // 2
            q, k, vt, b = _even_in_call(
                xf, g_mix, win_e[e], even_q_a_norm_g[e][None, :], even_kv_a_norm_g[e][None, :],
                wuq[e], wuk[e], wuvt[e], gains[e], poolw[e], even_pool_scale[e][None, :], tab,
                B=B, S=S, tm=tm_even, tk=tq)
            a = _attn_call(q, k, vt, tq=tq).reshape(T, MLA_HEADS * V_DIM)
            xf = _outproj_call(xf, a, b, wout_e, e, tm=tm_mlp)
        else:
            o = layer // 2
            xf = _conv_call(xf, g_mix, win_o, odd_conv_w, wout_o, o, S=S, tm=tm_mlp, tc=tc_conv)
        xf = _mlp_call(xf, mlp_norm_g[layer][None, :], wup, wdn, layer, tm=tm_mlp, tf=tf_mlp)
    return xf.reshape(B, S, D)
```

```python
import functools

import jax
import jax.numpy as jnp
import numpy as np
from jax import lax
from jax.experimental import pallas as pl
from jax.experimental.pallas import tpu as pltpu

F32 = jnp.float32
BF16 = jnp.bfloat16

RMS_EPS = 1e-6
ROPE_THETA = 10000.0
MLA_HEADS = 8
NOPE_DIM = 128
ROPE_DIM = 64
QK_DIM = NOPE_DIM + ROPE_DIM
V_DIM = 128
VT_ROWS = V_DIM + 16
LOG2_E = 1.4426950408889634
POOL_WINDOWS = (2, 4, 8, 16)
POOL_HISTORY = 16
CONV_WIDTH = 3
CONV_HISTORY = 8
LANES = 128

VMEM_LIMIT_BYTES = 56 * 1024 * 1024


def _rms(xf, g):
    ms = jnp.mean(xf * xf, axis=-1, keepdims=True)
    return xf * lax.rsqrt(ms + RMS_EPS) * g


def _dot(a, b):
    return jnp.dot(a, b, preferred_element_type=F32)


def _params(*sem):
    return pltpu.CompilerParams(dimension_semantics=sem, vmem_limit_bytes=VMEM_LIMIT_BYTES)


def _resident(block_shape, index_map):
    return pl.BlockSpec(block_shape, index_map, pipeline_mode=pl.Buffered(1))


def _rope_table_kernel(pos_ref, freq_ref, tab_ref):
    ang = pos_ref[...].astype(F32) * freq_ref[...]
    lane = lax.broadcasted_iota(jnp.int32, ang.shape, 1)
    sign = jnp.where((lane % ROPE_DIM) < ROPE_DIM // 2, -1.0, 1.0)
    tab_ref[:, :LANES] = jnp.cos(ang)
    tab_ref[:, LANES:] = jnp.sin(ang) * sign


def _rope_table_call(pos_col, tm):
    T = pos_col.shape[0]
    inv_freq = 1.0 / (ROPE_THETA ** (jnp.arange(0, ROPE_DIM, 2, dtype=F32) / ROPE_DIM))
    freq = jnp.tile(inv_freq, LANES // (ROPE_DIM // 2))[None, :]
    return pl.pallas_call(
        _rope_table_kernel,
        grid=(T // tm,),
        in_specs=[pl.BlockSpec((tm, 1), lambda i: (i, 0)),
                  pl.BlockSpec((1, LANES), lambda i: (0, 0))],
        out_specs=pl.BlockSpec((tm, 2 * LANES), lambda i: (i, 0)),
        out_shape=jax.ShapeDtypeStruct((T, 2 * LANES), F32),
        compiler_params=_params("parallel"),
        name="rope_table",
    )(pos_col, freq)


def _even_in_kernel(x_ref, g_ref, win_ref, qag_ref, kvag_ref, wuq_ref, wuk_ref, wuvt_ref, gains_ref,
                    poolw_ref, pscale_ref, tab_ref,
                    q_ref, k_ref, vt_ref, b_ref, carry_ref, *, tm, lora, pool_width, scale):
    s = pl.program_id(1)
    hn = _rms(x_ref[...], g_ref[...]).astype(BF16)
    proj = _dot(hn, win_ref[...])
    off_pool = 2 * lora
    off_kr = off_pool + pool_width
    cqn = _rms(proj[:, :lora], qag_ref[...]).astype(BF16)
    ckvn = _rms(proj[:, lora:2 * lora], kvag_ref[...]).astype(BF16)
    u = proj[:, off_pool:off_kr]
    kr2 = proj[:, off_kr:off_kr + LANES]
    krs2 = proj[:, off_kr + LANES:off_kr + 2 * LANES]

    qall = _dot(cqn, wuq_ref[...])
    knall = _dot(ckvn, wuk_ref[...])
    vt_all = lax.dot_general(wuvt_ref[...], ckvn, (((1,), (1,)), ((), ())), preferred_element_type=F32)

    cos2 = tab_ref[:, :LANES]
    sin2 = tab_ref[:, LANES:]
    gains = gains_ref[...]
    qg_nope, qg_rope2, qg_ropes2 = gains[0:1], gains[1:2], gains[2:3]
    kg_nope, kg_rope2, kg_ropes2 = gains[3:4], gains[4:5], gains[5:6]
    lane = lax.broadcasted_iota(jnp.int32, (tm, LANES), 1)
    low_half = lane < ROPE_DIM
    nheads = MLA_HEADS
    nope_w = nheads * NOPE_DIM

    kr_sq = jnp.where(low_half, kr2 * kr2, 0.0)
    k_roped2 = kr2 * kg_rope2 * cos2 + krs2 * kg_ropes2 * sin2
    for h in range(nheads):
        kn = knall[:, h * NOPE_DIM:(h + 1) * NOPE_DIM]
        ss = jnp.sum(kn * kn + kr_sq, axis=-1, keepdims=True)
        r = lax.rsqrt(ss * (1.0 / QK_DIM) + RMS_EPS)
        k_ref[0, h, :, 0:NOPE_DIM] = (kn * r * kg_nope).astype(BF16)
        k_ref[0, h, :, NOPE_DIM:QK_DIM] = (k_roped2 * r)[:, :ROPE_DIM].astype(BF16)
        vt_ref[0, h, 0, 0:V_DIM, :] = vt_all[h * V_DIM:(h + 1) * V_DIM].astype(BF16)
        vt_ref[0, h, 0, V_DIM:VT_ROWS, :] = jnp.ones((VT_ROWS - V_DIM, tm), BF16)

    rope_w = nheads * ROPE_DIM
    for p in range(nheads // 2):
        qr2 = qall[:, nope_w + p * LANES:nope_w + (p + 1) * LANES]
        qrs2 = qall[:, nope_w + rope_w + p * LANES:nope_w + rope_w + (p + 1) * LANES]
        sq = qr2 * qr2
        roped = qr2 * qg_rope2 * cos2 + qrs2 * qg_ropes2 * sin2
        for e in range(2):
            h = 2 * p + e
            qn = qall[:, h * NOPE_DIM:(h + 1) * NOPE_DIM]
            mine = low_half if e == 0 else jnp.logical_not(low_half)
            ss = jnp.sum(qn * qn + jnp.where(mine, sq, 0.0), axis=-1, keepdims=True)
            r = lax.rsqrt(ss * (1.0 / QK_DIM) + RMS_EPS) * scale
            q_ref[0, h, :, 0:NOPE_DIM] = (qn * r * qg_nope).astype(BF16)
            rp = roped * r
            if e == 1:
                rp = pltpu.roll(rp, ROPE_DIM, axis=1)
            q_ref[0, h, :, NOPE_DIM:QK_DIM] = rp[:, :ROPE_DIM].astype(BF16)

    prev = jnp.where(s == 0, 0.0, carry_ref[...])
    carry_ref[...] = u[tm - POOL_HISTORY:tm]
    ext = jnp.concatenate([prev, u], axis=0)
    t_in_seq = s * tm + lax.broadcasted_iota(jnp.int32, (tm, 1), 0)
    gd = pool_width // len(POOL_WINDOWS)
    pscale = pscale_ref[...]
    for gi, w in enumerate(POOL_WINDOWS):
        e = ext[:, gi * gd:(gi + 1) * gd]
        sh = 1
        while sh < w:
            e = e + pltpu.roll(e, sh, axis=0)
            sh *= 2
        inv_cnt = 1.0 / jnp.minimum(t_in_seq + 1, w).astype(F32)
        pooled = (e[POOL_HISTORY:] * inv_cnt - u[:, gi * gd:(gi + 1) * gd]).astype(BF16)
        y = _dot(pooled, poolw_ref[gi]) * pscale[:, gi * gd:(gi + 1) * gd]
        b_ref[:, gi * gd:(gi + 1) * gd] = y.astype(BF16)


def _even_in_call(xf, g, win, qag, kvag, wuq, wuk, wuvt, gains, poolw, pscale, tab, *, B, S, tm, tk):
    T, D = xf.shape
    lora = qag.shape[1]
    pool_width = pscale.shape[1]
    n_s = S // tm
    H = MLA_HEADS
    per_kv_block = tk // tm
    tok = lambda b, s: (b * n_s + s, 0)
    const2 = lambda b, s: (0, 0)
    kern = functools.partial(_even_in_kernel, tm=tm, lora=lora, pool_width=pool_width,
                             scale=QK_DIM ** -0.5 * LOG2_E)
    return pl.pallas_call(
        kern,
        grid=(B, n_s),
        in_specs=[
            pl.BlockSpec((tm, D), tok),
            _resident((1, D), const2),
            _resident(win.shape, const2),
            _resident((1, lora), const2),
            _resident((1, lora), const2),
            _resident(wuq.shape, const2),
            _resident(wuk.shape, const2),
            _resident(wuvt.shape, const2),
            _resident(gains.shape, const2),
            _resident(poolw.shape, lambda b, s: (0, 0, 0)),
            _resident((1, pool_width), const2),
            pl.BlockSpec((tm, 2 * LANES), tok),
        ],
        out_specs=[
            pl.BlockSpec((1, H, tm, QK_DIM), lambda b, s: (b, 0, s, 0)),
            pl.BlockSpec((1, H, tm, QK_DIM), lambda b, s: (b, 0, s, 0)),
            pl.BlockSpec((1, H, 1, VT_ROWS, tm),
                         lambda b, s: (b, 0, s // per_kv_block, 0, s % per_kv_block)),
            pl.BlockSpec((tm, pool_width), tok),
        ],
        out_shape=[
            jax.ShapeDtypeStruct((B, H, S, QK_DIM), BF16),
            jax.ShapeDtypeStruct((B, H, S, QK_DIM), BF16),
            jax.ShapeDtypeStruct((B, H, S // tk, VT_ROWS, tk), BF16),
            jax.ShapeDtypeStruct((T, pool_width), BF16),
        ],
        scratch_shapes=[pltpu.VMEM((POOL_HISTORY, pool_width), F32)],
        compiler_params=_params("arbitrary", "arbitrary"),
        name="even_in",
    )(xf, g, win, qag, kvag, wuq, wuk, wuvt, gains, poolw, pscale, tab)


ATTN_HEADS_PER_STEP = 4
ATTN_SLOTS = 2
ATTN_SCRATCH_PER_HEAD = 2 + 4 * ATTN_SLOTS


def _attn_kernel(q_ref, k_ref, vt_ref, o_ref, *scratch, tq):
    i = pl.program_id(2)
    heads = range(ATTN_HEADS_PER_STEP)
    ns = ATTN_SLOTS
    per_head = [scratch[h * ATTN_SCRATCH_PER_HEAD:(h + 1) * ATTN_SCRATCH_PER_HEAD] for h in heads]
    m_refs = [r[0] for r in per_head]
    acc_refs = [r[1] for r in per_head]
    s_refs = [r[2:2 + ns] for r in per_head]
    p_refs = [r[2 + ns:2 + 2 * ns] for r in per_head]
    a_refs = [r[2 + 2 * ns:2 + 3 * ns] for r in per_head]
    bm_refs = [r[2 + 3 * ns:2 + 4 * ns] for r in per_head]

    def scores(h, j, slot):
        start = pl.multiple_of(j * tq, tq)
        kb = k_ref[0, h, pl.ds(start, tq), :]
        s = lax.dot_general(kb, q_ref[0, h], (((1,), (1,)), ((), ())), preferred_element_type=F32)
        s_refs[h][slot][...] = s
        bm_refs[h][slot][...] = jnp.max(s, axis=0, keepdims=True)

    def softmax(h, slot, masked):
        s = s_refs[h][slot][...]
        if masked:
            key = lax.broadcasted_iota(jnp.int32, s.shape, 0)
            qry = lax.broadcasted_iota(jnp.int32, s.shape, 1)
            s = jnp.where(key <= qry, s, -1e30)
            block_max = jnp.max(s, axis=0, keepdims=True)
        else:
            block_max = bm_refs[h][slot][...]
        m_prev = m_refs[h][...]
        m_new = jnp.maximum(m_prev, block_max)
        m_refs[h][...] = m_new
        a_refs[h][slot][...] = jnp.exp2(m_prev - m_new)
        p_refs[h][slot][...] = jnp.exp2(s - m_new).astype(BF16)

    def accumulate(h, j, slot):
        acc_refs[h][...] = (a_refs[h][slot][...] * acc_refs[h][...]
                            + _dot(vt_ref[0, h, j], p_refs[h][slot][...]))

    def step(j, slot):
        for h in heads:
            scores(h, j + 1, (slot + 1) % ns)
            accumulate(h, jnp.maximum(j - 1, 0), (slot - 1) % ns)
            softmax(h, slot, False)

    def last(slot):
        for h in heads:
            accumulate(h, jnp.maximum(i - 1, 0), (slot - 1) % ns)
            softmax(h, slot, True)
            accumulate(h, i, slot)
            acc = acc_refs[h][...]
            out_t = acc[0:V_DIM] / acc[V_DIM:V_DIM + 1]
            o_ref[0, :, h * V_DIM:(h + 1) * V_DIM] = out_t.T.astype(BF16)

    for h in heads:
        m_refs[h][...] = jnp.full(m_refs[h].shape, -jnp.inf, F32)
        acc_refs[h][...] = jnp.zeros(acc_refs[h].shape, F32)
        scores(h, 0, 0)
        p_refs[h][ns - 1][...] = jnp.zeros(p_refs[h][ns - 1].shape, BF16)
        a_refs[h][ns - 1][...] = jnp.ones(a_refs[h][ns - 1].shape, F32)

    def group(g, carry):
        for slot in range(ns):
            step(ns * g + slot, slot)
        return carry

    lax.fori_loop(0, i // ns, group, 0)

    base = (i // ns) * ns
    for rem in range(ns):
        @pl.when(i % ns == rem)
        def _(rem=rem):
            for slot in range(rem):
                step(base + slot, slot)
            last(rem)


def _attn_call(q, k, vt, *, tq):
    B, H, S, _ = q.shape
    n_kv = vt.shape[2]
    assert vt.shape[4] == tq
    hps = ATTN_HEADS_PER_STEP
    head_scratch = ([pltpu.VMEM((1, tq), F32), pltpu.VMEM((VT_ROWS, tq), F32)]
                    + [pltpu.VMEM((tq, tq), F32)] * ATTN_SLOTS
                    + [pltpu.VMEM((tq, tq), BF16)] * ATTN_SLOTS
                    + [pltpu.VMEM((1, tq), F32)] * (2 * ATTN_SLOTS))
    assert len(head_scratch) == ATTN_SCRATCH_PER_HEAD
    return pl.pallas_call(
        functools.partial(_attn_kernel, tq=tq),
        grid=(B, H // hps, S // tq),
        in_specs=[
            pl.BlockSpec((1, hps, tq, QK_DIM), lambda b, h, i: (b, h, i, 0)),
            pl.BlockSpec((1, hps, S, QK_DIM), lambda b, h, i: (b, h, 0, 0)),
            pl.BlockSpec((1, hps, n_kv, VT_ROWS, tq), lambda b, h, i: (b, h, 0, 0, 0)),
        ],
        out_specs=pl.BlockSpec((1, tq, hps * V_DIM), lambda b, h, i: (b, i, h)),
        out_shape=jax.ShapeDtypeStruct((B, S, H * V_DIM), BF16),
        scratch_shapes=head_scratch * hps,
        compiler_params=_params("parallel", "parallel", "arbitrary"),
        name="attn",
    )(q, k, vt)


def _outproj_kernel(x_ref, a_ref, b_ref, wa_ref, wb_ref, o_ref):
    o_ref[...] = x_ref[...] + _dot(a_ref[...], wa_ref[...]) + _dot(b_ref[...], wb_ref[...])


def _outproj_call(xf, a, b, wout, e, *, tm):
    T, D = xf.shape
    wa_rows = a.shape[1]
    wb_rows = b.shape[1]
    assert wa_rows == wb_rows
    return pl.pallas_call(
        _outproj_kernel,
        grid=(T // tm,),
        in_specs=[
            pl.BlockSpec((tm, D), lambda i: (i, 0)),
            pl.BlockSpec((tm, wa_rows), lambda i: (i, 0)),
            pl.BlockSpec((tm, wb_rows), lambda i: (i, 0)),
            _resident((None, wa_rows, D), lambda i: (e, 0, 0)),
            _resident((None, wb_rows, D), lambda i: (e, 1, 0)),
        ],
        out_specs=pl.BlockSpec((tm, D), lambda i: (i, 0)),
        out_shape=jax.ShapeDtypeStruct((T, D), F32),
        compiler_params=_params("parallel"),
        name="outproj",
    )(xf, a, b, wout, wout)


def _conv_kernel(x_ref, g_ref, wb_ref, wc_ref, wu_ref, cw_ref, wo_ref, o_ref, hn_ref, carry_ref,
                 *, tm, tiles_per_seq):
    i = pl.program_id(0)
    c = pl.program_id(1)

    @pl.when(c == 0)
    def _():
        x = x_ref[...]
        hn_ref[...] = _rms(x, g_ref[...]).astype(BF16)
        o_ref[...] = x

    hn = hn_ref[...]
    gate_b = _dot(hn, wb_ref[...])
    v = _dot(hn, wc_ref[...]) * _dot(hn, wu_ref[...])
    prev = jnp.where(i % tiles_per_seq == 0, 0.0, carry_ref[c])
    carry_ref[c] = v[tm - CONV_HISTORY:tm]
    ext = jnp.concatenate([prev, v], axis=0)
    cw = cw_ref[...]
    conv = cw[CONV_WIDTH - 1:CONV_WIDTH] * v
    for back in range(1, CONV_WIDTH):
        tap = CONV_WIDTH - 1 - back
        conv = conv + cw[tap:tap + 1] * pltpu.roll(ext, back, axis=0)[CONV_HISTORY:]
    o_ref[...] += _dot((gate_b * conv).astype(BF16), wo_ref[...])


def _conv_call(xf, g, win, convw, wout, o, *, S, tm, tc):
    T, D = xf.shape
    C = wout.shape[1]
    n_c = C // tc
    kern = functools.partial(_conv_kernel, tm=tm, tiles_per_seq=S // tm)
    return pl.pallas_call(
        kern,
        grid=(T // tm, n_c),
        in_specs=[
            pl.BlockSpec((tm, D), lambda i, c: (i, 0)),
            pl.BlockSpec((1, D), lambda i, c: (0, 0)),
            pl.BlockSpec((None, D, tc), lambda i, c: (o, 0, c)),
            pl.BlockSpec((None, D, tc), lambda i, c: (o, 0, n_c + c)),
            pl.BlockSpec((None, D, tc), lambda i, c: (o, 0, 2 * n_c + c)),
            pl.BlockSpec((None, CONV_WIDTH, tc), lambda i, c: (o, 0, c)),
            pl.BlockSpec((None, tc, D), lambda i, c: (o, c, 0)),
        ],
        out_specs=pl.BlockSpec((tm, D), lambda i, c: (i, 0)),
        out_shape=jax.ShapeDtypeStruct((T, D), F32),
        scratch_shapes=[pltpu.VMEM((tm, D), BF16),
                        pltpu.VMEM((n_c, CONV_HISTORY, tc), F32)],
        compiler_params=_params("arbitrary", "arbitrary"),
        name="conv_mixer",
    )(xf, g, win, win, win, convw, wout)


def _mlp_kernel(x_ref, g_ref, wup_ref, wdn_ref, o_ref, hn_ref):
    @pl.when(pl.program_id(1) == 0)
    def _():
        x = x_ref[...]
        hn_ref[...] = _rms(x, g_ref[...]).astype(BF16)
        o_ref[...] = x

    up = _dot(hn_ref[...], wup_ref[...])
    act = jnp.square(jnp.maximum(up, 0.0)).astype(BF16)
    o_ref[...] += _dot(act, wdn_ref[...])


def _mlp_call(xf, g, wup, wdn, layer, *, tm, tf):
    T, D = xf.shape
    F = wup.shape[2]
    return pl.pallas_call(
        _mlp_kernel,
        grid=(T // tm, F // tf),
        in_specs=[
            pl.BlockSpec((tm, D), lambda i, f: (i, 0)),
            pl.BlockSpec((1, D), lambda i, f: (0, 0)),
            pl.BlockSpec((None, D, tf), lambda i, f: (layer, 0, f)),
            pl.BlockSpec((None, tf, D), lambda i, f: (layer, f, 0)),
        ],
        out_specs=pl.BlockSpec((tm, D), lambda i, f: (i, 0)),
        out_shape=jax.ShapeDtypeStruct((T, D), F32),
        scratch_shapes=[pltpu.VMEM((tm, D), BF16)],
        compiler_params=_params("parallel", "arbitrary"),
        name="mlp",
    )(xf, g, wup, wdn)


def _swap_halves_idx(n):
    return (np.arange(n) + n // 2) % n


def _prep_even_weights(even_w_in, even_w_uq, even_w_ukv, even_q_norm_g, even_k_norm_g, lora, pool_width):
    H = MLA_HEADS
    off_kr = 2 * lora
    off_pool = off_kr + ROPE_DIM
    kr = off_kr + np.arange(ROPE_DIM)
    krs = off_kr + _swap_halves_idx(ROPE_DIM)
    in_perm = np.concatenate([np.arange(off_kr), off_pool + np.arange(pool_width), kr, kr, krs, krs])
    win = even_w_in.astype(BF16)[:, :, in_perm]

    head0 = np.arange(H)[:, None] * QK_DIM
    q_nope = (head0 + np.arange(NOPE_DIM)[None, :]).reshape(-1)
    q_rope = (head0 + NOPE_DIM + np.arange(ROPE_DIM)[None, :]).reshape(-1)
    q_ropes = (head0 + NOPE_DIM + _swap_halves_idx(ROPE_DIM)[None, :]).reshape(-1)
    wuq = even_w_uq.astype(BF16)[:, :, np.concatenate([q_nope, q_rope, q_ropes])]

    kv0 = np.arange(H)[:, None] * (NOPE_DIM + V_DIM)
    k_nope = (kv0 + np.arange(NOPE_DIM)[None, :]).reshape(-1)
    v_cols = (kv0 + NOPE_DIM + np.arange(V_DIM)[None, :]).reshape(-1)
    wukv = even_w_ukv.astype(BF16)
    wuk = wukv[:, :, k_nope]
    wuvt = jnp.swapaxes(wukv[:, :, v_cols], 1, 2)

    def gain_rows(gvec):
        rope = gvec[:, NOPE_DIM:]
        ropes = rope[:, _swap_halves_idx(ROPE_DIM)]
        return [gvec[:, :NOPE_DIM], jnp.concatenate([rope, rope], -1), jnp.concatenate([ropes, ropes], -1)]

    rows = gain_rows(even_q_norm_g) + gain_rows(even_k_norm_g)
    rows += [jnp.zeros_like(rows[0])] * 2
    gains = jnp.stack(rows, axis=1).astype(F32)
    return win, wuq, wuk, wuvt, gains


def _pick_tile(n, want):
    t = min(n, want)
    while n % t:
        t //= 2
    return t


def kernel(x, positions, mix_norm_g, mlp_norm_g, w_mlp_up, w_mlp_down, even_w_in, even_q_a_norm_g,
           even_kv_a_norm_g, even_w_uq, even_w_ukv, even_q_norm_g, even_k_norm_g, even_pool_w,
           even_pool_scale, even_w_out, odd_w_in, odd_conv_w, odd_w_out):
    B, S, D = x.shape
    T = B * S
    depth = mix_norm_g.shape[0]
    lora = even_q_a_norm_g.shape[1]
    pool_width = even_pool_scale.shape[1]

    tm_mlp = _pick_tile(S, 512)
    tf_mlp = _pick_tile(w_mlp_up.shape[2], 1024)
    tm_even = _pick_tile(S, 512)
    tq = _pick_tile(S, 512)
    tc_conv = _pick_tile(odd_w_out.shape[1], 512)

    wup = w_mlp_up.astype(BF16)
    wdn = w_mlp_down.astype(BF16)
    win_e, wuq, wuk, wuvt, gains = _prep_even_weights(even_w_in, even_w_uq, even_w_ukv, even_q_norm_g,
                                                      even_k_norm_g, lora, pool_width)
    poolw = even_pool_w.astype(BF16)
    wout_e = even_w_out.astype(BF16)
    win_o = odd_w_in.astype(BF16)
    wout_o = odd_w_out.astype(BF16)

    tab = _rope_table_call(positions.reshape(T, 1), _pick_tile(T, 1024))

    xf = x.reshape(T, D)
    for layer in range(depth):
        g_mix = mix_norm_g[layer][None, :]
        if layer % 2 == 0:
            e = layer // 2
            q, k, vt, b = _even_in_call(
                xf, g_mix, win_e[e], even_q_a_norm_g[e][None, :], even_kv_a_norm_g[e][None, :],
                wuq[e], wuk[e], wuvt[e], gains[e], poolw[e], even_pool_scale[e][None, :], tab,
                B=B, S=S, tm=tm_even, tk=tq)
            a = _attn_call(q, k, vt, tq=tq).reshape(T, MLA_HEADS * V_DIM)
            xf = _outproj_call(xf, a, b, wout_e, e, tm=tm_mlp)
        else:
            o = layer // 2
            xf = _conv_call(xf, g_mix, win_o, odd_conv_w, wout_o, o, S=S, tm=tm_mlp, tc=tc_conv)
        xf = _mlp_call(xf, mlp_norm_g[layer][None, :], wup, wdn, layer, tm=tm_mlp, tf=tf_mlp)
    return xf.reshape(B, S, D)
```

```python
import functools

import jax
import jax.numpy as jnp
import numpy as np
from jax import lax
from jax.experimental import pallas as pl
from jax.experimental.pallas import tpu as pltpu

F32 = jnp.float32
BF16 = jnp.bfloat16

RMS_EPS = 1e-6
ROPE_THETA = 10000.0
MLA_HEADS = 8
NOPE_DIM = 128
ROPE_DIM = 64
QK_DIM = NOPE_DIM + ROPE_DIM
V_DIM = 128
VT_ROWS = V_DIM + 16
LOG2_E = 1.4426950408889634
POOL_WINDOWS = (2, 4, 8, 16)
POOL_HISTORY = 16
CONV_WIDTH = 3
CONV_HISTORY = 8
LANES = 128

VMEM_LIMIT_BYTES = 56 * 1024 * 1024


def _rms(xf, g):
    ms = jnp.mean(xf * xf, axis=-1, keepdims=True)
    return xf * lax.rsqrt(ms + RMS_EPS) * g


def _dot(a, b):
    return jnp.dot(a, b, preferred_element_type=F32)


def _params(*sem):
    return pltpu.CompilerParams(dimension_semantics=sem, vmem_limit_bytes=VMEM_LIMIT_BYTES)


def _resident(block_shape, index_map):
    return pl.BlockSpec(block_shape, index_map, pipeline_mode=pl.Buffered(1))


def _rope_table_kernel(pos_ref, freq_ref, tab_ref):
    ang = pos_ref[...].astype(F32) * freq_ref[...]
    lane = lax.broadcasted_iota(jnp.int32, ang.shape, 1)
    sign = jnp.where((lane % ROPE_DIM) < ROPE_DIM // 2, -1.0, 1.0)
    tab_ref[:, :LANES] = jnp.cos(ang)
    tab_ref[:, LANES:] = jnp.sin(ang) * sign


def _rope_table_call(pos_col, tm):
    T = pos_col.shape[0]
    inv_freq = 1.0 / (ROPE_THETA ** (jnp.arange(0, ROPE_DIM, 2, dtype=F32) / ROPE_DIM))
    freq = jnp.tile(inv_freq, LANES // (ROPE_DIM // 2))[None, :]
    return pl.pallas_call(
        _rope_table_kernel,
        grid=(T // tm,),
        in_specs=[pl.BlockSpec((tm, 1), lambda i: (i, 0)),
                  pl.BlockSpec((1, LANES), lambda i: (0, 0))],
        out_specs=pl.BlockSpec((tm, 2 * LANES), lambda i: (i, 0)),
        out_shape=jax.ShapeDtypeStruct((T, 2 * LANES), F32),
        compiler_params=_params("parallel"),
        name="rope_table",
    )(pos_col, freq)


def _even_in_kernel(x_ref, g_ref, win_ref, qag_ref, kvag_ref, wuq_ref, wuk_ref, wuvt_ref, gains_ref,
                    poolw_ref, pscale_ref, tab_ref,
                    q_ref, k_ref, vt_ref, b_ref, carry_ref, *, tm, lora, pool_width, scale):
    s = pl.program_id(1)
    hn = _rms(x_ref[...], g_ref[...]).astype(BF16)
    proj = _dot(hn, win_ref[...])
    off_pool = 2 * lora
    off_kr = off_pool + pool_width
    cqn = _rms(proj[:, :lora], qag_ref[...]).astype(BF16)
    ckvn = _rms(proj[:, lora:2 * lora], kvag_ref[...]).astype(BF16)
    u = proj[:, off_pool:off_kr]
    kr2 = proj[:, off_kr:off_kr + LANES]
    krs2 = proj[:, off_kr + LANES:off_kr + 2 * LANES]

    qall = _dot(cqn, wuq_ref[...])
    knall = _dot(ckvn, wuk_ref[...])
    vt_all = lax.dot_general(wuvt_ref[...], ckvn, (((1,), (1,)), ((), ())), preferred_element_type=F32)

    cos2 = tab_ref[:, :LANES]
    sin2 = tab_ref[:, LANES:]
    gains = gains_ref[...]
    qg_nope, qg_rope2, qg_ropes2 = gains[0:1], gains[1:2], gains[2:3]
    kg_nope, kg_rope2, kg_ropes2 = gains[3:4], gains[4:5], gains[5:6]
    lane = lax.broadcasted_iota(jnp.int32, (tm, LANES), 1)
    low_half = lane < ROPE_DIM
    nheads = MLA_HEADS
    nope_w = nheads * NOPE_DIM

    kr_sq = jnp.where(low_half, kr2 * kr2, 0.0)
    k_roped2 = kr2 * kg_rope2 * cos2 + krs2 * kg_ropes2 * sin2
    for h in range(nheads):
        kn = knall[:, h * NOPE_DIM:(h + 1) * NOPE_DIM]
        ss = jnp.sum(kn * kn + kr_sq, axis=-1, keepdims=True)
        r = lax.rsqrt(ss * (1.0 / QK_DIM) + RMS_EPS)
        k_ref[0, h, :, 0:NOPE_DIM] = (kn * r * kg_nope).astype(BF16)
        k_ref[0, h, :, NOPE_DIM:QK_DIM] = (k_roped2 * r)[:, :ROPE_DIM].astype(BF16)
        vt_ref[0, h, 0, 0:V_DIM, :] = vt_all[h * V_DIM:(h + 1) * V_DIM].astype(BF16)
        vt_ref[0, h, 0, V_DIM:VT_ROWS, :] = jnp.ones((VT_ROWS - V_DIM, tm), BF16)

    rope_w = nheads * ROPE_DIM
    for p in range(nheads // 2):
        qr2 = qall[:, nope_w + p * LANES:nope_w + (p + 1) * LANES]
        qrs2 = qall[:, nope_w + rope_w + p * LANES:nope_w + rope_w + (p + 1) * LANES]
        sq = qr2 * qr2
        roped = qr2 * qg_rope2 * cos2 + qrs2 * qg_ropes2 * sin2
        for e in range(2):
            h = 2 * p + e
            qn = qall[:, h * NOPE_DIM:(h + 1) * NOPE_DIM]
            mine = low_half if e == 0 else jnp.logical_not(low_half)
            ss = jnp.sum(qn * qn + jnp.where(mine, sq, 0.0), axis=-1, keepdims=True)
            r = lax.rsqrt(ss * (1.0 / QK_DIM) + RMS_EPS) * scale
            q_ref[0, h, :, 0:NOPE_DIM] = (qn * r * qg_nope).astype(BF16)
            rp = roped * r
            if e == 1:
                rp = pltpu.roll(rp, ROPE_DIM, axis=1)
            q_ref[0, h, :, NOPE_DIM:QK_DIM] = rp[:, :ROPE_DIM].astype(BF16)

    prev = jnp.where(s == 0, 0.0, carry_ref[...])
    carry_ref[...] = u[tm - POOL_HISTORY:tm]
    ext = jnp.concatenate([prev, u], axis=0)
    t_in_seq = s * tm + lax.broadcasted_iota(jnp.int32, (tm, 1), 0)
    gd = pool_width // len(POOL_WINDOWS)
    pscale = pscale_ref[...]
    for gi, w in enumerate(POOL_WINDOWS):
        e = ext[:, gi * gd:(gi + 1) * gd]
        sh = 1
        while sh < w:
            e = e + pltpu.roll(e, sh, axis=0)
            sh *= 2
        inv_cnt = 1.0 / jnp.minimum(t_in_seq + 1, w).astype(F32)
        pooled = (e[POOL_HISTORY:] * inv_cnt - u[:, gi * gd:(gi + 1) * gd]).astype(BF16)
        y = _dot(pooled, poolw_ref[gi]) * pscale[:, gi * gd:(gi + 1) * gd]
        b_ref[:, gi * gd:(gi + 1) * gd] = y.astype(BF16)


def _even_in_call(xf, g, win, qag, kvag, wuq, wuk, wuvt, gains, poolw, pscale, tab, *, B, S, tm, tk):
    T, D = xf.shape
    lora = qag.shape[1]
    pool_width = pscale.shape[1]
    n_s = S // tm
    H = MLA_HEADS
    per_kv_block = tk // tm
    tok = lambda b, s: (b * n_s + s, 0)
    const2 = lambda b, s: (0, 0)
    kern = functools.partial(_even_in_kernel, tm=tm, lora=lora, pool_width=pool_width,
                             scale=QK_DIM ** -0.5 * LOG2_E)
    return pl.pallas_call(
        kern,
        grid=(B, n_s),
        in_specs=[
            pl.BlockSpec((tm, D), tok),
            _resident((1, D), const2),
            _resident(win.shape, const2),
            _resident((1, lora), const2),
            _resident((1, lora), const2),
            _resident(wuq.shape, const2),
            _resident(wuk.shape, const2),
            _resident(wuvt.shape, const2),
            _resident(gains.shape, const2),
            _resident(poolw.shape, lambda b, s: (0, 0, 0)),
            _resident((1, pool_width), const2),
            pl.BlockSpec((tm, 2 * LANES), tok),
        ],
        out_specs=[
            pl.BlockSpec((1, H, tm, QK_DIM), lambda b, s: (b, 0, s, 0)),
            pl.BlockSpec((1, H, tm, QK_DIM), lambda b, s: (b, 0, s, 0)),
            pl.BlockSpec((1, H, 1, VT_ROWS, tm),
                         lambda b, s: (b, 0, s // per_kv_block, 0, s % per_kv_block)),
            pl.BlockSpec((tm, pool_width), tok),
        ],
        out_shape=[
            jax.ShapeDtypeStruct((B, H, S, QK_DIM), BF16),
            jax.ShapeDtypeStruct((B, H, S, QK_DIM), BF16),
            jax.ShapeDtypeStruct((B, H, S // tk, VT_ROWS, tk), BF16),
            jax.ShapeDtypeStruct((T, pool_width), BF16),
        ],
        scratch_shapes=[pltpu.VMEM((POOL_HISTORY, pool_width), F32)],
        compiler_params=_params("arbitrary", "arbitrary"),
        name="even_in",
    )(xf, g, win, qag, kvag, wuq, wuk, wuvt, gains, poolw, pscale, tab)


ATTN_HEADS_PER_STEP = 4
ATTN_SLOTS = 2
ATTN_SCRATCH_PER_HEAD = 2 + 4 * ATTN_SLOTS


def _attn_kernel(q_ref, k_ref, vt_ref, o_ref, *scratch, tq):
    i = pl.program_id(2)
    heads = range(ATTN_HEADS_PER_STEP)
    ns = ATTN_SLOTS
    per_head = [scratch[h * ATTN_SCRATCH_PER_HEAD:(h + 1) * ATTN_SCRATCH_PER_HEAD] for h in heads]
    m_refs = [r[0] for r in per_head]
    acc_refs = [r[1] for r in per_head]
    s_refs = [r[2:2 + ns] for r in per_head]
    p_refs = [r[2 + ns:2 + 2 * ns] for r in per_head]
    a_refs = [r[2 + 2 * ns:2 + 3 * ns] for r in per_head]
    bm_refs = [r[2 + 3 * ns:2 + 4 * ns] for r in per_head]

    def scores(h, j, slot):
        start = pl.multiple_of(j * tq, tq)
        kb = k_ref[0, h, pl.ds(start, tq), :]
        s = lax.dot_general(kb, q_ref[0, h], (((1,), (1,)), ((), ())), preferred_element_type=F32)
        s_refs[h][slot][...] = s
        bm_refs[h][slot][...] = jnp.max(s, axis=0, keepdims=True)

    def softmax(h, slot, masked):
        s = s_refs[h][slot][...]
        if masked:
            key = lax.broadcasted_iota(jnp.int32, s.shape, 0)
            qry = lax.broadcasted_iota(jnp.int32, s.shape, 1)
            s = jnp.where(key <= qry, s, -1e30)
            block_max = jnp.max(s, axis=0, keepdims=True)
        else:
            block_max = bm_refs[h][slot][...]
        m_prev = m_refs[h][...]
        m_new = jnp.maximum(m_prev, block_max)
        m_refs[h][...] = m_new
        a_refs[h][slot][...] = jnp.exp2(m_prev - m_new)
        p_refs[h][slot][...] = jnp.exp2(s - m_new).astype(BF16)

    def accumulate(h, j, slot):
        acc_refs[h][...] = (a_refs[h][slot][...] * acc_refs[h][...]
                            + _dot(vt_ref[0, h, j], p_refs[h][slot][...]))

    def step(j, slot):
        for h in heads:
            scores(h, j + 1, (slot + 1) % ns)
            accumulate(h, jnp.maximum(j - 1, 0), (slot - 1) % ns)
            softmax(h, slot, False)

    def last(slot):
        for h in heads:
            accumulate(h, jnp.maximum(i - 1, 0), (slot - 1) % ns)
            softmax(h, slot, True)
            accumulate(h, i, slot)
            acc = acc_refs[h][...]
            out_t = acc[0:V_DIM] / acc[V_DIM:V_DIM + 1]
            o_ref[0, :, h * V_DIM:(h + 1) * V_DIM] = out_t.T.astype(BF16)

    for h in heads:
        m_refs[h][...] = jnp.full(m_refs[h].shape, -jnp.inf, F32)
        acc_refs[h][...] = jnp.zeros(acc_refs[h].shape, F32)
        scores(h, 0, 0)
        p_refs[h][ns - 1][...] = jnp.zeros(p_refs[h][ns - 1].shape, BF16)
        a_refs[h][ns - 1][...] = jnp.ones(a_refs[h][ns - 1].shape, F32)

    def group(g, carry):
        for slot in range(ns):
            step(ns * g + slot, slot)
        return carry

    lax.fori_loop(0, i // ns, group, 0)

    base = (i // ns) * ns
    for rem in range(ns):
        @pl.when(i % ns == rem)
        def _(rem=rem):
            for slot in range(rem):
                step(base + slot, slot)
            last(rem)


def _attn_call(q, k, vt, *, tq):
    B, H, S, _ = q.shape
    n_kv = vt.shape[2]
    assert vt.shape[4] == tq
    hps = ATTN_HEADS_PER_STEP
    head_scratch = ([pltpu.VMEM((1, tq), F32), pltpu.VMEM((VT_ROWS, tq), F32)]
                    + [pltpu.VMEM((tq, tq), F32)] * ATTN_SLOTS
                    + [pltpu.VMEM((tq, tq), BF16)] * ATTN_SLOTS
                    + [pltpu.VMEM((1, tq), F32)] * (2 * ATTN_SLOTS))
    assert len(head_scratch) == ATTN_SCRATCH_PER_HEAD
    return pl.pallas_call(
        functools.partial(_attn_kernel, tq=tq),
        grid=(B, H // hps, S // tq),
        in_specs=[
            pl.BlockSpec((1, hps, tq, QK_DIM), lambda b, h, i: (b, h, i, 0)),
            pl.BlockSpec((1, hps, S, QK_DIM), lambda b, h, i: (b, h, 0, 0)),
            pl.BlockSpec((1, hps, n_kv, VT_ROWS, tq), lambda b, h, i: (b, h, 0, 0, 0)),
        ],
        out_specs=pl.BlockSpec((1, tq, hps * V_DIM), lambda b, h, i: (b, i, h)),
        out_shape=jax.ShapeDtypeStruct((B, S, H * V_DIM), BF16),
        scratch_shapes=head_scratch * hps,
        compiler_params=_params("parallel", "parallel", "arbitrary"),
        name="attn",
    )(q, k, vt)


def _outproj_kernel(x_ref, a_ref, b_ref, wa_ref, wb_ref, o_ref):
    o_ref[...] = x_ref[...] + _dot(a_ref[...], wa_ref[...]) + _dot(b_ref[...], wb_ref[...])


def _outproj_call(xf, a, b, wout, e, *, tm):
    T, D = xf.shape
    wa_rows = a.shape[1]
    wb_rows = b.shape[1]
    assert wa_rows == wb_rows
    return pl.pallas_call(
        _outproj_kernel,
        grid=(T // tm,),
        in_specs=[
            pl.BlockSpec((tm, D), lambda i: (i, 0)),
            pl.BlockSpec((tm, wa_rows), lambda i: (i, 0)),
            pl.BlockSpec((tm, wb_rows), lambda i: (i, 0)),
            _resident((None, wa_rows, D), lambda i: (e, 0, 0)),
            _resident((None, wb_rows, D), lambda i: (e, 1, 0)),
        ],
        out_specs=pl.BlockSpec((tm, D), lambda i: (i, 0)),
        out_shape=jax.ShapeDtypeStruct((T, D), F32),
        compiler_params=_params("parallel"),
        name="outproj",
    )(xf, a, b, wout, wout)


def _conv_kernel(x_ref, g_ref, wb_ref, wc_ref, wu_ref, cw_ref, wo_ref, o_ref, hn_ref, carry_ref,
                 *, tm, tiles_per_seq):
    i = pl.program_id(0)
    c = pl.program_id(1)

    @pl.when(c == 0)
    def _():
        x = x_ref[...]
        hn_ref[...] = _rms(x, g_ref[...]).astype(BF16)
        o_ref[...] = x

    hn = hn_ref[...]
    gate_b = _dot(hn, wb_ref[...])
    v = _dot(hn, wc_ref[...]) * _dot(hn, wu_ref[...])
    prev = jnp.where(i % tiles_per_seq == 0, 0.0, carry_ref[c])
    carry_ref[c] = v[tm - CONV_HISTORY:tm]
    ext = jnp.concatenate([prev, v], axis=0)
    cw = cw_ref[...]
    conv = cw[CONV_WIDTH - 1:CONV_WIDTH] * v
    for back in range(1, CONV_WIDTH):
        tap = CONV_WIDTH - 1 - back
        conv = conv + cw[tap:tap + 1] * pltpu.roll(ext, back, axis=0)[CONV_HISTORY:]
    o_ref[...] += _dot((gate_b * conv).astype(BF16), wo_ref[...])


def _conv_call(xf, g, win, convw, wout, o, *, S, tm, tc):
    T, D = xf.shape
    C = wout.shape[1]
    n_c = C // tc
    kern = functools.partial(_conv_kernel, tm=tm, tiles_per_seq=S // tm)
    return pl.pallas_call(
        kern,
        grid=(T // tm, n_c),
        in_specs=[
            pl.BlockSpec((tm, D), lambda i, c: (i, 0)),
            pl.BlockSpec((1, D), lambda i, c: (0, 0)),
            pl.BlockSpec((None, D, tc), lambda i, c: (o, 0, c)),
            pl.BlockSpec((None, D, tc), lambda i, c: (o, 0, n_c + c)),
            pl.BlockSpec((None, D, tc), lambda i, c: (o, 0, 2 * n_c + c)),
            pl.BlockSpec((None, CONV_WIDTH, tc), lambda i, c: (o, 0, c)),
            pl.BlockSpec((None, tc, D), lambda i, c: (o, c, 0)),
        ],
        out_specs=pl.BlockSpec((tm, D), lambda i, c: (i, 0)),
        out_shape=jax.ShapeDtypeStruct((T, D), F32),
        scratch_shapes=[pltpu.VMEM((tm, D), BF16),
                        pltpu.VMEM((n_c, CONV_HISTORY, tc), F32)],
        compiler_params=_params("arbitrary", "arbitrary"),
        name="conv_mixer",
    )(xf, g, win, win, win, convw, wout)


def _mlp_kernel(x_ref, g_ref, wup_ref, wdn_ref, o_ref, hn_ref, act0_ref, act1_ref, *, nf, n_pairs):
    t = pl.program_id(0)
    act_refs = (act0_ref, act1_ref)
    f_up = jnp.minimum(t, n_pairs - 1) % nf
    f_down = jnp.maximum(t - 1, 0) % nf

    @pl.when(t == 0)
    def _():
        act1_ref[...] = jnp.zeros(act1_ref.shape, BF16)

    def step(slot, new_tile):
        if new_tile:
            hn_ref[...] = _rms(x_ref[...], g_ref[...]).astype(BF16)
        base = jnp.where(f_down == 0, x_ref[...], o_ref[...])
        o_ref[...] = base + _dot(act_refs[1 - slot][...], wdn_ref[...])
        up = _dot(hn_ref[...], wup_ref[...])
        act_refs[slot][...] = jnp.square(jnp.maximum(up, 0.0)).astype(BF16)

    is_new_tile = jnp.logical_and(f_up == 0, t < n_pairs)
    for slot in range(2):
        for new_tile in (False, True):
            @pl.when(jnp.logical_and(t % 2 == slot, is_new_tile == new_tile))
            def _(slot=slot, new_tile=new_tile):
                step(slot, new_tile)


def _mlp_call(xf, g, wup, wdn, layer, *, tm, tf):
    T, D = xf.shape
    F = wup.shape[2]
    nf = F // tf
    assert nf >= 2
    n_pairs = (T // tm) * nf
    up_pair = lambda t: jnp.minimum(t, n_pairs - 1)
    down_pair = lambda t: jnp.maximum(t - 1, 0)
    return pl.pallas_call(
        functools.partial(_mlp_kernel, nf=nf, n_pairs=n_pairs),
        grid=(n_pairs + 1,),
        in_specs=[
            pl.BlockSpec((tm, D), lambda t: (up_pair(t) // nf, 0)),
            pl.BlockSpec((1, D), lambda t: (0, 0)),
            pl.BlockSpec((None, D, tf), lambda t: (layer, 0, up_pair(t) % nf)),
            pl.BlockSpec((None, tf, D), lambda t: (layer, down_pair(t) % nf, 0)),
        ],
        out_specs=pl.BlockSpec((tm, D), lambda t: (down_pair(t) // nf, 0)),
        out_shape=jax.ShapeDtypeStruct((T, D), F32),
        scratch_shapes=[pltpu.VMEM((tm, D), BF16), pltpu.VMEM((tm, tf), BF16),
                        pltpu.VMEM((tm, tf), BF16)],
        compiler_params=_params("arbitrary"),
        name="mlp",
    )(xf, g, wup, wdn)


def _swap_halves_idx(n):
    return (np.arange(n) + n // 2) % n


def _prep_even_weights(even_w_in, even_w_uq, even_w_ukv, even_q_norm_g, even_k_norm_g, lora, pool_width):
    H = MLA_HEADS
    off_kr = 2 * lora
    off_pool = off_kr + ROPE_DIM
    kr = off_kr + np.arange(ROPE_DIM)
    krs = off_kr + _swap_halves_idx(ROPE_DIM)
    in_perm = np.concatenate([np.arange(off_kr), off_pool + np.arange(pool_width), kr, kr, krs, krs])
    win = even_w_in.astype(BF16)[:, :, in_perm]

    head0 = np.arange(H)[:, None] * QK_DIM
    q_nope = (head0 + np.arange(NOPE_DIM)[None, :]).reshape(-1)
    q_rope = (head0 + NOPE_DIM + np.arange(ROPE_DIM)[None, :]).reshape(-1)
    q_ropes = (head0 + NOPE_DIM + _swap_halves_idx(ROPE_DIM)[None, :]).reshape(-1)
    wuq = even_w_uq.astype(BF16)[:, :, np.concatenate([q_nope, q_rope, q_ropes])]

    kv0 = np.arange(H)[:, None] * (NOPE_DIM + V_DIM)
    k_nope = (kv0 + np.arange(NOPE_DIM)[None, :]).reshape(-1)
    v_cols = (kv0 + NOPE_DIM + np.arange(V_DIM)[None, :]).reshape(-1)
    wukv = even_w_ukv.astype(BF16)
    wuk = wukv[:, :, k_nope]
    wuvt = jnp.swapaxes(wukv[:, :, v_cols], 1, 2)

    def gain_rows(gvec):
        rope = gvec[:, NOPE_DIM:]
        ropes = rope[:, _swap_halves_idx(ROPE_DIM)]
        return [gvec[:, :NOPE_DIM], jnp.concatenate([rope, rope], -1), jnp.concatenate([ropes, ropes], -1)]

    rows = gain_rows(even_q_norm_g) + gain_rows(even_k_norm_g)
    rows += [jnp.zeros_like(rows[0])] * 2
    gains = jnp.stack(rows, axis=1).astype(F32)
    return win, wuq, wuk, wuvt, gains


def _pick_tile(n, want):
    t = min(n, want)
    while n % t:
        t //= 2
    return t


def kernel(x, positions, mix_norm_g, mlp_norm_g, w_mlp_up, w_mlp_down, even_w_in, even_q_a_norm_g,
           even_kv_a_norm_g, even_w_uq, even_w_ukv, even_q_norm_g, even_k_norm_g, even_pool_w,
           even_pool_scale, even_w_out, odd_w_in, odd_conv_w, odd_w_out):
    B, S, D = x.shape
    T = B * S
    depth = mix_norm_g.shape[0]
    lora = even_q_a_norm_g.shape[1]
    pool_width = even_pool_scale.shape[1]

    tm_mlp = _pick_tile(S, 512)
    tf_mlp = _pick_tile(w_mlp_up.shape[2], 1024)
    tm_even = _pick_tile(S, 512)
    tq = _pick_tile(S, 512)
    tc_conv = _pick_tile(odd_w_out.shape[1], 512)

    wup = w_mlp_up.astype(BF16)
    wdn = w_mlp_down.astype(BF16)
    win_e, wuq, wuk, wuvt, gains = _prep_even_weights(even_w_in, even_w_uq, even_w_ukv, even_q_norm_g,
                                                      even_k_norm_g, lora, pool_width)
    poolw = even_pool_w.astype(BF16)
    wout_e = even_w_out.astype(BF16)
    win_o = odd_w_in.astype(BF16)
    wout_o = odd_w_out.astype(BF16)

    tab = _rope_table_call(positions.reshape(T, 1), _pick_tile(T, 1024))

    xf = x.reshape(T, D)
    for layer in range(depth):
        g_mix = mix_norm_g[layer][None, :]
        if layer % 2 == 0:
            e = layer // 2
            q, k, vt, b = _even_in_call(
                xf, g_mix, win_e[e], even_q_a_norm_g[e][None, :], even_kv_a_norm_g[e][None, :],
                wuq[e], wuk[e], wuvt[e], gains[e], poolw[e], even_pool_scale[e][None, :], tab,
                B=B, S=S, tm=tm_even, tk=tq)
            a = _attn_call(q, k, vt, tq=tq).reshape(T, MLA_HEADS * V_DIM)
            xf = _outproj_call(xf, a, b, wout_e, e, tm=tm_mlp)
        else:
            o = layer // 2
            xf = _conv_call(xf, g_mix, win_o, odd_conv_w, wout_o, o, S=S, tm=tm_mlp, tc=tc_conv)
        xf = _mlp_call(xf, mlp_norm_g[layer][None, :], wup, wdn, layer, tm=tm_mlp, tf=tf_mlp)
    return xf.reshape(B, S, D)
```

```python
import functools

import jax
import jax.numpy as jnp
import numpy as np
from jax import lax
from jax.experimental import pallas as pl
from jax.experimental.pallas import tpu as pltpu

F32 = jnp.float32
BF16 = jnp.bfloat16

RMS_EPS = 1e-6
ROPE_THETA = 10000.0
MLA_HEADS = 8
NOPE_DIM = 128
ROPE_DIM = 64
QK_DIM = NOPE_DIM + ROPE_DIM
V_DIM = 128
VT_ROWS = V_DIM + 16
LOG2_E = 1.4426950408889634
POOL_WINDOWS = (2, 4, 8, 16)
POOL_HISTORY = 16
CONV_WIDTH = 3
CONV_HISTORY = 8
LANES = 128

VMEM_LIMIT_BYTES = 60 * 1024 * 1024


def _rms(xf, g):
    ms = jnp.mean(xf * xf, axis=-1, keepdims=True)
    return xf * lax.rsqrt(ms + RMS_EPS) * g


def _dot(a, b):
    return jnp.dot(a, b, preferred_element_type=F32)


def _params(*sem):
    return pltpu.CompilerParams(dimension_semantics=sem, vmem_limit_bytes=VMEM_LIMIT_BYTES)


def _resident(block_shape, index_map):
    return pl.BlockSpec(block_shape, index_map, pipeline_mode=pl.Buffered(1))


def _rope_table_kernel(pos_ref, freq_ref, tab_ref):
    ang = pos_ref[...].astype(F32) * freq_ref[...]
    lane = lax.broadcasted_iota(jnp.int32, ang.shape, 1)
    sign = jnp.where((lane % ROPE_DIM) < ROPE_DIM // 2, -1.0, 1.0)
    tab_ref[:, :LANES] = jnp.cos(ang)
    tab_ref[:, LANES:] = jnp.sin(ang) * sign


def _rope_table_call(pos_col, tm):
    T = pos_col.shape[0]
    inv_freq = 1.0 / (ROPE_THETA ** (jnp.arange(0, ROPE_DIM, 2, dtype=F32) / ROPE_DIM))
    freq = jnp.tile(inv_freq, LANES // (ROPE_DIM // 2))[None, :]
    return pl.pallas_call(
        _rope_table_kernel,
        grid=(T // tm,),
        in_specs=[pl.BlockSpec((tm, 1), lambda i: (i, 0)),
                  pl.BlockSpec((1, LANES), lambda i: (0, 0))],
        out_specs=pl.BlockSpec((tm, 2 * LANES), lambda i: (i, 0)),
        out_shape=jax.ShapeDtypeStruct((T, 2 * LANES), F32),
        compiler_params=_params("parallel"),
        name="rope_table",
    )(pos_col, freq)


def _even_in_kernel(x_ref, g_ref, win_ref, qag_ref, kvag_ref, wuq_ref, wuk_ref, wuvt_ref, gains_ref,
                    poolw_ref, pscale_ref, tab_ref,
                    q_ref, k_ref, vt_ref, b_ref, carry_ref, *, tm, lora, pool_width, scale):
    s = pl.program_id(1)
    hn = _rms(x_ref[...], g_ref[...]).astype(BF16)
    proj = _dot(hn, win_ref[...])
    off_pool = 2 * lora
    off_kr = off_pool + pool_width
    cqn = _rms(proj[:, :lora], qag_ref[...]).astype(BF16)
    ckvn = _rms(proj[:, lora:2 * lora], kvag_ref[...]).astype(BF16)
    u = proj[:, off_pool:off_kr]
    kr2 = proj[:, off_kr:off_kr + LANES]
    krs2 = proj[:, off_kr + LANES:off_kr + 2 * LANES]

    qall = _dot(cqn, wuq_ref[...])
    knall = _dot(ckvn, wuk_ref[...])
    vt_all = lax.dot_general(wuvt_ref[...], ckvn, (((1,), (1,)), ((), ())), preferred_element_type=F32)

    cos2 = tab_ref[:, :LANES]
    sin2 = tab_ref[:, LANES:]
    gains = gains_ref[...]
    qg_nope, qg_rope2, qg_ropes2 = gains[0:1], gains[1:2], gains[2:3]
    kg_nope, kg_rope2, kg_ropes2 = gains[3:4], gains[4:5], gains[5:6]
    lane = lax.broadcasted_iota(jnp.int32, (tm, LANES), 1)
    low_half = lane < ROPE_DIM
    nheads = MLA_HEADS
    nope_w = nheads * NOPE_DIM

    kr_sq = jnp.where(low_half, kr2 * kr2, 0.0)
    k_roped2 = kr2 * kg_rope2 * cos2 + krs2 * kg_ropes2 * sin2
    for h in range(nheads):
        kn = knall[:, h * NOPE_DIM:(h + 1) * NOPE_DIM]
        ss = jnp.sum(kn * kn + kr_sq, axis=-1, keepdims=True)
        r = lax.rsqrt(ss * (1.0 / QK_DIM) + RMS_EPS)
        k_ref[0, h, :, 0:NOPE_DIM] = (kn * r * kg_nope).astype(BF16)
        k_ref[0, h, :, NOPE_DIM:QK_DIM] = (k_roped2 * r)[:, :ROPE_DIM].astype(BF16)
        vt_ref[0, h, 0, 0:V_DIM, :] = vt_all[h * V_DIM:(h + 1) * V_DIM].astype(BF16)
        vt_ref[0, h, 0, V_DIM:VT_ROWS, :] = jnp.ones((VT_ROWS - V_DIM, tm), BF16)

    rope_w = nheads * ROPE_DIM
    for p in range(nheads // 2):
        qr2 = qall[:, nope_w + p * LANES:nope_w + (p + 1) * LANES]
        qrs2 = qall[:, nope_w + rope_w + p * LANES:nope_w + rope_w + (p + 1) * LANES]
        sq = qr2 * qr2
        roped = qr2 * qg_rope2 * cos2 + qrs2 * qg_ropes2 * sin2
        for e in range(2):
            h = 2 * p + e
            qn = qall[:, h * NOPE_DIM:(h + 1) * NOPE_DIM]
            mine = low_half if e == 0 else jnp.logical_not(low_half)
            ss = jnp.sum(qn * qn + jnp.where(mine, sq, 0.0), axis=-1, keepdims=True)
            r = lax.rsqrt(ss * (1.0 / QK_DIM) + RMS_EPS) * scale
            q_ref[0, h, :, 0:NOPE_DIM] = (qn * r * qg_nope).astype(BF16)
            rp = roped * r
            if e == 1:
                rp = pltpu.roll(rp, ROPE_DIM, axis=1)
            q_ref[0, h, :, NOPE_DIM:QK_DIM] = rp[:, :ROPE_DIM].astype(BF16)

    prev = jnp.where(s == 0, 0.0, carry_ref[...])
    carry_ref[...] = u[tm - POOL_HISTORY:tm]
    ext = jnp.concatenate([prev, u], axis=0)
    t_in_seq = s * tm + lax.broadcasted_iota(jnp.int32, (tm, 1), 0)
    gd = pool_width // len(POOL_WINDOWS)
    pscale = pscale_ref[...]
    for gi, w in enumerate(POOL_WINDOWS):
        e = ext[:, gi * gd:(gi + 1) * gd]
        sh = 1
        while sh < w:
            e = e + pltpu.roll(e, sh, axis=0)
            sh *= 2
        inv_cnt = 1.0 / jnp.minimum(t_in_seq + 1, w).astype(F32)
        pooled = (e[POOL_HISTORY:] * inv_cnt - u[:, gi * gd:(gi + 1) * gd]).astype(BF16)
        y = _dot(pooled, poolw_ref[gi]) * pscale[:, gi * gd:(gi + 1) * gd]
        b_ref[:, gi * gd:(gi + 1) * gd] = y.astype(BF16)


def _even_in_call(xf, g, win, qag, kvag, wuq, wuk, wuvt, gains, poolw, pscale, tab, *, B, S, tm, tk):
    T, D = xf.shape
    lora = qag.shape[1]
    pool_width = pscale.shape[1]
    n_s = S // tm
    H = MLA_HEADS
    per_kv_block = tk // tm
    tok = lambda b, s: (b * n_s + s, 0)
    const2 = lambda b, s: (0, 0)
    kern = functools.partial(_even_in_kernel, tm=tm, lora=lora, pool_width=pool_width,
                             scale=QK_DIM ** -0.5 * LOG2_E)
    return pl.pallas_call(
        kern,
        grid=(B, n_s),
        in_specs=[
            pl.BlockSpec((tm, D), tok),
            _resident((1, D), const2),
            _resident(win.shape, const2),
            _resident((1, lora), const2),
            _resident((1, lora), const2),
            _resident(wuq.shape, const2),
            _resident(wuk.shape, const2),
            _resident(wuvt.shape, const2),
            _resident(gains.shape, const2),
            _resident(poolw.shape, lambda b, s: (0, 0, 0)),
            _resident((1, pool_width), const2),
            pl.BlockSpec((tm, 2 * LANES), tok),
        ],
        out_specs=[
            pl.BlockSpec((1, H, tm, QK_DIM), lambda b, s: (b, 0, s, 0)),
            pl.BlockSpec((1, H, tm, QK_DIM), lambda b, s: (b, 0, s, 0)),
            pl.BlockSpec((1, H, 1, VT_ROWS, tm),
                         lambda b, s: (b, 0, s // per_kv_block, 0, s % per_kv_block)),
            pl.BlockSpec((tm, pool_width), tok),
        ],
        out_shape=[
            jax.ShapeDtypeStruct((B, H, S, QK_DIM), BF16),
            jax.ShapeDtypeStruct((B, H, S, QK_DIM), BF16),
            jax.ShapeDtypeStruct((B, H, S // tk, VT_ROWS, tk), BF16),
            jax.ShapeDtypeStruct((T, pool_width), BF16),
        ],
        scratch_shapes=[pltpu.VMEM((POOL_HISTORY, pool_width), F32)],
        compiler_params=_params("arbitrary", "arbitrary"),
        name="even_in",
    )(xf, g, win, qag, kvag, wuq, wuk, wuvt, gains, poolw, pscale, tab)


ATTN_HEADS_PER_STEP = 4
ATTN_SLOTS = 2
ATTN_SCRATCH_PER_HEAD = 2 + 4 * ATTN_SLOTS


def _attn_kernel(q_ref, k_ref, vt_ref, o_ref, *scratch, tq):
    i = pl.program_id(2)
    heads = range(ATTN_HEADS_PER_STEP)
    ns = ATTN_SLOTS
    per_head = [scratch[h * ATTN_SCRATCH_PER_HEAD:(h + 1) * ATTN_SCRATCH_PER_HEAD] for h in heads]
    m_refs = [r[0] for r in per_head]
    acc_refs = [r[1] for r in per_head]
    s_refs = [r[2:2 + ns] for r in per_head]
    p_refs = [r[2 + ns:2 + 2 * ns] for r in per_head]
    a_refs = [r[2 + 2 * ns:2 + 3 * ns] for r in per_head]
    bm_refs = [r[2 + 3 * ns:2 + 4 * ns] for r in per_head]

    def scores(h, j, slot):
        start = pl.multiple_of(j * tq, tq)
        kb = k_ref[0, h, pl.ds(start, tq), :]
        s = lax.dot_general(kb, q_ref[0, h], (((1,), (1,)), ((), ())), preferred_element_type=F32)
        s_refs[h][slot][...] = s
        bm_refs[h][slot][...] = jnp.max(s, axis=0, keepdims=True)

    def softmax(h, slot, masked):
        s = s_refs[h][slot][...]
        if masked:
            key = lax.broadcasted_iota(jnp.int32, s.shape, 0)
            qry = lax.broadcasted_iota(jnp.int32, s.shape, 1)
            s = jnp.where(key <= qry, s, -1e30)
            block_max = jnp.max(s, axis=0, keepdims=True)
        else:
            block_max = bm_refs[h][slot][...]
        m_prev = m_refs[h][...]
        m_new = jnp.maximum(m_prev, block_max)
        m_refs[h][...] = m_new
        a_refs[h][slot][...] = jnp.exp2(m_prev - m_new)
        p_refs[h][slot][...] = jnp.exp2(s - m_new).astype(BF16)

    def accumulate(h, j, slot):
        acc_refs[h][...] = (a_refs[h][slot][...] * acc_refs[h][...]
                            + _dot(vt_ref[0, h, j], p_refs[h][slot][...]))

    def step(j, slot):
        for h in heads:
            scores(h, j + 1, (slot + 1) % ns)
            accumulate(h, jnp.maximum(j - 1, 0), (slot - 1) % ns)
            softmax(h, slot, False)

    def last(slot):
        for h in heads:
            accumulate(h, jnp.maximum(i - 1, 0), (slot - 1) % ns)
            softmax(h, slot, True)
            accumulate(h, i, slot)
            acc = acc_refs[h][...]
            out_t = acc[0:V_DIM] / acc[V_DIM:V_DIM + 1]
            o_ref[0, :, h * V_DIM:(h + 1) * V_DIM] = out_t.T.astype(BF16)

    for h in heads:
        m_refs[h][...] = jnp.full(m_refs[h].shape, -jnp.inf, F32)
        acc_refs[h][...] = jnp.zeros(acc_refs[h].shape, F32)
        scores(h, 0, 0)
        p_refs[h][ns - 1][...] = jnp.zeros(p_refs[h][ns - 1].shape, BF16)
        a_refs[h][ns - 1][...] = jnp.ones(a_refs[h][ns - 1].shape, F32)

    def group(g, carry):
        for slot in range(ns):
            step(ns * g + slot, slot)
        return carry

    lax.fori_loop(0, i // ns, group, 0)

    base = (i // ns) * ns
    for rem in range(ns):
        @pl.when(i % ns == rem)
        def _(rem=rem):
            for slot in range(rem):
                step(base + slot, slot)
            last(rem)


def _attn_call(q, k, vt, *, tq):
    B, H, S, _ = q.shape
    n_kv = vt.shape[2]
    assert vt.shape[4] == tq
    hps = ATTN_HEADS_PER_STEP
    head_scratch = ([pltpu.VMEM((1, tq), F32), pltpu.VMEM((VT_ROWS, tq), F32)]
                    + [pltpu.VMEM((tq, tq), F32)] * ATTN_SLOTS
                    + [pltpu.VMEM((tq, tq), BF16)] * ATTN_SLOTS
                    + [pltpu.VMEM((1, tq), F32)] * (2 * ATTN_SLOTS))
    assert len(head_scratch) == ATTN_SCRATCH_PER_HEAD
    return pl.pallas_call(
        functools.partial(_attn_kernel, tq=tq),
        grid=(B, H // hps, S // tq),
        in_specs=[
            pl.BlockSpec((1, hps, tq, QK_DIM), lambda b, h, i: (b, h, i, 0)),
            pl.BlockSpec((1, hps, S, QK_DIM), lambda b, h, i: (b, h, 0, 0)),
            pl.BlockSpec((1, hps, n_kv, VT_ROWS, tq), lambda b, h, i: (b, h, 0, 0, 0)),
        ],
        out_specs=pl.BlockSpec((1, tq, hps * V_DIM), lambda b, h, i: (b, i, h)),
        out_shape=jax.ShapeDtypeStruct((B, S, H * V_DIM), BF16),
        scratch_shapes=head_scratch * hps,
        compiler_params=_params("parallel", "parallel", "arbitrary"),
        name="attn",
    )(q, k, vt)


def _outproj_kernel(x_ref, a_ref, b_ref, wa_ref, wb_ref, o_ref):
    o_ref[...] = x_ref[...] + _dot(a_ref[...], wa_ref[...]) + _dot(b_ref[...], wb_ref[...])


def _outproj_call(xf, a, b, wout, e, *, tm):
    T, D = xf.shape
    wa_rows = a.shape[1]
    wb_rows = b.shape[1]
    assert wa_rows == wb_rows
    return pl.pallas_call(
        _outproj_kernel,
        grid=(T // tm,),
        in_specs=[
            pl.BlockSpec((tm, D), lambda i: (i, 0)),
            pl.BlockSpec((tm, wa_rows), lambda i: (i, 0)),
            pl.BlockSpec((tm, wb_rows), lambda i: (i, 0)),
            _resident((None, wa_rows, D), lambda i: (e, 0, 0)),
            _resident((None, wb_rows, D), lambda i: (e, 1, 0)),
        ],
        out_specs=pl.BlockSpec((tm, D), lambda i: (i, 0)),
        out_shape=jax.ShapeDtypeStruct((T, D), F32),
        compiler_params=_params("parallel"),
        name="outproj",
    )(xf, a, b, wout, wout)


def _conv_kernel(x_ref, g_ref, wb_ref, wc_ref, wu_ref, cw_ref, wo_ref, o_ref, hn_ref, carry_ref,
                 *, tm, tiles_per_seq):
    i = pl.program_id(0)
    c = pl.program_id(1)

    @pl.when(c == 0)
    def _():
        x = x_ref[...]
        hn_ref[...] = _rms(x, g_ref[...]).astype(BF16)
        o_ref[...] = x

    hn = hn_ref[...]
    gate_b = _dot(hn, wb_ref[...])
    v = _dot(hn, wc_ref[...]) * _dot(hn, wu_ref[...])
    prev = jnp.where(i % tiles_per_seq == 0, 0.0, carry_ref[c])
    carry_ref[c] = v[tm - CONV_HISTORY:tm]
    ext = jnp.concatenate([prev, v], axis=0)
    cw = cw_ref[...]
    conv = cw[CONV_WIDTH - 1:CONV_WIDTH] * v
    for back in range(1, CONV_WIDTH):
        tap = CONV_WIDTH - 1 - back
        conv = conv + cw[tap:tap + 1] * pltpu.roll(ext, back, axis=0)[CONV_HISTORY:]
    o_ref[...] += _dot((gate_b * conv).astype(BF16), wo_ref[...])


def _conv_call(xf, g, win, convw, wout, o, *, S, tm, tc):
    T, D = xf.shape
    C = wout.shape[1]
    n_c = C // tc
    kern = functools.partial(_conv_kernel, tm=tm, tiles_per_seq=S // tm)
    return pl.pallas_call(
        kern,
        grid=(T // tm, n_c),
        in_specs=[
            pl.BlockSpec((tm, D), lambda i, c: (i, 0)),
            pl.BlockSpec((1, D), lambda i, c: (0, 0)),
            pl.BlockSpec((None, D, tc), lambda i, c: (o, 0, c)),
            pl.BlockSpec((None, D, tc), lambda i, c: (o, 0, n_c + c)),
            pl.BlockSpec((None, D, tc), lambda i, c: (o, 0, 2 * n_c + c)),
            pl.BlockSpec((None, CONV_WIDTH, tc), lambda i, c: (o, 0, c)),
            pl.BlockSpec((None, tc, D), lambda i, c: (o, c, 0)),
        ],
        out_specs=pl.BlockSpec((tm, D), lambda i, c: (i, 0)),
        out_shape=jax.ShapeDtypeStruct((T, D), F32),
        scratch_shapes=[pltpu.VMEM((tm, D), BF16),
                        pltpu.VMEM((n_c, CONV_HISTORY, tc), F32)],
        compiler_params=_params("arbitrary", "arbitrary"),
        name="conv_mixer",
    )(xf, g, win, win, win, convw, wout)


def _mlp_kernel(x_ref, g_ref, wup_ref, wdn_ref, o_ref, hn_ref):
    @pl.when(pl.program_id(1) == 0)
    def _():
        x = x_ref[...]
        hn_ref[...] = _rms(x, g_ref[...]).astype(BF16)
        o_ref[...] = x

    up = _dot(hn_ref[...], wup_ref[...])
    act = jnp.square(jnp.maximum(up, 0.0)).astype(BF16)
    o_ref[...] += _dot(act, wdn_ref[...])


def _mlp_call(xf, g, wup, wdn, layer, *, tm, tf):
    T, D = xf.shape
    F = wup.shape[2]
    return pl.pallas_call(
        _mlp_kernel,
        grid=(T // tm, F // tf),
        in_specs=[
            pl.BlockSpec((tm, D), lambda i, f: (i, 0)),
            pl.BlockSpec((1, D), lambda i, f: (0, 0)),
            pl.BlockSpec((None, D, tf), lambda i, f: (layer, 0, f)),
            pl.BlockSpec((None, tf, D), lambda i, f: (layer, f, 0)),
        ],
        out_specs=pl.BlockSpec((tm, D), lambda i, f: (i, 0)),
        out_shape=jax.ShapeDtypeStruct((T, D), F32),
        scratch_shapes=[pltpu.VMEM((tm, D), BF16)],
        compiler_params=_params("parallel", "arbitrary"),
        name="mlp",
    )(xf, g, wup, wdn)


def _swap_halves_idx(n):
    return (np.arange(n) + n // 2) % n


def _prep_even_weights(even_w_in, even_w_uq, even_w_ukv, even_q_norm_g, even_k_norm_g, lora, pool_width):
    H = MLA_HEADS
    off_kr = 2 * lora
    off_pool = off_kr + ROPE_DIM
    kr = off_kr + np.arange(ROPE_DIM)
    krs = off_kr + _swap_halves_idx(ROPE_DIM)
    in_perm = np.concatenate([np.arange(off_kr), off_pool + np.arange(pool_width), kr, kr, krs, krs])
    win = even_w_in.astype(BF16)[:, :, in_perm]

    head0 = np.arange(H)[:, None] * QK_DIM
    q_nope = (head0 + np.arange(NOPE_DIM)[None, :]).reshape(-1)
    q_rope = (head0 + NOPE_DIM + np.arange(ROPE_DIM)[None, :]).reshape(-1)
    q_ropes = (head0 + NOPE_DIM + _swap_halves_idx(ROPE_DIM)[None, :]).reshape(-1)
    wuq = even_w_uq.astype(BF16)[:, :, np.concatenate([q_nope, q_rope, q_ropes])]

    kv0 = np.arange(H)[:, None] * (NOPE_DIM + V_DIM)
    k_nope = (kv0 + np.arange(NOPE_DIM)[None, :]).reshape(-1)
    v_cols = (kv0 + NOPE_DIM + np.arange(V_DIM)[None, :]).reshape(-1)
    wukv = even_w_ukv.astype(BF16)
    wuk = wukv[:, :, k_nope]
    wuvt = jnp.swapaxes(wukv[:, :, v_cols], 1, 2)

    def gain_rows(gvec):
        rope = gvec[:, NOPE_DIM:]
        ropes = rope[:, _swap_halves_idx(ROPE_DIM)]
        return [gvec[:, :NOPE_DIM], jnp.concatenate([rope, rope], -1), jnp.concatenate([ropes, ropes], -1)]

    rows = gain_rows(even_q_norm_g) + gain_rows(even_k_norm_g)
    rows += [jnp.zeros_like(rows[0])] * 2
    gains = jnp.stack(rows, axis=1).astype(F32)
    return win, wuq, wuk, wuvt, gains


def _pick_tile(n, want):
    t = min(n, want)
    while n % t:
        t //= 2
    return t


def kernel(x, positions, mix_norm_g, mlp_norm_g, w_mlp_up, w_mlp_down, even_w_in, even_q_a_norm_g,
           even_kv_a_norm_g, even_w_uq, even_w_ukv, even_q_norm_g, even_k_norm_g, even_pool_w,
           even_pool_scale, even_w_out, odd_w_in, odd_conv_w, odd_w_out):
    B, S, D = x.shape
    T = B * S
    depth = mix_norm_g.shape[0]
    lora = even_q_a_norm_g.shape[1]
    pool_width = even_pool_scale.shape[1]

    tm_mlp = _pick_tile(S, 512)
    tf_mlp = _pick_tile(w_mlp_up.shape[2], 2048)
    tm_even = _pick_tile(S, 512)
    tq = _pick_tile(S, 512)
    tc_conv = _pick_tile(odd_w_out.shape[1], 1024)

    wup = w_mlp_up.astype(BF16)
    wdn = w_mlp_down.astype(BF16)
    win_e, wuq, wuk, wuvt, gains = _prep_even_weights(even_w_in, even_w_uq, even_w_ukv, even_q_norm_g,
                                                      even_k_norm_g, lora, pool_width)
    poolw = even_pool_w.astype(BF16)
    wout_e = even_w_out.astype(BF16)
    win_o = odd_w_in.astype(BF16)
    wout_o = odd_w_out.astype(BF16)

    tab = _rope_table_call(positions.reshape(T, 1), _pick_tile(T, 1024))

    xf = x.reshape(T, D)
    for layer in range(depth):
        g_mix = mix_norm_g[layer][None, :]
        if layer % 2 == 0:
            e = layer // 2
            q, k, vt, b = _even_in_call(
                xf, g_mix, win_e[e], even_q_a_norm_g[e][None, :], even_kv_a_norm_g[e][None, :],
                wuq[e], wuk[e], wuvt[e], gains[e], poolw[e], even_pool_scale[e][None, :], tab,
                B=B, S=S, tm=tm_even, tk=tq)
            a = _attn_call(q, k, vt, tq=tq).reshape(T, MLA_HEADS * V_DIM)
            xf = _outproj_call(xf, a, b, wout_e, e, tm=tm_mlp)
        else:
            o = layer // 2
            xf = _conv_call(xf, g_mix, win_o, odd_conv_w, wout_o, o, S=S, tm=tm_mlp, tc=tc_conv)
        xf = _mlp_call(xf, mlp_norm_g[layer][None, :], wup, wdn, layer, tm=tm_mlp, tf=tf_mlp)
    return xf.reshape(B, S, D)
```

```python
import functools

import jax
import jax.numpy as jnp
import numpy as np
from jax import lax
from jax.experimental import pallas as pl
from jax.experimental.pallas import tpu as pltpu

F32 = jnp.float32
BF16 = jnp.bfloat16

RMS_EPS = 1e-6
ROPE_THETA = 10000.0
MLA_HEADS = 8
NOPE_DIM = 128
ROPE_DIM = 64
QK_DIM = NOPE_DIM + ROPE_DIM
V_DIM = 128
VT_ROWS = V_DIM + 16
LOG2_E = 1.4426950408889634
POOL_WINDOWS = (2, 4, 8, 16)
POOL_HISTORY = 16
CONV_WIDTH = 3
CONV_HISTORY = 8
LANES = 128
MXU_TILE = 256

VMEM_LIMIT_BYTES = 60 * 1024 * 1024


def _rms(xf, g):
    ms = jnp.mean(xf * xf, axis=-1, keepdims=True)
    return xf * lax.rsqrt(ms + RMS_EPS) * g


def _dot(a, b):
    return jnp.dot(a, b, preferred_element_type=F32)


def _params(*sem):
    return pltpu.CompilerParams(dimension_semantics=sem, vmem_limit_bytes=VMEM_LIMIT_BYTES)


def _resident(block_shape, index_map):
    return pl.BlockSpec(block_shape, index_map, pipeline_mode=pl.Buffered(1))


def _rope_table_kernel(pos_ref, freq_ref, tab_ref):
    ang = pos_ref[...].astype(F32) * freq_ref[...]
    lane = lax.broadcasted_iota(jnp.int32, ang.shape, 1)
    sign = jnp.where((lane % ROPE_DIM) < ROPE_DIM // 2, -1.0, 1.0)
    tab_ref[:, :LANES] = jnp.cos(ang)
    tab_ref[:, LANES:] = jnp.sin(ang) * sign


def _rope_table_call(pos_col, tm):
    T = pos_col.shape[0]
    inv_freq = 1.0 / (ROPE_THETA ** (jnp.arange(0, ROPE_DIM, 2, dtype=F32) / ROPE_DIM))
    freq = jnp.tile(inv_freq, LANES // (ROPE_DIM // 2))[None, :]
    return pl.pallas_call(
        _rope_table_kernel,
        grid=(T // tm,),
        in_specs=[pl.BlockSpec((tm, 1), lambda i: (i, 0)),
                  pl.BlockSpec((1, LANES), lambda i: (0, 0))],
        out_specs=pl.BlockSpec((tm, 2 * LANES), lambda i: (i, 0)),
        out_shape=jax.ShapeDtypeStruct((T, 2 * LANES), F32),
        compiler_params=_params("parallel"),
        name="rope_table",
    )(pos_col, freq)


def _even_in_kernel(x_ref, g_ref, win_ref, qag_ref, kvag_ref, wuq_ref, wuk_ref, wuvt_ref, gains_ref,
                    poolw_ref, pscale_ref, tab_ref,
                    q_ref, k_ref, vt_ref, b_ref, carry_ref, *, tm, lora, pool_width, scale):
    s = pl.program_id(1)
    hn = _rms(x_ref[...], g_ref[...]).astype(BF16)
    off_pool = 2 * lora
    off_kr = off_pool + pool_width
    ckvn = _rms(_dot(hn, win_ref[:, lora:2 * lora]), kvag_ref[...]).astype(BF16)
    kr_both = _dot(hn, win_ref[:, off_kr:off_kr + 2 * LANES])
    kr2 = kr_both[:, :LANES]
    krs2 = kr_both[:, LANES:]
    cqn = _rms(_dot(hn, win_ref[:, :lora]), qag_ref[...]).astype(BF16)
    knall = _dot(ckvn, wuk_ref[...])

    cos2 = tab_ref[:, :LANES]
    sin2 = tab_ref[:, LANES:]
    gains = gains_ref[...]
    qg_nope, qg_rope2, qg_ropes2 = gains[0:1], gains[1:2], gains[2:3]
    kg_nope, kg_rope2, kg_ropes2 = gains[3:4], gains[4:5], gains[5:6]
    lane = lax.broadcasted_iota(jnp.int32, (tm, LANES), 1)
    low_half = lane < ROPE_DIM
    nheads = MLA_HEADS

    kr_sq = jnp.where(low_half, kr2 * kr2, 0.0)
    k_roped2 = kr2 * kg_rope2 * cos2 + krs2 * kg_ropes2 * sin2
    for h in range(nheads):
        kn = knall[:, h * NOPE_DIM:(h + 1) * NOPE_DIM]
        ss = jnp.sum(kn * kn + kr_sq, axis=-1, keepdims=True)
        r = lax.rsqrt(ss * (1.0 / QK_DIM) + RMS_EPS)
        k_ref[0, h, :, 0:NOPE_DIM] = (kn * r * kg_nope).astype(BF16)
        k_ref[0, h, :, NOPE_DIM:QK_DIM] = (k_roped2 * r)[:, :ROPE_DIM].astype(BF16)

    t_in_seq = s * tm + lax.broadcasted_iota(jnp.int32, (tm, 1), 0)
    gd = pool_width // len(POOL_WINDOWS)
    pscale = pscale_ref[...]
    pair_w = 2 * NOPE_DIM + 2 * LANES

    def pool_project(g0, g1):
        return _dot(hn, win_ref[:, off_pool + g0 * gd:off_pool + g1 * gd])

    def pool_windows(u_all, g0, g1):
        pooled = []
        for gi in range(g0, g1):
            w = POOL_WINDOWS[gi]
            cols = slice(gi * gd, (gi + 1) * gd)
            u = u_all[:, (gi - g0) * gd:(gi - g0 + 1) * gd]
            prev = jnp.where(s == 0, 0.0, carry_ref[:, cols])
            carry_ref[:, cols] = u[tm - POOL_HISTORY:tm]
            e = jnp.concatenate([prev, u], axis=0)
            sh = 1
            while sh < w:
                e = e + pltpu.roll(e, sh, axis=0)
                sh *= 2
            inv_cnt = 1.0 / jnp.minimum(t_in_seq + 1, w).astype(F32)
            pooled.append((e[POOL_HISTORY:] * inv_cnt - u).astype(BF16))
        return pooled

    def pool_mix(pooled, g0):
        for k, pg in enumerate(pooled):
            cols = slice((g0 + k) * gd, (g0 + k + 1) * gd)
            b_ref[:, cols] = (_dot(pg, poolw_ref[g0 + k]) * pscale[:, cols]).astype(BF16)

    def query_project(p):
        return _dot(cqn, wuq_ref[:, p * pair_w:(p + 1) * pair_w])

    def query_finish(qp, p):
        qr2 = qp[:, 2 * NOPE_DIM:2 * NOPE_DIM + LANES]
        qrs2 = qp[:, 2 * NOPE_DIM + LANES:]
        sq = qr2 * qr2
        roped = qr2 * qg_rope2 * cos2 + qrs2 * qg_ropes2 * sin2
        for e in range(2):
            h = 2 * p + e
            qn = qp[:, e * NOPE_DIM:(e + 1) * NOPE_DIM]
            mine = low_half if e == 0 else jnp.logical_not(low_half)
            ss = jnp.sum(qn * qn + jnp.where(mine, sq, 0.0), axis=-1, keepdims=True)
            r = lax.rsqrt(ss * (1.0 / QK_DIM) + RMS_EPS) * scale
            q_ref[0, h, :, 0:NOPE_DIM] = (qn * r * qg_nope).astype(BF16)
            rp = roped * r
            if e == 1:
                rp = pltpu.roll(rp, ROPE_DIM, axis=1)
            q_ref[0, h, :, NOPE_DIM:QK_DIM] = rp[:, :ROPE_DIM].astype(BF16)

    assert len(POOL_WINDOWS) == 4 and nheads == 8
    u01 = pool_project(0, 2)
    qp0 = query_project(0)
    qp1 = query_project(1)
    pooled01 = pool_windows(u01, 0, 2)
    u23 = pool_project(2, 4)
    pool_mix(pooled01, 0)
    query_finish(qp0, 0)
    query_finish(qp1, 1)
    qp2 = query_project(2)
    qp3 = query_project(3)
    pooled23 = pool_windows(u23, 2, 4)
    pool_mix(pooled23, 2)
    vt_all = lax.dot_general(wuvt_ref[...], ckvn, (((1,), (1,)), ((), ())), preferred_element_type=F32)
    query_finish(qp2, 2)
    query_finish(qp3, 3)
    for h in range(nheads):
        vt_ref[0, h, 0, 0:V_DIM, :] = vt_all[h * V_DIM:(h + 1) * V_DIM].astype(BF16)
        vt_ref[0, h, 0, V_DIM:VT_ROWS, :] = jnp.ones((VT_ROWS - V_DIM, tm), BF16)


def _even_in_call(xf, g, win, qag, kvag, wuq, wuk, wuvt, gains, poolw, pscale, tab, *, B, S, tm, tk):
    T, D = xf.shape
    lora = qag.shape[1]
    pool_width = pscale.shape[1]
    n_s = S // tm
    H = MLA_HEADS
    per_kv_block = tk // tm
    tok = lambda b, s: (b * n_s + s, 0)
    const2 = lambda b, s: (0, 0)
    kern = functools.partial(_even_in_kernel, tm=tm, lora=lora, pool_width=pool_width,
                             scale=QK_DIM ** -0.5 * LOG2_E)
    return pl.pallas_call(
        kern,
        grid=(B, n_s),
        in_specs=[
            pl.BlockSpec((tm, D), tok),
            _resident((1, D), const2),
            _resident(win.shape, const2),
            _resident((1, lora), const2),
            _resident((1, lora), const2),
            _resident(wuq.shape, const2),
            _resident(wuk.shape, const2),
            _resident(wuvt.shape, const2),
            _resident(gains.shape, const2),
            _resident(poolw.shape, lambda b, s: (0, 0, 0)),
            _resident((1, pool_width), const2),
            pl.BlockSpec((tm, 2 * LANES), tok),
        ],
        out_specs=[
            pl.BlockSpec((1, H, tm, QK_DIM), lambda b, s: (b, 0, s, 0)),
            pl.BlockSpec((1, H, tm, QK_DIM), lambda b, s: (b, 0, s, 0)),
            pl.BlockSpec((1, H, 1, VT_ROWS, tm),
                         lambda b, s: (b, 0, s // per_kv_block, 0, s % per_kv_block)),
            pl.BlockSpec((tm, pool_width), tok),
        ],
        out_shape=[
            jax.ShapeDtypeStruct((B, H, S, QK_DIM), BF16),
            jax.ShapeDtypeStruct((B, H, S, QK_DIM), BF16),
            jax.ShapeDtypeStruct((B, H, S // tk, VT_ROWS, tk), BF16),
            jax.ShapeDtypeStruct((T, pool_width), BF16),
        ],
        scratch_shapes=[pltpu.VMEM((POOL_HISTORY, pool_width), F32)],
        compiler_params=_params("arbitrary", "arbitrary"),
        name="even_in",
    )(xf, g, win, qag, kvag, wuq, wuk, wuvt, gains, poolw, pscale, tab)


ATTN_HEADS_PER_STEP = 4
ATTN_SLOTS = 2
ATTN_SCRATCH_PER_HEAD = 2 + 4 * ATTN_SLOTS


def _attn_kernel(q_ref, k_ref, vt_ref, o_ref, *scratch, tq):
    i = pl.program_id(2)
    heads = range(ATTN_HEADS_PER_STEP)
    ns = ATTN_SLOTS
    per_head = [scratch[h * ATTN_SCRATCH_PER_HEAD:(h + 1) * ATTN_SCRATCH_PER_HEAD] for h in heads]
    m_refs = [r[0] for r in per_head]
    acc_refs = [r[1] for r in per_head]
    s_refs = [r[2:2 + ns] for r in per_head]
    p_refs = [r[2 + ns:2 + 2 * ns] for r in per_head]
    a_refs = [r[2 + 2 * ns:2 + 3 * ns] for r in per_head]
    bm_refs = [r[2 + 3 * ns:2 + 4 * ns] for r in per_head]

    def scores(h, j, slot):
        start = pl.multiple_of(j * tq, tq)
        kb = k_ref[0, h, pl.ds(start, tq), :]
        s = lax.dot_general(kb, q_ref[0, h], (((1,), (1,)), ((), ())), preferred_element_type=F32)
        s_refs[h][slot][...] = s
        bm_refs[h][slot][...] = jnp.max(s, axis=0, keepdims=True)

    def softmax(h, slot, masked):
        s = s_refs[h][slot][...]
        if masked:
            key = lax.broadcasted_iota(jnp.int32, s.shape, 0)
            qry = lax.broadcasted_iota(jnp.int32, s.shape, 1)
            s = jnp.where(key <= qry, s, -1e30)
            block_max = jnp.max(s, axis=0, keepdims=True)
        else:
            block_max = bm_refs[h][slot][...]
        m_prev = m_refs[h][...]
        m_new = jnp.maximum(m_prev, block_max)
        m_refs[h][...] = m_new
        a_refs[h][slot][...] = jnp.exp2(m_prev - m_new)
        p_refs[h][slot][...] = jnp.exp2(s - m_new).astype(BF16)

    def accumulate(h, j, slot):
        acc_refs[h][...] = (a_refs[h][slot][...] * acc_refs[h][...]
                            + _dot(vt_ref[0, h, j], p_refs[h][slot][...]))

    def step(j, slot):
        for h in heads:
            scores(h, j + 1, (slot + 1) % ns)
            accumulate(h, jnp.maximum(j - 1, 0), (slot - 1) % ns)
            softmax(h, slot, False)

    def last(slot):
        for h in heads:
            accumulate(h, jnp.maximum(i - 1, 0), (slot - 1) % ns)
            softmax(h, slot, True)
            accumulate(h, i, slot)
            acc = acc_refs[h][...]
            out_t = acc[0:V_DIM] / acc[V_DIM:V_DIM + 1]
            o_ref[0, :, h * V_DIM:(h + 1) * V_DIM] = out_t.T.astype(BF16)

    for h in heads:
        m_refs[h][...] = jnp.full(m_refs[h].shape, -jnp.inf, F32)
        acc_refs[h][...] = jnp.zeros(acc_refs[h].shape, F32)
        scores(h, 0, 0)
        p_refs[h][ns - 1][...] = jnp.zeros(p_refs[h][ns - 1].shape, BF16)
        a_refs[h][ns - 1][...] = jnp.ones(a_refs[h][ns - 1].shape, F32)

    def group(g, carry):
        for slot in range(ns):
            step(ns * g + slot, slot)
        return carry

    lax.fori_loop(0, i // ns, group, 0)

    base = (i // ns) * ns
    for rem in range(ns):
        @pl.when(i % ns == rem)
        def _(rem=rem):
            for slot in range(rem):
                step(base + slot, slot)
            last(rem)


def _attn_call(q, k, vt, *, tq):
    B, H, S, _ = q.shape
    n_kv = vt.shape[2]
    assert vt.shape[4] == tq
    hps = ATTN_HEADS_PER_STEP
    head_scratch = ([pltpu.VMEM((1, tq), F32), pltpu.VMEM((VT_ROWS, tq), F32)]
                    + [pltpu.VMEM((tq, tq), F32)] * ATTN_SLOTS
                    + [pltpu.VMEM((tq, tq), BF16)] * ATTN_SLOTS
                    + [pltpu.VMEM((1, tq), F32)] * (2 * ATTN_SLOTS))
    assert len(head_scratch) == ATTN_SCRATCH_PER_HEAD
    return pl.pallas_call(
        functools.partial(_attn_kernel, tq=tq),
        grid=(B, H // hps, S // tq),
        in_specs=[
            pl.BlockSpec((1, hps, tq, QK_DIM), lambda b, h, i: (b, h, i, 0)),
            pl.BlockSpec((1, hps, S, QK_DIM), lambda b, h, i: (b, h, 0, 0)),
            pl.BlockSpec((1, hps, n_kv, VT_ROWS, tq), lambda b, h, i: (b, h, 0, 0, 0)),
        ],
        out_specs=pl.BlockSpec((1, tq, hps * V_DIM), lambda b, h, i: (b, i, h)),
        out_shape=jax.ShapeDtypeStruct((B, S, H * V_DIM), BF16),
        scratch_shapes=head_scratch * hps,
        compiler_params=_params("parallel", "parallel", "arbitrary"),
        name="attn",
    )(q, k, vt)


def _outproj_kernel(x_ref, a_ref, b_ref, wa_ref, wb_ref, o_ref):
    o_ref[...] = x_ref[...] + _dot(a_ref[...], wa_ref[...]) + _dot(b_ref[...], wb_ref[...])


def _outproj_call(xf, a, b, wout, e, *, tm):
    T, D = xf.shape
    wa_rows = a.shape[1]
    wb_rows = b.shape[1]
    assert wa_rows == wb_rows
    return pl.pallas_call(
        _outproj_kernel,
        grid=(T // tm,),
        in_specs=[
            pl.BlockSpec((tm, D), lambda i: (i, 0)),
            pl.BlockSpec((tm, wa_rows), lambda i: (i, 0)),
            pl.BlockSpec((tm, wb_rows), lambda i: (i, 0)),
            _resident((None, wa_rows, D), lambda i: (e, 0, 0)),
            _resident((None, wb_rows, D), lambda i: (e, 1, 0)),
        ],
        out_specs=pl.BlockSpec((tm, D), lambda i: (i, 0)),
        out_shape=jax.ShapeDtypeStruct((T, D), F32),
        compiler_params=_params("parallel"),
        name="outproj",
    )(xf, a, b, wout, wout)


def _conv_kernel(x_ref, g_ref, wb_ref, wc_ref, wu_ref, cw_ref, wo_ref, o_ref, hn_ref, carry_ref,
                 *, tm, tiles_per_seq):
    i = pl.program_id(0)
    c = pl.program_id(1)

    @pl.when(c == 0)
    def _():
        x = x_ref[...]
        hn_ref[...] = _rms(x, g_ref[...]).astype(BF16)
        o_ref[...] = x

    hn = hn_ref[...]
    cw = cw_ref[...]
    first_tile = i % tiles_per_seq == 0
    tc = wo_ref.shape[0]
    gated = []
    for lo in range(0, tc, MXU_TILE):
        cols = slice(lo, lo + MXU_TILE)
        gate_b = _dot(hn, wb_ref[:, cols])
        v = _dot(hn, wc_ref[:, cols]) * _dot(hn, wu_ref[:, cols])
        prev = jnp.where(first_tile, 0.0, carry_ref[c, :, cols])
        carry_ref[c, :, cols] = v[tm - CONV_HISTORY:tm]
        ext = jnp.concatenate([prev, v], axis=0)
        conv = cw[CONV_WIDTH - 1:CONV_WIDTH, cols] * v
        for back in range(1, CONV_WIDTH):
            tap = CONV_WIDTH - 1 - back
            conv = conv + cw[tap:tap + 1, cols] * pltpu.roll(ext, back, axis=0)[CONV_HISTORY:]
        gated.append((gate_b * conv).astype(BF16))
    o_ref[...] += _dot(jnp.concatenate(gated, axis=1), wo_ref[...])


def _conv_call(xf, g, win, convw, wout, o, *, S, tm, tc):
    T, D = xf.shape
    C = wout.shape[1]
    n_c = C // tc
    kern = functools.partial(_conv_kernel, tm=tm, tiles_per_seq=S // tm)
    return pl.pallas_call(
        kern,
        grid=(T // tm, n_c),
        in_specs=[
            pl.BlockSpec((tm, D), lambda i, c: (i, 0)),
            pl.BlockSpec((1, D), lambda i, c: (0, 0)),
            pl.BlockSpec((None, D, tc), lambda i, c: (o, 0, c)),
            pl.BlockSpec((None, D, tc), lambda i, c: (o, 0, n_c + c)),
            pl.BlockSpec((None, D, tc), lambda i, c: (o, 0, 2 * n_c + c)),
            pl.BlockSpec((None, CONV_WIDTH, tc), lambda i, c: (o, 0, c)),
            pl.BlockSpec((None, tc, D), lambda i, c: (o, c, 0)),
        ],
        out_specs=pl.BlockSpec((tm, D), lambda i, c: (i, 0)),
        out_shape=jax.ShapeDtypeStruct((T, D), F32),
        scratch_shapes=[pltpu.VMEM((tm, D), BF16),
                        pltpu.VMEM((n_c, CONV_HISTORY, tc), F32)],
        compiler_params=_params("arbitrary", "arbitrary"),
        name="conv_mixer",
    )(xf, g, win, win, win, convw, wout)


def _mlp_kernel(x_ref, g_ref, wup_ref, wdn_ref, o_ref, hn_ref):
    @pl.when(pl.program_id(1) == 0)
    def _():
        x = x_ref[...]
        hn_ref[...] = _rms(x, g_ref[...]).astype(BF16)
        o_ref[...] = x

    up = _dot(hn_ref[...], wup_ref[...])
    act = jnp.square(jnp.maximum(up, 0.0)).astype(BF16)
    o_ref[...] += _dot(act, wdn_ref[...])


def _mlp_call(xf, g, wup, wdn, layer, *, tm, tf):
    T, D = xf.shape
    F = wup.shape[2]
    return pl.pallas_call(
        _mlp_kernel,
        grid=(T // tm, F // tf),
        in_specs=[
            pl.BlockSpec((tm, D), lambda i, f: (i, 0)),
            pl.BlockSpec((1, D), lambda i, f: (0, 0)),
            pl.BlockSpec((None, D, tf), lambda i, f: (layer, 0, f)),
            pl.BlockSpec((None, tf, D), lambda i, f: (layer, f, 0)),
        ],
        out_specs=pl.BlockSpec((tm, D), lambda i, f: (i, 0)),
        out_shape=jax.ShapeDtypeStruct((T, D), F32),
        scratch_shapes=[pltpu.VMEM((tm, D), BF16)],
        compiler_params=_params("parallel", "arbitrary"),
        name="mlp",
    )(xf, g, wup, wdn)


def _swap_halves_idx(n):
    return (np.arange(n) + n // 2) % n


def _prep_even_weights(even_w_in, even_w_uq, even_w_ukv, even_q_norm_g, even_k_norm_g, lora, pool_width):
    H = MLA_HEADS
    off_kr = 2 * lora
    off_pool = off_kr + ROPE_DIM
    kr = off_kr + np.arange(ROPE_DIM)
    krs = off_kr + _swap_halves_idx(ROPE_DIM)
    in_perm = np.concatenate([np.arange(off_kr), off_pool + np.arange(pool_width), kr, kr, krs, krs])
    win = even_w_in.astype(BF16)[:, :, in_perm]

    q_cols = []
    for pair in range(H // 2):
        heads = (2 * pair, 2 * pair + 1)
        q_cols += [h * QK_DIM + np.arange(NOPE_DIM) for h in heads]
        q_cols += [h * QK_DIM + NOPE_DIM + np.arange(ROPE_DIM) for h in heads]
        q_cols += [h * QK_DIM + NOPE_DIM + _swap_halves_idx(ROPE_DIM) for h in heads]
    wuq = even_w_uq.astype(BF16)[:, :, np.concatenate(q_cols)]

    kv0 = np.arange(H)[:, None] * (NOPE_DIM + V_DIM)
    k_nope = (kv0 + np.arange(NOPE_DIM)[None, :]).reshape(-1)
    v_cols = (kv0 + NOPE_DIM + np.arange(V_DIM)[None, :]).reshape(-1)
    wukv = even_w_ukv.astype(BF16)
    wuk = wukv[:, :, k_nope]
    wuvt = jnp.swapaxes(wukv[:, :, v_cols], 1, 2)

    def gain_rows(gvec):
        rope = gvec[:, NOPE_DIM:]
        ropes = rope[:, _swap_halves_idx(ROPE_DIM)]
        return [gvec[:, :NOPE_DIM], jnp.concatenate([rope, rope], -1), jnp.concatenate([ropes, ropes], -1)]

    rows = gain_rows(even_q_norm_g) + gain_rows(even_k_norm_g)
    rows += [jnp.zeros_like(rows[0])] * 2
    gains = jnp.stack(rows, axis=1).astype(F32)
    return win, wuq, wuk, wuvt, gains


def _pick_tile(n, want):
    t = min(n, want)
    while n % t:
        t //= 2
    return t


def kernel(x, positions, mix_norm_g, mlp_norm_g, w_mlp_up, w_mlp_down, even_w_in, even_q_a_norm_g,
           even_kv_a_norm_g, even_w_uq, even_w_ukv, even_q_norm_g, even_k_norm_g, even_pool_w,
           even_pool_scale, even_w_out, odd_w_in, odd_conv_w, odd_w_out):
    B, S, D = x.shape
    T = B * S
    depth = mix_norm_g.shape[0]
    lora = even_q_a_norm_g.shape[1]
    pool_width = even_pool_scale.shape[1]

    tm_mlp = _pick_tile(S, 512)
    tf_mlp = _pick_tile(w_mlp_up.shape[2], 2048)
    tm_even = _pick_tile(S, 512)
    tq = _pick_tile(S, 512)
    tc_conv = _pick_tile(odd_w_out.shape[1], 1024)

    wup = w_mlp_up.astype(BF16)
    wdn = w_mlp_down.astype(BF16)
    win_e, wuq, wuk, wuvt, gains = _prep_even_weights(even_w_in, even_w_uq, even_w_ukv, even_q_norm_g,
                                                      even_k_norm_g, lora, pool_width)
    poolw = even_pool_w.astype(BF16)
    wout_e = even_w_out.astype(BF16)
    win_o = odd_w_in.astype(BF16)
    wout_o = odd_w_out.astype(BF16)

    tab = _rope_table_call(positions.reshape(T, 1), _pick_tile(T, 1024))

    xf = x.reshape(T, D)
    for layer in range(depth):
        g_mix = mix_norm_g[layer][None, :]
        if layer % 2 == 0:
            e = layer // 2
            q, k, vt, b = _even_in_call(
                xf, g_mix, win_e[e], even_q_a_norm_g[e][None, :], even_kv_a_norm_g[e][None, :],
                wuq[e], wuk[e], wuvt[e], gains[e], poolw[e], even_pool_scale[e][None, :], tab,
                B=B, S=S, tm=tm_even, tk=tq)
            a = _attn_call(q, k, vt, tq=tq).reshape(T, MLA_HEADS * V_DIM)
            xf = _outproj_call(xf, a, b, wout_e, e, tm=tm_mlp)
        else:
            o = layer // 2
            xf = _conv_call(xf, g_mix, win_o, odd_conv_w, wout_o, o, S=S, tm=tm_mlp, tc=tc_conv)
        xf = _mlp_call(xf, mlp_norm_g[layer][None, :], wup, wdn, layer, tm=tm_mlp, tf=tf_mlp)
    return xf.reshape(B, S, D)
```

```python
import functools

import jax
import jax.numpy as jnp
import numpy as np
from jax import lax
from jax.experimental import pallas as pl
from jax.experimental.pallas import tpu as pltpu

F32 = jnp.float32
BF16 = jnp.bfloat16

RMS_EPS = 1e-6
ROPE_THETA = 10000.0
MLA_HEADS = 8
NOPE_DIM = 128
ROPE_DIM = 64
QK_DIM = NOPE_DIM + ROPE_DIM
V_DIM = 128
VT_ROWS = V_DIM + 16
LOG2_E = 1.4426950408889634
POOL_WINDOWS = (2, 4, 8, 16)
POOL_HISTORY = 16
CONV_WIDTH = 3
CONV_HISTORY = 8
LANES = 128
MXU_TILE = 256

VMEM_LIMIT_BYTES = 60 * 1024 * 1024


def _rms(xf, g):
    ms = jnp.mean(xf * xf, axis=-1, keepdims=True)
    return xf * lax.rsqrt(ms + RMS_EPS) * g


def _dot(a, b):
    return jnp.dot(a, b, preferred_element_type=F32)


def _params(*sem):
    return pltpu.CompilerParams(dimension_semantics=sem, vmem_limit_bytes=VMEM_LIMIT_BYTES)


def _resident(block_shape, index_map):
    return pl.BlockSpec(block_shape, index_map, pipeline_mode=pl.Buffered(1))


def _rope_table_kernel(pos_ref, freq_ref, tab_ref):
    ang = pos_ref[...].astype(F32) * freq_ref[...]
    lane = lax.broadcasted_iota(jnp.int32, ang.shape, 1)
    sign = jnp.where((lane % ROPE_DIM) < ROPE_DIM // 2, -1.0, 1.0)
    tab_ref[:, :LANES] = jnp.cos(ang)
    tab_ref[:, LANES:] = jnp.sin(ang) * sign


def _rope_table_call(pos_col, tm):
    T = pos_col.shape[0]
    inv_freq = 1.0 / (ROPE_THETA ** (jnp.arange(0, ROPE_DIM, 2, dtype=F32) / ROPE_DIM))
    freq = jnp.tile(inv_freq, LANES // (ROPE_DIM // 2))[None, :]
    return pl.pallas_call(
        _rope_table_kernel,
        grid=(T // tm,),
        in_specs=[pl.BlockSpec((tm, 1), lambda i: (i, 0)),
                  pl.BlockSpec((1, LANES), lambda i: (0, 0))],
        out_specs=pl.BlockSpec((tm, 2 * LANES), lambda i: (i, 0)),
        out_shape=jax.ShapeDtypeStruct((T, 2 * LANES), F32),
        compiler_params=_params("parallel"),
        name="rope_table",
    )(pos_col, freq)


def _even_in_kernel(x_ref, g_ref, win_ref, qag_ref, kvag_ref, wuq_ref, wuk_ref, wuvt_ref, gains_ref,
                    poolw_ref, pscale_ref, tab_ref,
                    q_ref, k_ref, vt_ref, b_ref, carry_ref, *, tm, lora, pool_width, scale):
    s = pl.program_id(1)
    hn = _rms(x_ref[...], g_ref[...]).astype(BF16)
    off_pool = 2 * lora
    off_kr = off_pool + pool_width
    ckvn = _rms(_dot(hn, win_ref[:, lora:2 * lora]), kvag_ref[...]).astype(BF16)
    kr_both = _dot(hn, win_ref[:, off_kr:off_kr + 2 * LANES])
    kr2 = kr_both[:, :LANES]
    krs2 = kr_both[:, LANES:]
    cqn = _rms(_dot(hn, win_ref[:, :lora]), qag_ref[...]).astype(BF16)
    knall = _dot(ckvn, wuk_ref[...])

    cos2 = tab_ref[:, :LANES]
    sin2 = tab_ref[:, LANES:]
    gains = gains_ref[...]
    qg_nope, qg_rope2, qg_ropes2 = gains[0:1], gains[1:2], gains[2:3]
    kg_nope, kg_rope2, kg_ropes2 = gains[3:4], gains[4:5], gains[5:6]
    lane = lax.broadcasted_iota(jnp.int32, (tm, LANES), 1)
    low_half = lane < ROPE_DIM
    nheads = MLA_HEADS

    kr_sq = jnp.where(low_half, kr2 * kr2, 0.0)
    k_roped2 = kr2 * kg_rope2 * cos2 + krs2 * kg_ropes2 * sin2
    for h in range(nheads):
        kn = knall[:, h * NOPE_DIM:(h + 1) * NOPE_DIM]
        ss = jnp.sum(kn * kn + kr_sq, axis=-1, keepdims=True)
        r = lax.rsqrt(ss * (1.0 / QK_DIM) + RMS_EPS)
        k_ref[0, h, :, 0:NOPE_DIM] = (kn * r * kg_nope).astype(BF16)
        k_ref[0, h, :, NOPE_DIM:QK_DIM] = (k_roped2 * r)[:, :ROPE_DIM].astype(BF16)

    t_in_seq = s * tm + lax.broadcasted_iota(jnp.int32, (tm, 1), 0)
    gd = pool_width // len(POOL_WINDOWS)
    pscale = pscale_ref[...]
    pair_w = 2 * NOPE_DIM + 2 * LANES

    def pool_project(g0, g1):
        return _dot(hn, win_ref[:, off_pool + g0 * gd:off_pool + g1 * gd])

    def pool_windows(u_all, g0, g1):
        pooled = []
        for gi in range(g0, g1):
            w = POOL_WINDOWS[gi]
            cols = slice(gi * gd, (gi + 1) * gd)
            u = u_all[:, (gi - g0) * gd:(gi - g0 + 1) * gd]
            prev = jnp.where(s == 0, 0.0, carry_ref[:, cols])
            carry_ref[:, cols] = u[tm - POOL_HISTORY:tm]
            e = jnp.concatenate([prev, u], axis=0)
            sh = 1
            while sh < w:
                e = e + pltpu.roll(e, sh, axis=0)
                sh *= 2
            inv_cnt = 1.0 / jnp.minimum(t_in_seq + 1, w).astype(F32)
            pooled.append((e[POOL_HISTORY:] * inv_cnt - u).astype(BF16))
        return pooled

    def pool_mix(pooled, g0):
        for k, pg in enumerate(pooled):
            cols = slice((g0 + k) * gd, (g0 + k + 1) * gd)
            b_ref[:, cols] = (_dot(pg, poolw_ref[g0 + k]) * pscale[:, cols]).astype(BF16)

    def query_project(p):
        return _dot(cqn, wuq_ref[:, p * pair_w:(p + 1) * pair_w])

    def query_finish(qp, p):
        qr2 = qp[:, 2 * NOPE_DIM:2 * NOPE_DIM + LANES]
        qrs2 = qp[:, 2 * NOPE_DIM + LANES:]
        sq = qr2 * qr2
        roped = qr2 * qg_rope2 * cos2 + qrs2 * qg_ropes2 * sin2
        for e in range(2):
            h = 2 * p + e
            qn = qp[:, e * NOPE_DIM:(e + 1) * NOPE_DIM]
            mine = low_half if e == 0 else jnp.logical_not(low_half)
            ss = jnp.sum(qn * qn + jnp.where(mine, sq, 0.0), axis=-1, keepdims=True)
            r = lax.rsqrt(ss * (1.0 / QK_DIM) + RMS_EPS) * scale
            q_ref[0, h, :, 0:NOPE_DIM] = (qn * r * qg_nope).astype(BF16)
            rp = roped * r
            if e == 1:
                rp = pltpu.roll(rp, ROPE_DIM, axis=1)
            q_ref[0, h, :, NOPE_DIM:QK_DIM] = rp[:, :ROPE_DIM].astype(BF16)

    assert len(POOL_WINDOWS) == 4 and nheads == 8
    u01 = pool_project(0, 2)
    qp0 = query_project(0)
    qp1 = query_project(1)
    pooled01 = pool_windows(u01, 0, 2)
    u23 = pool_project(2, 4)
    pool_mix(pooled01, 0)
    query_finish(qp0, 0)
    query_finish(qp1, 1)
    qp2 = query_project(2)
    qp3 = query_project(3)
    pooled23 = pool_windows(u23, 2, 4)
    pool_mix(pooled23, 2)
    vt_all = lax.dot_general(wuvt_ref[...], ckvn, (((1,), (1,)), ((), ())), preferred_element_type=F32)
    query_finish(qp2, 2)
    query_finish(qp3, 3)
    for h in range(nheads):
        vt_ref[0, h, 0, 0:V_DIM, :] = vt_all[h * V_DIM:(h + 1) * V_DIM].astype(BF16)
        vt_ref[0, h, 0, V_DIM:VT_ROWS, :] = jnp.ones((VT_ROWS - V_DIM, tm), BF16)


def _even_in_call(xf, g, win, qag, kvag, wuq, wuk, wuvt, gains, poolw, pscale, tab, e, *, B, S, tm, tk):
    T, D = xf.shape
    lora = qag.shape[1]
    pool_width = pscale.shape[1]
    n_s = S // tm
    H = MLA_HEADS
    per_kv_block = tk // tm
    tok = lambda b, s: (b * n_s + s, 0)
    const2 = lambda b, s: (0, 0)
    layer3 = lambda b, s: (e, 0, 0)
    kern = functools.partial(_even_in_kernel, tm=tm, lora=lora, pool_width=pool_width,
                             scale=QK_DIM ** -0.5 * LOG2_E)
    return pl.pallas_call(
        kern,
        grid=(B, n_s),
        in_specs=[
            pl.BlockSpec((tm, D), tok),
            _resident((1, D), const2),
            _resident((None,) + win.shape[1:], layer3),
            _resident((1, lora), const2),
            _resident((1, lora), const2),
            _resident((None,) + wuq.shape[1:], layer3),
            _resident((None,) + wuk.shape[1:], layer3),
            _resident((None,) + wuvt.shape[1:], layer3),
            _resident((None,) + gains.shape[1:], layer3),
            _resident((None,) + poolw.shape[1:], lambda b, s: (e, 0, 0, 0)),
            _resident((1, pool_width), const2),
            pl.BlockSpec((tm, 2 * LANES), tok),
        ],
        out_specs=[
            pl.BlockSpec((1, H, tm, QK_DIM), lambda b, s: (b, 0, s, 0)),
            pl.BlockSpec((1, H, tm, QK_DIM), lambda b, s: (b, 0, s, 0)),
            pl.BlockSpec((1, H, 1, VT_ROWS, tm),
                         lambda b, s: (b, 0, s // per_kv_block, 0, s % per_kv_block)),
            pl.BlockSpec((tm, pool_width), tok),
        ],
        out_shape=[
            jax.ShapeDtypeStruct((B, H, S, QK_DIM), BF16),
            jax.ShapeDtypeStruct((B, H, S, QK_DIM), BF16),
            jax.ShapeDtypeStruct((B, H, S // tk, VT_ROWS, tk), BF16),
            jax.ShapeDtypeStruct((T, pool_width), BF16),
        ],
        scratch_shapes=[pltpu.VMEM((POOL_HISTORY, pool_width), F32)],
        compiler_params=_params("arbitrary", "arbitrary"),
        name="even_in",
    )(xf, g, win, qag, kvag, wuq, wuk, wuvt, gains, poolw, pscale, tab)


ATTN_HEADS_PER_STEP = 4
ATTN_SLOTS = 2
ATTN_SCRATCH_PER_HEAD = 2 + 4 * ATTN_SLOTS


def _attn_kernel(q_ref, k_ref, vt_ref, o_ref, *scratch, tq):
    i = pl.program_id(2)
    heads = range(ATTN_HEADS_PER_STEP)
    ns = ATTN_SLOTS
    per_head = [scratch[h * ATTN_SCRATCH_PER_HEAD:(h + 1) * ATTN_SCRATCH_PER_HEAD] for h in heads]
    m_refs = [r[0] for r in per_head]
    acc_refs = [r[1] for r in per_head]
    s_refs = [r[2:2 + ns] for r in per_head]
    p_refs = [r[2 + ns:2 + 2 * ns] for r in per_head]
    a_refs = [r[2 + 2 * ns:2 + 3 * ns] for r in per_head]
    bm_refs = [r[2 + 3 * ns:2 + 4 * ns] for r in per_head]

    def scores(h, j, slot):
        start = pl.multiple_of(j * tq, tq)
        kb = k_ref[0, h, pl.ds(start, tq), :]
        s = lax.dot_general(kb, q_ref[0, h], (((1,), (1,)), ((), ())), preferred_element_type=F32)
        s_refs[h][slot][...] = s
        bm_refs[h][slot][...] = jnp.max(s, axis=0, keepdims=True)

    def softmax(h, slot, masked):
        s = s_refs[h][slot][...]
        if masked:
            key = lax.broadcasted_iota(jnp.int32, s.shape, 0)
            qry = lax.broadcasted_iota(jnp.int32, s.shape, 1)
            s = jnp.where(key <= qry, s, -1e30)
            block_max = jnp.max(s, axis=0, keepdims=True)
        else:
            block_max = bm_refs[h][slot][...]
        m_prev = m_refs[h][...]
        m_new = jnp.maximum(m_prev, block_max)
        m_refs[h][...] = m_new
        a_refs[h][slot][...] = jnp.exp2(m_prev - m_new)
        p_refs[h][slot][...] = jnp.exp2(s - m_new).astype(BF16)

    def accumulate(h, j, slot):
        acc_refs[h][...] = (a_refs[h][slot][...] * acc_refs[h][...]
                            + _dot(vt_ref[0, h, j], p_refs[h][slot][...]))

    def step(j, slot):
        for h in heads:
            scores(h, j + 1, (slot + 1) % ns)
            accumulate(h, jnp.maximum(j - 1, 0), (slot - 1) % ns)
            softmax(h, slot, False)

    def last(slot):
        for h in heads:
            accumulate(h, jnp.maximum(i - 1, 0), (slot - 1) % ns)
            softmax(h, slot, True)
            accumulate(h, i, slot)
            acc = acc_refs[h][...]
            out_t = acc[0:V_DIM] / acc[V_DIM:V_DIM + 1]
            o_ref[0, :, h * V_DIM:(h + 1) * V_DIM] = out_t.T.astype(BF16)

    for h in heads:
        m_refs[h][...] = jnp.full(m_refs[h].shape, -jnp.inf, F32)
        acc_refs[h][...] = jnp.zeros(acc_refs[h].shape, F32)
        scores(h, 0, 0)
        p_refs[h][ns - 1][...] = jnp.zeros(p_refs[h][ns - 1].shape, BF16)
        a_refs[h][ns - 1][...] = jnp.ones(a_refs[h][ns - 1].shape, F32)

    def group(g, carry):
        for slot in range(ns):
            step(ns * g + slot, slot)
        return carry

    lax.fori_loop(0, i // ns, group, 0)

    base = (i // ns) * ns
    for rem in range(ns):
        @pl.when(i % ns == rem)
        def _(rem=rem):
            for slot in range(rem):
                step(base + slot, slot)
            last(rem)


def _attn_call(q, k, vt, *, tq):
    B, H, S, _ = q.shape
    n_kv = vt.shape[2]
    assert vt.shape[4] == tq
    hps = ATTN_HEADS_PER_STEP
    head_scratch = ([pltpu.VMEM((1, tq), F32), pltpu.VMEM((VT_ROWS, tq), F32)]
                    + [pltpu.VMEM((tq, tq), F32)] * ATTN_SLOTS
                    + [pltpu.VMEM((tq, tq), BF16)] * ATTN_SLOTS
                    + [pltpu.VMEM((1, tq), F32)] * (2 * ATTN_SLOTS))
    assert len(head_scratch) == ATTN_SCRATCH_PER_HEAD
    return pl.pallas_call(
        functools.partial(_attn_kernel, tq=tq),
        grid=(B, H // hps, S // tq),
        in_specs=[
            pl.BlockSpec((1, hps, tq, QK_DIM), lambda b, h, i: (b, h, i, 0)),
            pl.BlockSpec((1, hps, S, QK_DIM), lambda b, h, i: (b, h, 0, 0)),
            pl.BlockSpec((1, hps, n_kv, VT_ROWS, tq), lambda b, h, i: (b, h, 0, 0, 0)),
        ],
        out_specs=pl.BlockSpec((1, tq, hps * V_DIM), lambda b, h, i: (b, i, h)),
        out_shape=jax.ShapeDtypeStruct((B, S, H * V_DIM), BF16),
        scratch_shapes=head_scratch * hps,
        compiler_params=_params("parallel", "parallel", "arbitrary"),
        name="attn",
    )(q, k, vt)


def _outproj_kernel(x_ref, a_ref, b_ref, wa_ref, wb_ref, o_ref):
    o_ref[...] = x_ref[...] + _dot(a_ref[...], wa_ref[...]) + _dot(b_ref[...], wb_ref[...])


def _outproj_call(xf, a, b, wout, e, *, tm):
    T, D = xf.shape
    wa_rows = a.shape[1]
    wb_rows = b.shape[1]
    assert wa_rows == wb_rows
    return pl.pallas_call(
        _outproj_kernel,
        grid=(T // tm,),
        in_specs=[
            pl.BlockSpec((tm, D), lambda i: (i, 0)),
            pl.BlockSpec((tm, wa_rows), lambda i: (i, 0)),
            pl.BlockSpec((tm, wb_rows), lambda i: (i, 0)),
            _resident((None, wa_rows, D), lambda i: (e, 0, 0)),
            _resident((None, wb_rows, D), lambda i: (e, 1, 0)),
        ],
        out_specs=pl.BlockSpec((tm, D), lambda i: (i, 0)),
        out_shape=jax.ShapeDtypeStruct((T, D), F32),
        compiler_params=_params("parallel"),
        name="outproj",
    )(xf, a, b, wout, wout)


def _conv_kernel(x_ref, g_ref, wb_ref, wc_ref, wu_ref, cw_ref, wo_ref, o_ref, hn_ref, carry_ref,
                 *, tm, tiles_per_seq):
    i = pl.program_id(0)
    c = pl.program_id(1)

    @pl.when(c == 0)
    def _():
        x = x_ref[...]
        hn_ref[...] = _rms(x, g_ref[...]).astype(BF16)
        o_ref[...] = x

    hn = hn_ref[...]
    cw = cw_ref[...]
    first_tile = i % tiles_per_seq == 0
    tc = wo_ref.shape[0]
    gated = []
    for lo in range(0, tc, MXU_TILE):
        cols = slice(lo, lo + MXU_TILE)
        gate_b = _dot(hn, wb_ref[:, cols])
        v = _dot(hn, wc_ref[:, cols]) * _dot(hn, wu_ref[:, cols])
        prev = jnp.where(first_tile, 0.0, carry_ref[c, :, cols])
        carry_ref[c, :, cols] = v[tm - CONV_HISTORY:tm]
        ext = jnp.concatenate([prev, v], axis=0)
        conv = cw[CONV_WIDTH - 1:CONV_WIDTH, cols] * v
        for back in range(1, CONV_WIDTH):
            tap = CONV_WIDTH - 1 - back
            conv = conv + cw[tap:tap + 1, cols] * pltpu.roll(ext, back, axis=0)[CONV_HISTORY:]
        gated.append((gate_b * conv).astype(BF16))
    o_ref[...] += _dot(jnp.concatenate(gated, axis=1), wo_ref[...])


def _conv_call(xf, g, win, convw, wout, o, *, S, tm, tc):
    T, D = xf.shape
    C = wout.shape[1]
    n_c = C // tc
    kern = functools.partial(_conv_kernel, tm=tm, tiles_per_seq=S // tm)
    return pl.pallas_call(
        kern,
        grid=(T // tm, n_c),
        in_specs=[
            pl.BlockSpec((tm, D), lambda i, c: (i, 0)),
            pl.BlockSpec((1, D), lambda i, c: (0, 0)),
            pl.BlockSpec((None, D, tc), lambda i, c: (o, 0, c)),
            pl.BlockSpec((None, D, tc), lambda i, c: (o, 0, n_c + c)),
            pl.BlockSpec((None, D, tc), lambda i, c: (o, 0, 2 * n_c + c)),
            pl.BlockSpec((None, CONV_WIDTH, tc), lambda i, c: (o, 0, c)),
            pl.BlockSpec((None, tc, D), lambda i, c: (o, c, 0)),
        ],
        out_specs=pl.BlockSpec((tm, D), lambda i, c: (i, 0)),
        out_shape=jax.ShapeDtypeStruct((T, D), F32),
        scratch_shapes=[pltpu.VMEM((tm, D), BF16),
                        pltpu.VMEM((n_c, CONV_HISTORY, tc), F32)],
        compiler_params=_params("arbitrary", "arbitrary"),
        name="conv_mixer",
    )(xf, g, win, win, win, convw, wout)


def _mlp_kernel(x_ref, g_ref, wup_ref, wdn_ref, o_ref, hn_ref):
    @pl.when(pl.program_id(1) == 0)
    def _():
        x = x_ref[...]
        hn_ref[...] = _rms(x, g_ref[...]).astype(BF16)
        o_ref[...] = x

    up = _dot(hn_ref[...], wup_ref[...])
    act = jnp.square(jnp.maximum(up, 0.0)).astype(BF16)
    o_ref[...] += _dot(act, wdn_ref[...])


def _mlp_call(xf, g, wup, wdn, layer, *, tm, tf):
    T, D = xf.shape
    F = wup.shape[2]
    return pl.pallas_call(
        _mlp_kernel,
        grid=(T // tm, F // tf),
        in_specs=[
            pl.BlockSpec((tm, D), lambda i, f: (i, 0)),
            pl.BlockSpec((1, D), lambda i, f: (0, 0)),
            pl.BlockSpec((None, D, tf), lambda i, f: (layer, 0, f)),
            pl.BlockSpec((None, tf, D), lambda i, f: (layer, f, 0)),
        ],
        out_specs=pl.BlockSpec((tm, D), lambda i, f: (i, 0)),
        out_shape=jax.ShapeDtypeStruct((T, D), F32),
        scratch_shapes=[pltpu.VMEM((tm, D), BF16)],
        compiler_params=_params("parallel", "arbitrary"),
        name="mlp",
    )(xf, g, wup, wdn)


def _swap_halves(a):
    half = a.shape[-1] // 2
    return jnp.concatenate([a[..., half:], a[..., :half]], axis=-1)


def _prep_even_weights(even_w_in, even_w_uq, even_w_ukv, even_q_norm_g, even_k_norm_g, lora, pool_width):
    H = MLA_HEADS
    E = even_w_in.shape[0]
    off_kr = 2 * lora
    off_pool = off_kr + ROPE_DIM
    w = even_w_in.astype(BF16)
    kr = w[:, :, off_kr:off_pool]
    krs = _swap_halves(kr)
    win = jnp.concatenate([w[:, :, :off_kr], w[:, :, off_pool:off_pool + pool_width], kr, kr, krs, krs], axis=-1)

    wq = even_w_uq.astype(BF16).reshape(E, lora, H, QK_DIM)
    rope = wq[..., NOPE_DIM:]
    by_pair = lambda a: a.reshape(E, lora, H // 2, 2 * a.shape[-1])
    wuq = jnp.concatenate([by_pair(wq[..., :NOPE_DIM]), by_pair(rope), by_pair(_swap_halves(rope))], axis=-1)
    wuq = wuq.reshape(E, lora, -1)

    wkv = even_w_ukv.astype(BF16).reshape(E, even_w_ukv.shape[1], H, NOPE_DIM + V_DIM)
    wuk = wkv[..., :NOPE_DIM].reshape(E, wkv.shape[1], H * NOPE_DIM)
    wuvt = jnp.swapaxes(wkv[..., NOPE_DIM:].reshape(E, wkv.shape[1], H * V_DIM), 1, 2)

    def gain_rows(gvec):
        rope = gvec[:, NOPE_DIM:]
        ropes = _swap_halves(rope)
        return [gvec[:, :NOPE_DIM], jnp.concatenate([rope, rope], -1), jnp.concatenate([ropes, ropes], -1)]

    rows = gain_rows(even_q_norm_g) + gain_rows(even_k_norm_g)
    rows += [jnp.zeros_like(rows[0])] * 2
    gains = jnp.stack(rows, axis=1).astype(F32)
    return win, wuq, wuk, wuvt, gains


def _pick_tile(n, want):
    t = min(n, want)
    while n % t:
        t //= 2
    return t


def kernel(x, positions, mix_norm_g, mlp_norm_g, w_mlp_up, w_mlp_down, even_w_in, even_q_a_norm_g,
           even_kv_a_norm_g, even_w_uq, even_w_ukv, even_q_norm_g, even_k_norm_g, even_pool_w,
           even_pool_scale, even_w_out, odd_w_in, odd_conv_w, odd_w_out):
    B, S, D = x.shape
    T = B * S
    depth = mix_norm_g.shape[0]
    lora = even_q_a_norm_g.shape[1]
    pool_width = even_pool_scale.shape[1]

    tm_mlp = _pick_tile(S, 512)
    tf_mlp = _pick_tile(w_mlp_up.shape[2], 2048)
    tm_even = _pick_tile(S, 512)
    tq = _pick_tile(S, 512)
    tc_conv = _pick_tile(odd_w_out.shape[1], 1024)

    wup = w_mlp_up.astype(BF16)
    wdn = w_mlp_down.astype(BF16)
    win_e, wuq, wuk, wuvt, gains = _prep_even_weights(even_w_in, even_w_uq, even_w_ukv, even_q_norm_g,
                                                      even_k_norm_g, lora, pool_width)
    poolw = even_pool_w.astype(BF16)
    wout_e = even_w_out.astype(BF16)
    win_o = odd_w_in.astype(BF16)
    wout_o = odd_w_out.astype(BF16)

    tab = _rope_table_call(positions.reshape(T, 1), _pick_tile(T, 1024))

    xf = x.reshape(T, D)
    for layer in range(depth):
        g_mix = mix_norm_g[layer][None, :]
        if layer % 2 == 0:
            e = layer // 2
            q, k, vt, b = _even_in_call(
                xf, g_mix, win_e, even_q_a_norm_g[e][None, :], even_kv_a_norm_g[e][None, :],
                wuq, wuk, wuvt, gains, poolw, even_pool_scale[e][None, :], tab, e,
                B=B, S=S, tm=tm_even, tk=tq)
            a = _attn_call(q, k, vt, tq=tq).reshape(T, MLA_HEADS * V_DIM)
            xf = _outproj_call(xf, a, b, wout_e, e, tm=tm_mlp)
        else:
            o = layer // 2
            xf = _conv_call(xf, g_mix, win_o, odd_conv_w, wout_o, o, S=S, tm=tm_mlp, tc=tc_conv)
        xf = _mlp_call(xf, mlp_norm_g[layer][None, :], wup, wdn, layer, tm=tm_mlp, tf=tf_mlp)
    return xf.reshape(B, S, D)
```

```python
import functools

import jax
import jax.numpy as jnp
import numpy as np
from jax import lax
from jax.experimental import pallas as pl
from jax.experimental.pallas import tpu as pltpu

F32 = jnp.float32
BF16 = jnp.bfloat16

RMS_EPS = 1e-6
ROPE_THETA = 10000.0
MLA_HEADS = 8
NOPE_DIM = 128
ROPE_DIM = 64
QK_DIM = NOPE_DIM + ROPE_DIM
V_DIM = 128
VT_ROWS = V_DIM + 16
LOG2_E = 1.4426950408889634
POOL_WINDOWS = (2, 4, 8, 16)
POOL_HISTORY = 16
CONV_WIDTH = 3
CONV_HISTORY = 8
LANES = 128
MXU_TILE = 256

VMEM_LIMIT_BYTES = 60 * 1024 * 1024


def _rms(xf, g):
    ms = jnp.mean(xf * xf, axis=-1, keepdims=True)
    return xf * lax.rsqrt(ms + RMS_EPS) * g


def _dot(a, b):
    return jnp.dot(a, b, preferred_element_type=F32)


def _store_inv_rms(x_ref, r_ref):
    x = x_ref[...]
    r = lax.rsqrt(jnp.mean(x * x, axis=-1, keepdims=True) + RMS_EPS)
    r_ref[...] = jnp.broadcast_to(r, r_ref.shape)


def _normed_operand(x, r_ref, g_ref):
    return (x * jnp.tile(r_ref[...], (1, x.shape[1] // LANES)) * g_ref[...]).astype(BF16)


def _params(*sem):
    return pltpu.CompilerParams(dimension_semantics=sem, vmem_limit_bytes=VMEM_LIMIT_BYTES)


def _resident(block_shape, index_map):
    return pl.BlockSpec(block_shape, index_map, pipeline_mode=pl.Buffered(1))


def _rope_table_kernel(pos_ref, freq_ref, tab_ref):
    ang = pos_ref[...].astype(F32) * freq_ref[...]
    lane = lax.broadcasted_iota(jnp.int32, ang.shape, 1)
    sign = jnp.where((lane % ROPE_DIM) < ROPE_DIM // 2, -1.0, 1.0)
    tab_ref[:, :LANES] = jnp.cos(ang)
    tab_ref[:, LANES:] = jnp.sin(ang) * sign


def _rope_table_call(pos_col, tm):
    T = pos_col.shape[0]
    inv_freq = 1.0 / (ROPE_THETA ** (jnp.arange(0, ROPE_DIM, 2, dtype=F32) / ROPE_DIM))
    freq = jnp.tile(inv_freq, LANES // (ROPE_DIM // 2))[None, :]
    return pl.pallas_call(
        _rope_table_kernel,
        grid=(T // tm,),
        in_specs=[pl.BlockSpec((tm, 1), lambda i: (i, 0)),
                  pl.BlockSpec((1, LANES), lambda i: (0, 0))],
        out_specs=pl.BlockSpec((tm, 2 * LANES), lambda i: (i, 0)),
        out_shape=jax.ShapeDtypeStruct((T, 2 * LANES), F32),
        compiler_params=_params("parallel"),
        name="rope_table",
    )(pos_col, freq)


def _even_in_kernel(x_ref, g_ref, win_ref, qag_ref, kvag_ref, wuq_ref, wuk_ref, wuvt_ref, gains_ref,
                    poolw_ref, pscale_ref, tab_ref,
                    q_ref, k_ref, vt_ref, b_ref, carry_ref, *, tm, lora, pool_width, scale):
    s = pl.program_id(1)
    hn = _rms(x_ref[...], g_ref[...]).astype(BF16)
    off_pool = 2 * lora
    off_kr = off_pool + pool_width
    ckvn = _rms(_dot(hn, win_ref[:, lora:2 * lora]), kvag_ref[...]).astype(BF16)
    kr_both = _dot(hn, win_ref[:, off_kr:off_kr + 2 * LANES])
    kr2 = kr_both[:, :LANES]
    krs2 = kr_both[:, LANES:]
    cqn = _rms(_dot(hn, win_ref[:, :lora]), qag_ref[...]).astype(BF16)
    knall = _dot(ckvn, wuk_ref[...])

    cos2 = tab_ref[:, :LANES]
    sin2 = tab_ref[:, LANES:]
    gains = gains_ref[...]
    qg_nope, qg_rope2, qg_ropes2 = gains[0:1], gains[1:2], gains[2:3]
    kg_nope, kg_rope2, kg_ropes2 = gains[3:4], gains[4:5], gains[5:6]
    lane = lax.broadcasted_iota(jnp.int32, (tm, LANES), 1)
    low_half = lane < ROPE_DIM
    nheads = MLA_HEADS

    kr_sq = jnp.where(low_half, kr2 * kr2, 0.0)
    k_roped2 = kr2 * kg_rope2 * cos2 + krs2 * kg_ropes2 * sin2
    for h in range(nheads):
        kn = knall[:, h * NOPE_DIM:(h + 1) * NOPE_DIM]
        ss = jnp.sum(kn * kn + kr_sq, axis=-1, keepdims=True)
        r = lax.rsqrt(ss * (1.0 / QK_DIM) + RMS_EPS)
        k_ref[0, h, :, 0:NOPE_DIM] = (kn * r * kg_nope).astype(BF16)
        k_ref[0, h, :, NOPE_DIM:QK_DIM] = (k_roped2 * r)[:, :ROPE_DIM].astype(BF16)

    t_in_seq = s * tm + lax.broadcasted_iota(jnp.int32, (tm, 1), 0)
    gd = pool_width // len(POOL_WINDOWS)
    pscale = pscale_ref[...]
    pair_w = 2 * NOPE_DIM + 2 * LANES

    def pool_project(g0, g1):
        return _dot(hn, win_ref[:, off_pool + g0 * gd:off_pool + g1 * gd])

    def pool_windows(u_all, g0, g1):
        pooled = []
        for gi in range(g0, g1):
            w = POOL_WINDOWS[gi]
            cols = slice(gi * gd, (gi + 1) * gd)
            u = u_all[:, (gi - g0) * gd:(gi - g0 + 1) * gd]
            prev = jnp.where(s == 0, 0.0, carry_ref[:, cols])
            carry_ref[:, cols] = u[tm - POOL_HISTORY:tm]
            e = jnp.concatenate([prev, u], axis=0)
            sh = 1
            while sh < w:
                e = e + pltpu.roll(e, sh, axis=0)
                sh *= 2
            inv_cnt = 1.0 / jnp.minimum(t_in_seq + 1, w).astype(F32)
            pooled.append((e[POOL_HISTORY:] * inv_cnt - u).astype(BF16))
        return pooled

    def pool_mix(pooled, g0):
        for k, pg in enumerate(pooled):
            cols = slice((g0 + k) * gd, (g0 + k + 1) * gd)
            b_ref[:, cols] = (_dot(pg, poolw_ref[g0 + k]) * pscale[:, cols]).astype(BF16)

    def query_project(p):
        return _dot(cqn, wuq_ref[:, p * pair_w:(p + 1) * pair_w])

    def query_finish(qp, p):
        qr2 = qp[:, 2 * NOPE_DIM:2 * NOPE_DIM + LANES]
        qrs2 = qp[:, 2 * NOPE_DIM + LANES:]
        sq = qr2 * qr2
        roped = qr2 * qg_rope2 * cos2 + qrs2 * qg_ropes2 * sin2
        for e in range(2):
            h = 2 * p + e
            qn = qp[:, e * NOPE_DIM:(e + 1) * NOPE_DIM]
            mine = low_half if e == 0 else jnp.logical_not(low_half)
            ss = jnp.sum(qn * qn + jnp.where(mine, sq, 0.0), axis=-1, keepdims=True)
            r = lax.rsqrt(ss * (1.0 / QK_DIM) + RMS_EPS) * scale
            q_ref[0, h, :, 0:NOPE_DIM] = (qn * r * qg_nope).astype(BF16)
            rp = roped * r
            if e == 1:
                rp = pltpu.roll(rp, ROPE_DIM, axis=1)
            q_ref[0, h, :, NOPE_DIM:QK_DIM] = rp[:, :ROPE_DIM].astype(BF16)

    assert len(POOL_WINDOWS) == 4 and nheads == 8
    u01 = pool_project(0, 2)
    qp0 = query_project(0)
    qp1 = query_project(1)
    pooled01 = pool_windows(u01, 0, 2)
    u23 = pool_project(2, 4)
    pool_mix(pooled01, 0)
    query_finish(qp0, 0)
    query_finish(qp1, 1)
    qp2 = query_project(2)
    qp3 = query_project(3)
    pooled23 = pool_windows(u23, 2, 4)
    pool_mix(pooled23, 2)
    vt_all = lax.dot_general(wuvt_ref[...], ckvn, (((1,), (1,)), ((), ())), preferred_element_type=F32)
    query_finish(qp2, 2)
    query_finish(qp3, 3)
    for h in range(nheads):
        vt_ref[0, h, 0, 0:V_DIM, :] = vt_all[h * V_DIM:(h + 1) * V_DIM].astype(BF16)
        vt_ref[0, h, 0, V_DIM:VT_ROWS, :] = jnp.ones((VT_ROWS - V_DIM, tm), BF16)


def _even_in_call(xf, g, win, qag, kvag, wuq, wuk, wuvt, gains, poolw, pscale, tab, e, *, B, S, tm, tk):
    T, D = xf.shape
    lora = qag.shape[1]
    pool_width = pscale.shape[1]
    n_s = S // tm
    H = MLA_HEADS
    per_kv_block = tk // tm
    tok = lambda b, s: (b * n_s + s, 0)
    const2 = lambda b, s: (0, 0)
    layer3 = lambda b, s: (e, 0, 0)
    kern = functools.partial(_even_in_kernel, tm=tm, lora=lora, pool_width=pool_width,
                             scale=QK_DIM ** -0.5 * LOG2_E)
    return pl.pallas_call(
        kern,
        grid=(B, n_s),
        in_specs=[
            pl.BlockSpec((tm, D), tok),
            _resident((1, D), const2),
            _resident((None,) + win.shape[1:], layer3),
            _resident((1, lora), const2),
            _resident((1, lora), const2),
            _resident((None,) + wuq.shape[1:], layer3),
            _resident((None,) + wuk.shape[1:], layer3),
            _resident((None,) + wuvt.shape[1:], layer3),
            _resident((None,) + gains.shape[1:], layer3),
            _resident((None,) + poolw.shape[1:], lambda b, s: (e, 0, 0, 0)),
            _resident((1, pool_width), const2),
            pl.BlockSpec((tm, 2 * LANES), tok),
        ],
        out_specs=[
            pl.BlockSpec((1, H, tm, QK_DIM), lambda b, s: (b, 0, s, 0)),
            pl.BlockSpec((1, H, tm, QK_DIM), lambda b, s: (b, 0, s, 0)),
            pl.BlockSpec((1, H, 1, VT_ROWS, tm),
                         lambda b, s: (b, 0, s // per_kv_block, 0, s % per_kv_block)),
            pl.BlockSpec((tm, pool_width), tok),
        ],
        out_shape=[
            jax.ShapeDtypeStruct((B, H, S, QK_DIM), BF16),
            jax.ShapeDtypeStruct((B, H, S, QK_DIM), BF16),
            jax.ShapeDtypeStruct((B, H, S // tk, VT_ROWS, tk), BF16),
            jax.ShapeDtypeStruct((T, pool_width), BF16),
        ],
        scratch_shapes=[pltpu.VMEM((POOL_HISTORY, pool_width), F32)],
        compiler_params=_params("arbitrary", "arbitrary"),
        name="even_in",
    )(xf, g, win, qag, kvag, wuq, wuk, wuvt, gains, poolw, pscale, tab)


ATTN_HEADS_PER_STEP = 4
ATTN_SLOTS = 2
ATTN_SCRATCH_PER_HEAD = 2 + 4 * ATTN_SLOTS


def _attn_kernel(q_ref, k_ref, vt_ref, o_ref, *scratch, tq):
    i = pl.program_id(2)
    heads = range(ATTN_HEADS_PER_STEP)
    ns = ATTN_SLOTS
    per_head = [scratch[h * ATTN_SCRATCH_PER_HEAD:(h + 1) * ATTN_SCRATCH_PER_HEAD] for h in heads]
    m_refs = [r[0] for r in per_head]
    acc_refs = [r[1] for r in per_head]
    s_refs = [r[2:2 + ns] for r in per_head]
    p_refs = [r[2 + ns:2 + 2 * ns] for r in per_head]
    a_refs = [r[2 + 2 * ns:2 + 3 * ns] for r in per_head]
    bm_refs = [r[2 + 3 * ns:2 + 4 * ns] for r in per_head]

    def scores(h, j, slot):
        start = pl.multiple_of(j * tq, tq)
        kb = k_ref[0, h, pl.ds(start, tq), :]
        s = lax.dot_general(kb, q_ref[0, h], (((1,), (1,)), ((), ())), preferred_element_type=F32)
        s_refs[h][slot][...] = s
        bm_refs[h][slot][...] = jnp.max(s, axis=0, keepdims=True)

    def softmax(h, slot, masked):
        s = s_refs[h][slot][...]
        if masked:
            key = lax.broadcasted_iota(jnp.int32, s.shape, 0)
            qry = lax.broadcasted_iota(jnp.int32, s.shape, 1)
            s = jnp.where(key <= qry, s, -1e30)
            block_max = jnp.max(s, axis=0, keepdims=True)
        else:
            block_max = bm_refs[h][slot][...]
        m_prev = m_refs[h][...]
        m_new = jnp.maximum(m_prev, block_max)
        m_refs[h][...] = m_new
        a_refs[h][slot][...] = jnp.exp2(m_prev - m_new)
        p_refs[h][slot][...] = jnp.exp2(s - m_new).astype(BF16)

    def accumulate(h, j, slot):
        acc_refs[h][...] = (a_refs[h][slot][...] * acc_refs[h][...]
                            + _dot(vt_ref[0, h, j], p_refs[h][slot][...]))

    def step(j, slot):
        for h in heads:
            scores(h, j + 1, (slot + 1) % ns)
            accumulate(h, jnp.maximum(j - 1, 0), (slot - 1) % ns)
            softmax(h, slot, False)

    def last(slot):
        for h in heads:
            accumulate(h, jnp.maximum(i - 1, 0), (slot - 1) % ns)
            softmax(h, slot, True)
            accumulate(h, i, slot)
            acc = acc_refs[h][...]
            out_t = acc[0:V_DIM] / acc[V_DIM:V_DIM + 1]
            o_ref[0, :, h * V_DIM:(h + 1) * V_DIM] = out_t.T.astype(BF16)

    for h in heads:
        m_refs[h][...] = jnp.full(m_refs[h].shape, -jnp.inf, F32)
        acc_refs[h][...] = jnp.zeros(acc_refs[h].shape, F32)
        scores(h, 0, 0)
        p_refs[h][ns - 1][...] = jnp.zeros(p_refs[h][ns - 1].shape, BF16)
        a_refs[h][ns - 1][...] = jnp.ones(a_refs[h][ns - 1].shape, F32)

    def group(g, carry):
        for slot in range(ns):
            step(ns * g + slot, slot)
        return carry

    lax.fori_loop(0, i // ns, group, 0)

    base = (i // ns) * ns
    for rem in range(ns):
        @pl.when(i % ns == rem)
        def _(rem=rem):
            for slot in range(rem):
                step(base + slot, slot)
            last(rem)


def _attn_call(q, k, vt, *, tq):
    B, H, S, _ = q.shape
    n_kv = vt.shape[2]
    assert vt.shape[4] == tq
    hps = ATTN_HEADS_PER_STEP
    head_scratch = ([pltpu.VMEM((1, tq), F32), pltpu.VMEM((VT_ROWS, tq), F32)]
                    + [pltpu.VMEM((tq, tq), F32)] * ATTN_SLOTS
                    + [pltpu.VMEM((tq, tq), BF16)] * ATTN_SLOTS
                    + [pltpu.VMEM((1, tq), F32)] * (2 * ATTN_SLOTS))
    assert len(head_scratch) == ATTN_SCRATCH_PER_HEAD
    return pl.pallas_call(
        functools.partial(_attn_kernel, tq=tq),
        grid=(B, H // hps, S // tq),
        in_specs=[
            pl.BlockSpec((1, hps, tq, QK_DIM), lambda b, h, i: (b, h, i, 0)),
            pl.BlockSpec((1, hps, S, QK_DIM), lambda b, h, i: (b, h, 0, 0)),
            pl.BlockSpec((1, hps, n_kv, VT_ROWS, tq), lambda b, h, i: (b, h, 0, 0, 0)),
        ],
        out_specs=pl.BlockSpec((1, tq, hps * V_DIM), lambda b, h, i: (b, i, h)),
        out_shape=jax.ShapeDtypeStruct((B, S, H * V_DIM), BF16),
        scratch_shapes=head_scratch * hps,
        compiler_params=_params("parallel", "parallel", "arbitrary"),
        name="attn",
    )(q, k, vt)


def _outproj_kernel(x_ref, a_ref, b_ref, wa_ref, wb_ref, o_ref):
    o_ref[...] = x_ref[...] + _dot(a_ref[...], wa_ref[...]) + _dot(b_ref[...], wb_ref[...])


def _outproj_call(xf, a, b, wout, e, *, tm):
    T, D = xf.shape
    wa_rows = a.shape[1]
    wb_rows = b.shape[1]
    assert wa_rows == wb_rows
    return pl.pallas_call(
        _outproj_kernel,
        grid=(T // tm,),
        in_specs=[
            pl.BlockSpec((tm, D), lambda i: (i, 0)),
            pl.BlockSpec((tm, wa_rows), lambda i: (i, 0)),
            pl.BlockSpec((tm, wb_rows), lambda i: (i, 0)),
            _resident((None, wa_rows, D), lambda i: (e, 0, 0)),
            _resident((None, wb_rows, D), lambda i: (e, 1, 0)),
        ],
        out_specs=pl.BlockSpec((tm, D), lambda i: (i, 0)),
        out_shape=jax.ShapeDtypeStruct((T, D), F32),
        compiler_params=_params("parallel"),
        name="outproj",
    )(xf, a, b, wout, wout)


def _conv_kernel(x_ref, g_ref, wb_ref, wc_ref, wu_ref, cw_ref, wo_ref, o_ref, r_ref, carry_ref,
                 *, tm, tiles_per_seq):
    i = pl.program_id(0)
    c = pl.program_id(1)
    first = c == 0

    @pl.when(first)
    def _():
        _store_inv_rms(x_ref, r_ref)

    x = x_ref[...]
    hn = _normed_operand(x, r_ref, g_ref)
    cw = cw_ref[...]
    first_tile = i % tiles_per_seq == 0
    tc = wo_ref.shape[0]
    gated = []
    for lo in range(0, tc, MXU_TILE):
        cols = slice(lo, lo + MXU_TILE)
        gate_b = _dot(hn, wb_ref[:, cols])
        v = _dot(hn, wc_ref[:, cols]) * _dot(hn, wu_ref[:, cols])
        prev = jnp.where(first_tile, 0.0, carry_ref[c, :, cols])
        carry_ref[c, :, cols] = v[tm - CONV_HISTORY:tm]
        ext = jnp.concatenate([prev, v], axis=0)
        conv = cw[CONV_WIDTH - 1:CONV_WIDTH, cols] * v
        for back in range(1, CONV_WIDTH):
            tap = CONV_WIDTH - 1 - back
            conv = conv + cw[tap:tap + 1, cols] * pltpu.roll(ext, back, axis=0)[CONV_HISTORY:]
        gated.append((gate_b * conv).astype(BF16))
    o_ref[...] = jnp.where(first, x, o_ref[...]) + _dot(jnp.concatenate(gated, axis=1), wo_ref[...])


def _conv_call(xf, g, win, convw, wout, o, *, S, tm, tc):
    T, D = xf.shape
    C = wout.shape[1]
    n_c = C // tc
    kern = functools.partial(_conv_kernel, tm=tm, tiles_per_seq=S // tm)
    return pl.pallas_call(
        kern,
        grid=(T // tm, n_c),
        in_specs=[
            pl.BlockSpec((tm, D), lambda i, c: (i, 0)),
            pl.BlockSpec((1, D), lambda i, c: (0, 0)),
            pl.BlockSpec((None, D, tc), lambda i, c: (o, 0, c)),
            pl.BlockSpec((None, D, tc), lambda i, c: (o, 0, n_c + c)),
            pl.BlockSpec((None, D, tc), lambda i, c: (o, 0, 2 * n_c + c)),
            pl.BlockSpec((None, CONV_WIDTH, tc), lambda i, c: (o, 0, c)),
            pl.BlockSpec((None, tc, D), lambda i, c: (o, c, 0)),
        ],
        out_specs=pl.BlockSpec((tm, D), lambda i, c: (i, 0)),
        out_shape=jax.ShapeDtypeStruct((T, D), F32),
        scratch_shapes=[pltpu.VMEM((tm, LANES), F32),
                        pltpu.VMEM((n_c, CONV_HISTORY, tc), F32)],
        compiler_params=_params("arbitrary", "arbitrary"),
        name="conv_mixer",
    )(xf, g, win, win, win, convw, wout)


def _mlp_kernel(x_ref, g_ref, wup_ref, wdn_ref, o_ref, r_ref):
    first = pl.program_id(1) == 0

    @pl.when(first)
    def _():
        _store_inv_rms(x_ref, r_ref)

    x = x_ref[...]
    up = _dot(_normed_operand(x, r_ref, g_ref), wup_ref[...])
    act = jnp.square(jnp.maximum(up, 0.0)).astype(BF16)
    o_ref[...] = jnp.where(first, x, o_ref[...]) + _dot(act, wdn_ref[...])


def _mlp_call(xf, g, wup, wdn, layer, *, tm, tf):
    T, D = xf.shape
    F = wup.shape[2]
    return pl.pallas_call(
        _mlp_kernel,
        grid=(T // tm, F // tf),
        in_specs=[
            pl.BlockSpec((tm, D), lambda i, f: (i, 0)),
            pl.BlockSpec((1, D), lambda i, f: (0, 0)),
            pl.BlockSpec((None, D, tf), lambda i, f: (layer, 0, f)),
            pl.BlockSpec((None, tf, D), lambda i, f: (layer, f, 0)),
        ],
        out_specs=pl.BlockSpec((tm, D), lambda i, f: (i, 0)),
        out_shape=jax.ShapeDtypeStruct((T, D), F32),
        scratch_shapes=[pltpu.VMEM((tm, LANES), F32)],
        compiler_params=_params("parallel", "arbitrary"),
        name="mlp",
    )(xf, g, wup, wdn)


def _swap_halves(a):
    half = a.shape[-1] // 2
    return jnp.concatenate([a[..., half:], a[..., :half]], axis=-1)


def _prep_even_weights(even_w_in, even_w_uq, even_w_ukv, even_q_norm_g, even_k_norm_g, lora, pool_width):
    H = MLA_HEADS
    E = even_w_in.shape[0]
    off_kr = 2 * lora
    off_pool = off_kr + ROPE_DIM
    w = even_w_in.astype(BF16)
    kr = w[:, :, off_kr:off_pool]
    krs = _swap_halves(kr)
    win = jnp.concatenate([w[:, :, :off_kr], w[:, :, off_pool:off_pool + pool_width], kr, kr, krs, krs], axis=-1)

    wq = even_w_uq.astype(BF16).reshape(E, lora, H, QK_DIM)
    rope = wq[..., NOPE_DIM:]
    by_pair = lambda a: a.reshape(E, lora, H // 2, 2 * a.shape[-1])
    wuq = jnp.concatenate([by_pair(wq[..., :NOPE_DIM]), by_pair(rope), by_pair(_swap_halves(rope))], axis=-1)
    wuq = wuq.reshape(E, lora, -1)

    wkv = even_w_ukv.astype(BF16).reshape(E, even_w_ukv.shape[1], H, NOPE_DIM + V_DIM)
    wuk = wkv[..., :NOPE_DIM].reshape(E, wkv.shape[1], H * NOPE_DIM)
    wuvt = jnp.swapaxes(wkv[..., NOPE_DIM:].reshape(E, wkv.shape[1], H * V_DIM), 1, 2)

    def gain_rows(gvec):
        rope = gvec[:, NOPE_DIM:]
        ropes = _swap_halves(rope)
        return [gvec[:, :NOPE_DIM], jnp.concatenate([rope, rope], -1), jnp.concatenate([ropes, ropes], -1)]

    rows = gain_rows(even_q_norm_g) + gain_rows(even_k_norm_g)
    rows += [jnp.zeros_like(rows[0])] * 2
    gains = jnp.stack(rows, axis=1).astype(F32)
    return win, wuq, wuk, wuvt, gains


def _pick_tile(n, want):
    t = min(n, want)
    while n % t:
        t //= 2
    return t


def kernel(x, positions, mix_norm_g, mlp_norm_g, w_mlp_up, w_mlp_down, even_w_in, even_q_a_norm_g,
           even_kv_a_norm_g, even_w_uq, even_w_ukv, even_q_norm_g, even_k_norm_g, even_pool_w,
           even_pool_scale, even_w_out, odd_w_in, odd_conv_w, odd_w_out):
    B, S, D = x.shape
    T = B * S
    depth = mix_norm_g.shape[0]
    lora = even_q_a_norm_g.shape[1]
    pool_width = even_pool_scale.shape[1]

    tm_mlp = _pick_tile(S, 512)
    tf_mlp = _pick_tile(w_mlp_up.shape[2], 2048)
    tm_even = _pick_tile(S, 512)
    tq = _pick_tile(S, 512)
    tc_conv = _pick_tile(odd_w_out.shape[1], 1024)

    wup = w_mlp_up.astype(BF16)
    wdn = w_mlp_down.astype(BF16)
    win_e, wuq, wuk, wuvt, gains = _prep_even_weights(even_w_in, even_w_uq, even_w_ukv, even_q_norm_g,
                                                      even_k_norm_g, lora, pool_width)
    poolw = even_pool_w.astype(BF16)
    wout_e = even_w_out.astype(BF16)
    win_o = odd_w_in.astype(BF16)
    wout_o = odd_w_out.astype(BF16)

    tab = _rope_table_call(positions.reshape(T, 1), _pick_tile(T, 1024))

    xf = x.reshape(T, D)
    for layer in range(depth):
        g_mix = mix_norm_g[layer][None, :]
        if layer % 2 == 0:
            e = layer // 2
            q, k, vt, b = _even_in_call(
                xf, g_mix, win_e, even_q_a_norm_g[e][None, :], even_kv_a_norm_g[e][None, :],
                wuq, wuk, wuvt, gains, poolw, even_pool_scale[e][None, :], tab, e,
                B=B, S=S, tm=tm_even, tk=tq)
            a = _attn_call(q, k, vt, tq=tq).reshape(T, MLA_HEADS * V_DIM)
            xf = _outproj_call(xf, a, b, wout_e, e, tm=tm_mlp)
        else:
            o = layer // 2
            xf = _conv_call(xf, g_mix, win_o, odd_conv_w, wout_o, o, S=S, tm=tm_mlp, tc=tc_conv)
        xf = _mlp_call(xf, mlp_norm_g[layer][None, :], wup, wdn, layer, tm=tm_mlp, tf=tf_mlp)
    return xf.reshape(B, S, D)
```

```python
import functools

import jax
import jax.numpy as jnp
import numpy as np
from jax import lax
from jax.experimental import pallas as pl
from jax.experimental.pallas import tpu as pltpu

F32 = jnp.float32
BF16 = jnp.bfloat16

RMS_EPS = 1e-6
ROPE_THETA = 10000.0
MLA_HEADS = 8
NOPE_DIM = 128
ROPE_DIM = 64
QK_DIM = NOPE_DIM + ROPE_DIM
V_DIM = 128
VT_ROWS = V_DIM + 16
LOG2_E = 1.4426950408889634
POOL_WINDOWS = (2, 4, 8, 16)
POOL_HISTORY = 16
CONV_WIDTH = 3
CONV_HISTORY = 8
LANES = 128
MXU_TILE = 256

VMEM_LIMIT_BYTES = 60 * 1024 * 1024


def _rms(xf, g):
    ms = jnp.mean(xf * xf, axis=-1, keepdims=True)
    return xf * lax.rsqrt(ms + RMS_EPS) * g


def _dot(a, b):
    return jnp.dot(a, b, preferred_element_type=F32)


def _store_inv_rms(x_ref, r_ref):
    x = x_ref[...]
    r = lax.rsqrt(jnp.mean(x * x, axis=-1, keepdims=True) + RMS_EPS)
    r_ref[...] = jnp.broadcast_to(r, r_ref.shape)


def _normed_operand(x, r_ref, g_ref):
    return (x * jnp.tile(r_ref[...], (1, x.shape[1] // LANES)) * g_ref[...]).astype(BF16)


def _params(*sem):
    return pltpu.CompilerParams(dimension_semantics=sem, vmem_limit_bytes=VMEM_LIMIT_BYTES)


def _resident(block_shape, index_map):
    return pl.BlockSpec(block_shape, index_map, pipeline_mode=pl.Buffered(1))


def _rope_table_kernel(pos_ref, freq_ref, tab_ref):
    ang = pos_ref[...].astype(F32) * freq_ref[...]
    lane = lax.broadcasted_iota(jnp.int32, ang.shape, 1)
    sign = jnp.where((lane % ROPE_DIM) < ROPE_DIM // 2, -1.0, 1.0)
    tab_ref[:, :LANES] = jnp.cos(ang)
    tab_ref[:, LANES:] = jnp.sin(ang) * sign


def _rope_table_call(pos_col, tm):
    T = pos_col.shape[0]
    inv_freq = 1.0 / (ROPE_THETA ** (jnp.arange(0, ROPE_DIM, 2, dtype=F32) / ROPE_DIM))
    freq = jnp.tile(inv_freq, LANES // (ROPE_DIM // 2))[None, :]
    return pl.pallas_call(
        _rope_table_kernel,
        grid=(T // tm,),
        in_specs=[pl.BlockSpec((tm, 1), lambda i: (i, 0)),
                  pl.BlockSpec((1, LANES), lambda i: (0, 0))],
        out_specs=pl.BlockSpec((tm, 2 * LANES), lambda i: (i, 0)),
        out_shape=jax.ShapeDtypeStruct((T, 2 * LANES), F32),
        compiler_params=_params("parallel"),
        name="rope_table",
    )(pos_col, freq)


def _even_in_kernel(x_ref, g_ref, win_ref, qag_ref, kvag_ref, wuq_ref, wuk_ref, wuvt_ref, gains_ref,
                    poolw_ref, pscale_ref, tab_ref,
                    q_ref, k_ref, vt_ref, b_ref, carry_ref, *, tm, lora, pool_width, scale):
    s = pl.program_id(1)
    hn = _rms(x_ref[...], g_ref[...]).astype(BF16)
    off_pool = 2 * lora
    off_kr = off_pool + pool_width
    ckvn = _rms(_dot(hn, win_ref[:, lora:2 * lora]), kvag_ref[...]).astype(BF16)
    kr_both = _dot(hn, win_ref[:, off_kr:off_kr + 2 * LANES])
    kr2 = kr_both[:, :LANES]
    krs2 = kr_both[:, LANES:]
    cqn = _rms(_dot(hn, win_ref[:, :lora]), qag_ref[...]).astype(BF16)
    knall = _dot(ckvn, wuk_ref[...])

    cos2 = tab_ref[:, :LANES]
    sin2 = tab_ref[:, LANES:]
    gains = gains_ref[...]
    qg_nope, qg_rope2, qg_ropes2 = gains[0:1], gains[1:2], gains[2:3]
    kg_nope, kg_rope2, kg_ropes2 = gains[3:4], gains[4:5], gains[5:6]
    lane = lax.broadcasted_iota(jnp.int32, (tm, LANES), 1)
    low_half = lane < ROPE_DIM
    nheads = MLA_HEADS

    kr_sq = jnp.where(low_half, kr2 * kr2, 0.0)
    k_roped2 = kr2 * kg_rope2 * cos2 + krs2 * kg_ropes2 * sin2
    for h in range(nheads):
        kn = knall[:, h * NOPE_DIM:(h + 1) * NOPE_DIM]
        ss = jnp.sum(kn * kn + kr_sq, axis=-1, keepdims=True)
        r = lax.rsqrt(ss * (1.0 / QK_DIM) + RMS_EPS)
        k_ref[0, h, :, 0:NOPE_DIM] = (kn * r * kg_nope).astype(BF16)
        k_ref[0, h, :, NOPE_DIM:QK_DIM] = (k_roped2 * r)[:, :ROPE_DIM].astype(BF16)

    t_in_seq = s * tm + lax.broadcasted_iota(jnp.int32, (tm, 1), 0)
    gd = pool_width // len(POOL_WINDOWS)
    pscale = pscale_ref[...]
    pair_w = 2 * NOPE_DIM + 2 * LANES

    def pool_project(g0, g1):
        return _dot(hn, win_ref[:, off_pool + g0 * gd:off_pool + g1 * gd])

    def pool_windows(u_all, g0, g1):
        pooled = []
        for gi in range(g0, g1):
            w = POOL_WINDOWS[gi]
            cols = slice(gi * gd, (gi + 1) * gd)
            u = u_all[:, (gi - g0) * gd:(gi - g0 + 1) * gd]
            prev = jnp.where(s == 0, 0.0, carry_ref[:, cols])
            carry_ref[:, cols] = u[tm - POOL_HISTORY:tm]
            e = jnp.concatenate([prev, u], axis=0)
            sh = 1
            while sh < w:
                e = e + pltpu.roll(e, sh, axis=0)
                sh *= 2
            inv_cnt = 1.0 / jnp.minimum(t_in_seq + 1, w).astype(F32)
            pooled.append((e[POOL_HISTORY:] * inv_cnt - u).astype(BF16))
        return pooled

    def pool_mix(pooled, g0):
        for k, pg in enumerate(pooled):
            cols = slice((g0 + k) * gd, (g0 + k + 1) * gd)
            b_ref[:, cols] = (_dot(pg, poolw_ref[g0 + k]) * pscale[:, cols]).astype(BF16)

    def query_project(p):
        return _dot(cqn, wuq_ref[:, p * pair_w:(p + 1) * pair_w])

    def query_finish(qp, p):
        qr2 = qp[:, 2 * NOPE_DIM:2 * NOPE_DIM + LANES]
        qrs2 = qp[:, 2 * NOPE_DIM + LANES:]
        sq = qr2 * qr2
        roped = qr2 * qg_rope2 * cos2 + qrs2 * qg_ropes2 * sin2
        for e in range(2):
            h = 2 * p + e
            qn = qp[:, e * NOPE_DIM:(e + 1) * NOPE_DIM]
            mine = low_half if e == 0 else jnp.logical_not(low_half)
            ss = jnp.sum(qn * qn + jnp.where(mine, sq, 0.0), axis=-1, keepdims=True)
            r = lax.rsqrt(ss * (1.0 / QK_DIM) + RMS_EPS) * scale
            q_ref[0, h, :, 0:NOPE_DIM] = (qn * r * qg_nope).astype(BF16)
            rp = roped * r
            if e == 1:
                rp = pltpu.roll(rp, ROPE_DIM, axis=1)
            q_ref[0, h, :, NOPE_DIM:QK_DIM] = rp[:, :ROPE_DIM].astype(BF16)

    assert len(POOL_WINDOWS) == 4 and nheads == 8
    u01 = pool_project(0, 2)
    qp0 = query_project(0)
    qp1 = query_project(1)
    pooled01 = pool_windows(u01, 0, 2)
    u23 = pool_project(2, 4)
    pool_mix(pooled01, 0)
    query_finish(qp0, 0)
    query_finish(qp1, 1)
    qp2 = query_project(2)
    qp3 = query_project(3)
    pooled23 = pool_windows(u23, 2, 4)
    pool_mix(pooled23, 2)
    vt_all = lax.dot_general(wuvt_ref[...], ckvn, (((1,), (1,)), ((), ())), preferred_element_type=F32)
    query_finish(qp2, 2)
    query_finish(qp3, 3)
    tk = vt_ref.shape[4]
    for h in range(nheads):
        for c in range(tm // tk):
            vt_ref[0, h, c, 0:V_DIM, :] = vt_all[h * V_DIM:(h + 1) * V_DIM, c * tk:(c + 1) * tk].astype(BF16)
            vt_ref[0, h, c, V_DIM:VT_ROWS, :] = jnp.ones((VT_ROWS - V_DIM, tk), BF16)


def _even_in_call(xf, g, win, qag, kvag, wuq, wuk, wuvt, gains, poolw, pscale, tab, e, *, B, S, tm, tk):
    T, D = xf.shape
    lora = qag.shape[1]
    pool_width = pscale.shape[1]
    n_s = S // tm
    H = MLA_HEADS
    kv_per_tile = tm // tk
    tok = lambda b, s: (b * n_s + s, 0)
    const2 = lambda b, s: (0, 0)
    layer3 = lambda b, s: (e, 0, 0)
    kern = functools.partial(_even_in_kernel, tm=tm, lora=lora, pool_width=pool_width,
                             scale=QK_DIM ** -0.5 * LOG2_E)
    return pl.pallas_call(
        kern,
        grid=(B, n_s),
        in_specs=[
            pl.BlockSpec((tm, D), tok),
            _resident((1, D), const2),
            _resident((None,) + win.shape[1:], layer3),
            _resident((1, lora), const2),
            _resident((1, lora), const2),
            _resident((None,) + wuq.shape[1:], layer3),
            _resident((None,) + wuk.shape[1:], layer3),
            _resident((None,) + wuvt.shape[1:], layer3),
            _resident((None,) + gains.shape[1:], layer3),
            _resident((None,) + poolw.shape[1:], lambda b, s: (e, 0, 0, 0)),
            _resident((1, pool_width), const2),
            pl.BlockSpec((tm, 2 * LANES), tok),
        ],
        out_specs=[
            pl.BlockSpec((1, H, tm, QK_DIM), lambda b, s: (b, 0, s, 0)),
            pl.BlockSpec((1, H, tm, QK_DIM), lambda b, s: (b, 0, s, 0)),
            pl.BlockSpec((1, H, kv_per_tile, VT_ROWS, tk), lambda b, s: (b, 0, s, 0, 0)),
            pl.BlockSpec((tm, pool_width), tok),
        ],
        out_shape=[
            jax.ShapeDtypeStruct((B, H, S, QK_DIM), BF16),
            jax.ShapeDtypeStruct((B, H, S, QK_DIM), BF16),
            jax.ShapeDtypeStruct((B, H, S // tk, VT_ROWS, tk), BF16),
            jax.ShapeDtypeStruct((T, pool_width), BF16),
        ],
        scratch_shapes=[pltpu.VMEM((POOL_HISTORY, pool_width), F32)],
        compiler_params=_params("arbitrary", "arbitrary"),
        name="even_in",
    )(xf, g, win, qag, kvag, wuq, wuk, wuvt, gains, poolw, pscale, tab)


ATTN_HEADS_PER_STEP = 4
ATTN_SLOTS = 2
ATTN_SCRATCH_PER_HEAD = 2 + 4 * ATTN_SLOTS


def _attn_kernel(q_ref, k_ref, vt_ref, o_ref, *scratch, tq, tk):
    i = pl.program_id(2)
    heads = range(ATTN_HEADS_PER_STEP)
    ns = ATTN_SLOTS
    assert tq == ns * tk
    per_head = [scratch[h * ATTN_SCRATCH_PER_HEAD:(h + 1) * ATTN_SCRATCH_PER_HEAD] for h in heads]
    m_refs = [r[0] for r in per_head]
    acc_refs = [r[1] for r in per_head]
    s_refs = [r[2:2 + ns] for r in per_head]
    p_refs = [r[2 + ns:2 + 2 * ns] for r in per_head]
    a_refs = [r[2 + 2 * ns:2 + 3 * ns] for r in per_head]
    bm_refs = [r[2 + 3 * ns:2 + 4 * ns] for r in per_head]

    def scores(h, j, slot):
        start = pl.multiple_of(j * tk, tk)
        kb = k_ref[0, h, pl.ds(start, tk), :]
        s = lax.dot_general(kb, q_ref[0, h], (((1,), (1,)), ((), ())), preferred_element_type=F32)
        s_refs[h][slot][...] = s
        bm_refs[h][slot][...] = jnp.max(s, axis=0, keepdims=True)

    def softmax(h, slot, diag_offset):
        s = s_refs[h][slot][...]
        if diag_offset is not None:
            key = lax.broadcasted_iota(jnp.int32, s.shape, 0) + diag_offset
            qry = lax.broadcasted_iota(jnp.int32, s.shape, 1)
            s = jnp.where(key <= qry, s, -1e30)
            block_max = jnp.max(s, axis=0, keepdims=True)
        else:
            block_max = bm_refs[h][slot][...]
        m_prev = m_refs[h][...]
        m_new = jnp.maximum(m_prev, block_max)
        m_refs[h][...] = m_new
        a_refs[h][slot][...] = jnp.exp2(m_prev - m_new)
        p_refs[h][slot][...] = jnp.exp2(s - m_new).astype(BF16)

    def accumulate(h, j, slot):
        acc_refs[h][...] = (a_refs[h][slot][...] * acc_refs[h][...]
                            + _dot(vt_ref[0, h, j], p_refs[h][slot][...]))

    def step(j, slot, diag_offset=None):
        for h in heads:
            scores(h, j + 1, (slot + 1) % ns)
            accumulate(h, jnp.maximum(j - 1, 0), (slot - 1) % ns)
            softmax(h, slot, diag_offset)

    def finish(j, slot):
        for h in heads:
            accumulate(h, j, slot)
            softmax(h, 1 - slot, tk)
            accumulate(h, j + 1, 1 - slot)
            acc = acc_refs[h][...]
            out_t = acc[0:V_DIM] / acc[V_DIM:V_DIM + 1]
            o_ref[0, :, h * V_DIM:(h + 1) * V_DIM] = out_t.T.astype(BF16)

    for h in heads:
        m_refs[h][...] = jnp.full(m_refs[h].shape, -jnp.inf, F32)
        acc_refs[h][...] = jnp.zeros(acc_refs[h].shape, F32)
        scores(h, 0, 0)
        p_refs[h][ns - 1][...] = jnp.zeros(p_refs[h][ns - 1].shape, BF16)
        a_refs[h][ns - 1][...] = jnp.ones(a_refs[h][ns - 1].shape, F32)

    def group(g, carry):
        for slot in range(ns):
            step(ns * g + slot, slot)
        return carry

    lax.fori_loop(0, i, group, 0)
    step(ns * i, 0, diag_offset=0)
    finish(ns * i, 0)


def _attn_call(q, k, vt, *, tq):
    B, H, S, _ = q.shape
    n_kv, tk = vt.shape[2], vt.shape[4]
    hps = ATTN_HEADS_PER_STEP
    head_scratch = ([pltpu.VMEM((1, tq), F32), pltpu.VMEM((VT_ROWS, tq), F32)]
                    + [pltpu.VMEM((tk, tq), F32)] * ATTN_SLOTS
                    + [pltpu.VMEM((tk, tq), BF16)] * ATTN_SLOTS
                    + [pltpu.VMEM((1, tq), F32)] * (2 * ATTN_SLOTS))
    assert len(head_scratch) == ATTN_SCRATCH_PER_HEAD
    return pl.pallas_call(
        functools.partial(_attn_kernel, tq=tq, tk=tk),
        grid=(B, H // hps, S // tq),
        in_specs=[
            pl.BlockSpec((1, hps, tq, QK_DIM), lambda b, h, i: (b, h, i, 0)),
            pl.BlockSpec((1, hps, S, QK_DIM), lambda b, h, i: (b, h, 0, 0)),
            pl.BlockSpec((1, hps, n_kv, VT_ROWS, tk), lambda b, h, i: (b, h, 0, 0, 0)),
        ],
        out_specs=pl.BlockSpec((1, tq, hps * V_DIM), lambda b, h, i: (b, i, h)),
        out_shape=jax.ShapeDtypeStruct((B, S, H * V_DIM), BF16),
        scratch_shapes=head_scratch * hps,
        compiler_params=_params("parallel", "parallel", "arbitrary"),
        name="attn",
    )(q, k, vt)


def _outproj_kernel(x_ref, a_ref, b_ref, wa_ref, wb_ref, o_ref):
    o_ref[...] = x_ref[...] + _dot(a_ref[...], wa_ref[...]) + _dot(b_ref[...], wb_ref[...])


def _outproj_call(xf, a, b, wout, e, *, tm):
    T, D = xf.shape
    wa_rows = a.shape[1]
    wb_rows = b.shape[1]
    assert wa_rows == wb_rows
    return pl.pallas_call(
        _outproj_kernel,
        grid=(T // tm,),
        in_specs=[
            pl.BlockSpec((tm, D), lambda i: (i, 0)),
            pl.BlockSpec((tm, wa_rows), lambda i: (i, 0)),
            pl.BlockSpec((tm, wb_rows), lambda i: (i, 0)),
            _resident((None, wa_rows, D), lambda i: (e, 0, 0)),
            _resident((None, wb_rows, D), lambda i: (e, 1, 0)),
        ],
        out_specs=pl.BlockSpec((tm, D), lambda i: (i, 0)),
        out_shape=jax.ShapeDtypeStruct((T, D), F32),
        compiler_params=_params("parallel"),
        name="outproj",
    )(xf, a, b, wout, wout)


def _conv_kernel(x_ref, g_ref, wb_ref, wc_ref, wu_ref, cw_ref, wo_ref, o_ref, r_ref, carry_ref,
                 *, tm, tiles_per_seq):
    i = pl.program_id(0)
    c = pl.program_id(1)
    first = c == 0

    @pl.when(first)
    def _():
        _store_inv_rms(x_ref, r_ref)

    x = x_ref[...]
    hn = _normed_operand(x, r_ref, g_ref)
    cw = cw_ref[...]
    first_tile = i % tiles_per_seq == 0
    tc = wo_ref.shape[0]
    gated = []
    for lo in range(0, tc, MXU_TILE):
        cols = slice(lo, lo + MXU_TILE)
        gate_b = _dot(hn, wb_ref[:, cols])
        v = _dot(hn, wc_ref[:, cols]) * _dot(hn, wu_ref[:, cols])
        prev = jnp.where(first_tile, 0.0, carry_ref[c, :, cols])
        carry_ref[c, :, cols] = v[tm - CONV_HISTORY:tm]
        ext = jnp.concatenate([prev, v], axis=0)
        conv = cw[CONV_WIDTH - 1:CONV_WIDTH, cols] * v
        for back in range(1, CONV_WIDTH):
            tap = CONV_WIDTH - 1 - back
            conv = conv + cw[tap:tap + 1, cols] * pltpu.roll(ext, back, axis=0)[CONV_HISTORY:]
        gated.append((gate_b * conv).astype(BF16))
    o_ref[...] = jnp.where(first, x, o_ref[...]) + _dot(jnp.concatenate(gated, axis=1), wo_ref[...])


def _conv_call(xf, g, win, convw, wout, o, *, S, tm, tc):
    T, D = xf.shape
    C = wout.shape[1]
    n_c = C // tc
    kern = functools.partial(_conv_kernel, tm=tm, tiles_per_seq=S // tm)
    return pl.pallas_call(
        kern,
        grid=(T // tm, n_c),
        in_specs=[
            pl.BlockSpec((tm, D), lambda i, c: (i, 0)),
            pl.BlockSpec((1, D), lambda i, c: (0, 0)),
            pl.BlockSpec((None, D, tc), lambda i, c: (o, 0, c)),
            pl.BlockSpec((None, D, tc), lambda i, c: (o, 0, n_c + c)),
            pl.BlockSpec((None, D, tc), lambda i, c: (o, 0, 2 * n_c + c)),
            pl.BlockSpec((None, CONV_WIDTH, tc), lambda i, c: (o, 0, c)),
            pl.BlockSpec((None, tc, D), lambda i, c: (o, c, 0)),
        ],
        out_specs=pl.BlockSpec((tm, D), lambda i, c: (i, 0)),
        out_shape=jax.ShapeDtypeStruct((T, D), F32),
        scratch_shapes=[pltpu.VMEM((tm, LANES), F32),
                        pltpu.VMEM((n_c, CONV_HISTORY, tc), F32)],
        compiler_params=_params("arbitrary", "arbitrary"),
        name="conv_mixer",
    )(xf, g, win, win, win, convw, wout)


def _mlp_kernel(x_ref, g_ref, wup_ref, wdn_ref, o_ref, r_ref):
    first = pl.program_id(1) == 0

    @pl.when(first)
    def _():
        _store_inv_rms(x_ref, r_ref)

    x = x_ref[...]
    up = _dot(_normed_operand(x, r_ref, g_ref), wup_ref[...])
    act = jnp.square(jnp.maximum(up, 0.0)).astype(BF16)
    o_ref[...] = jnp.where(first, x, o_ref[...]) + _dot(act, wdn_ref[...])


def _mlp_call(xf, g, wup, wdn, layer, *, tm, tf):
    T, D = xf.shape
    F = wup.shape[2]
    return pl.pallas_call(
        _mlp_kernel,
        grid=(T // tm, F // tf),
        in_specs=[
            pl.BlockSpec((tm, D), lambda i, f: (i, 0)),
            pl.BlockSpec((1, D), lambda i, f: (0, 0)),
            pl.BlockSpec((None, D, tf), lambda i, f: (layer, 0, f)),
            pl.BlockSpec((None, tf, D), lambda i, f: (layer, f, 0)),
        ],
        out_specs=pl.BlockSpec((tm, D), lambda i, f: (i, 0)),
        out_shape=jax.ShapeDtypeStruct((T, D), F32),
        scratch_shapes=[pltpu.VMEM((tm, LANES), F32)],
        compiler_params=_params("parallel", "arbitrary"),
        name="mlp",
    )(xf, g, wup, wdn)


def _swap_halves(a):
    half = a.shape[-1] // 2
    return jnp.concatenate([a[..., half:], a[..., :half]], axis=-1)


def _prep_even_weights(even_w_in, even_w_uq, even_w_ukv, even_q_norm_g, even_k_norm_g, lora, pool_width):
    H = MLA_HEADS
    E = even_w_in.shape[0]
    off_kr = 2 * lora
    off_pool = off_kr + ROPE_DIM
    w = even_w_in.astype(BF16)
    kr = w[:, :, off_kr:off_pool]
    krs = _swap_halves(kr)
    win = jnp.concatenate([w[:, :, :off_kr], w[:, :, off_pool:off_pool + pool_width], kr, kr, krs, krs], axis=-1)

    wq = even_w_uq.astype(BF16).reshape(E, lora, H, QK_DIM)
    rope = wq[..., NOPE_DIM:]
    by_pair = lambda a: a.reshape(E, lora, H // 2, 2 * a.shape[-1])
    wuq = jnp.concatenate([by_pair(wq[..., :NOPE_DIM]), by_pair(rope), by_pair(_swap_halves(rope))], axis=-1)
    wuq = wuq.reshape(E, lora, -1)

    wkv = even_w_ukv.astype(BF16).reshape(E, even_w_ukv.shape[1], H, NOPE_DIM + V_DIM)
    wuk = wkv[..., :NOPE_DIM].reshape(E, wkv.shape[1], H * NOPE_DIM)
    wuvt = jnp.swapaxes(wkv[..., NOPE_DIM:].reshape(E, wkv.shape[1], H * V_DIM), 1, 2)

    def gain_rows(gvec):
        rope = gvec[:, NOPE_DIM:]
        ropes = _swap_halves(rope)
        return [gvec[:, :NOPE_DIM], jnp.concatenate([rope, rope], -1), jnp.concatenate([ropes, ropes], -1)]

    rows = gain_rows(even_q_norm_g) + gain_rows(even_k_norm_g)
    rows += [jnp.zeros_like(rows[0])] * 2
    gains = jnp.stack(rows, axis=1).astype(F32)
    return win, wuq, wuk, wuvt, gains


def _pick_tile(n, want):
    t = min(n, want)
    while n % t:
        t //= 2
    return t


def kernel(x, positions, mix_norm_g, mlp_norm_g, w_mlp_up, w_mlp_down, even_w_in, even_q_a_norm_g,
           even_kv_a_norm_g, even_w_uq, even_w_ukv, even_q_norm_g, even_k_norm_g, even_pool_w,
           even_pool_scale, even_w_out, odd_w_in, odd_conv_w, odd_w_out):
    B, S, D = x.shape
    T = B * S
    depth = mix_norm_g.shape[0]
    lora = even_q_a_norm_g.shape[1]
    pool_width = even_pool_scale.shape[1]

    tm_mlp = _pick_tile(S, 512)
    tf_mlp = _pick_tile(w_mlp_up.shape[2], 2048)
    tm_even = _pick_tile(S, 512)
    tq = _pick_tile(S, 512)
    tc_conv = _pick_tile(odd_w_out.shape[1], 1024)

    wup = w_mlp_up.astype(BF16)
    wdn = w_mlp_down.astype(BF16)
    win_e, wuq, wuk, wuvt, gains = _prep_even_weights(even_w_in, even_w_uq, even_w_ukv, even_q_norm_g,
                                                      even_k_norm_g, lora, pool_width)
    poolw = even_pool_w.astype(BF16)
    wout_e = even_w_out.astype(BF16)
    win_o = odd_w_in.astype(BF16)
    wout_o = odd_w_out.astype(BF16)

    tab = _rope_table_call(positions.reshape(T, 1), _pick_tile(T, 1024))

    xf = x.reshape(T, D)
    for layer in range(depth):
        g_mix = mix_norm_g[layer][None, :]
        if layer % 2 == 0:
            e = layer // 2
            q, k, vt, b = _even_in_call(
                xf, g_mix, win_e, even_q_a_norm_g[e][None, :], even_kv_a_norm_g[e][None, :],
                wuq, wuk, wuvt, gains, poolw, even_pool_scale[e][None, :], tab, e,
                B=B, S=S, tm=tm_even, tk=tq // ATTN_SLOTS)
            a = _attn_call(q, k, vt, tq=tq).reshape(T, MLA_HEADS * V_DIM)
            xf = _outproj_call(xf, a, b, wout_e, e, tm=tm_mlp)
        else:
            o = layer // 2
            xf = _conv_call(xf, g_mix, win_o, odd_conv_w, wout_o, o, S=S, tm=tm_mlp, tc=tc_conv)
        xf = _mlp_call(xf, mlp_norm_g[layer][None, :], wup, wdn, layer, tm=tm_mlp, tf=tf_mlp)
    return xf.reshape(B, S, D)
```

```python
import functools

import jax
import jax.numpy as jnp
from jax import lax
from jax.experimental import pallas as pl
from jax.experimental.pallas import tpu as pltpu

F32 = jnp.float32
BF16 = jnp.bfloat16

RMS_EPS = 1e-6
ROPE_THETA = 10000.0
MLA_HEADS = 8
NOPE_DIM = 128
ROPE_DIM = 64
QK_DIM = NOPE_DIM + ROPE_DIM
V_DIM = 128
VT_ROWS = V_DIM + 16
LOG2_E = 1.4426950408889634
POOL_WINDOWS = (2, 4, 8, 16)
POOL_HISTORY = 16
CONV_WIDTH = 3
CONV_HISTORY = 8
LANES = 128
MXU_TILE = 256

V7X_VMEM_BYTES = 64 * 1024 * 1024
VMEM_LIMIT_BYTES = V7X_VMEM_BYTES - 4 * 1024 * 1024


def _rms(xf, g):
    ms = jnp.mean(xf * xf, axis=-1, keepdims=True)
    return xf * lax.rsqrt(ms + RMS_EPS) * g


def _dot(a, b):
    return jnp.dot(a, b, preferred_element_type=F32)


def _store_inv_rms(x_ref, r_ref):
    x = x_ref[...]
    r = lax.rsqrt(jnp.mean(x * x, axis=-1, keepdims=True) + RMS_EPS)
    r_ref[...] = jnp.broadcast_to(r, r_ref.shape)


def _normed_operand(x, r_ref, g_ref):
    return (x * jnp.tile(r_ref[...], (1, x.shape[1] // LANES)) * g_ref[...]).astype(BF16)


def _params(*sem):
    return pltpu.CompilerParams(dimension_semantics=sem, vmem_limit_bytes=VMEM_LIMIT_BYTES)


def _resident(block_shape, index_map):
    return pl.BlockSpec(block_shape, index_map, pipeline_mode=pl.Buffered(1))


def _rope_table_kernel(pos_ref, freq_ref, tab_ref):
    ang = pos_ref[...].astype(F32) * freq_ref[...]
    lane = lax.broadcasted_iota(jnp.int32, ang.shape, 1)
    sign = jnp.where((lane % ROPE_DIM) < ROPE_DIM // 2, -1.0, 1.0)
    tab_ref[:, :LANES] = jnp.cos(ang)
    tab_ref[:, LANES:] = jnp.sin(ang) * sign


def _rope_table_call(pos_col, tm):
    T = pos_col.shape[0]
    inv_freq = 1.0 / (ROPE_THETA ** (jnp.arange(0, ROPE_DIM, 2, dtype=F32) / ROPE_DIM))
    freq = jnp.tile(inv_freq, LANES // (ROPE_DIM // 2))[None, :]
    return pl.pallas_call(
        _rope_table_kernel,
        grid=(T // tm,),
        in_specs=[pl.BlockSpec((tm, 1), lambda i: (i, 0)),
                  pl.BlockSpec((1, LANES), lambda i: (0, 0))],
        out_specs=pl.BlockSpec((tm, 2 * LANES), lambda i: (i, 0)),
        out_shape=jax.ShapeDtypeStruct((T, 2 * LANES), F32),
        compiler_params=_params("parallel"),
        name="rope_table",
    )(pos_col, freq)


def _even_in_kernel(x_ref, g_ref, win_ref, qag_ref, kvag_ref, wuq_ref, wuk_ref, wuvt_ref, gains_ref,
                    poolw_ref, pscale_ref, tab_ref,
                    q_ref, k_ref, vt_ref, b_ref, carry_ref, *, tm, lora, pool_width, scale):
    s = pl.program_id(1)
    hn = _rms(x_ref[...], g_ref[...]).astype(BF16)
    off_pool = 2 * lora
    off_kr = off_pool + pool_width
    ckvn = _rms(_dot(hn, win_ref[:, lora:2 * lora]), kvag_ref[...]).astype(BF16)
    kr_both = _dot(hn, win_ref[:, off_kr:off_kr + 2 * LANES])
    kr2 = kr_both[:, :LANES]
    krs2 = kr_both[:, LANES:]
    cqn = _rms(_dot(hn, win_ref[:, :lora]), qag_ref[...]).astype(BF16)
    knall = _dot(ckvn, wuk_ref[...])

    cos2 = tab_ref[:, :LANES]
    sin2 = tab_ref[:, LANES:]
    gains = gains_ref[...]
    qg_nope, qg_rope2, qg_ropes2 = gains[0:1], gains[1:2], gains[2:3]
    kg_nope, kg_rope2, kg_ropes2 = gains[3:4], gains[4:5], gains[5:6]
    lane = lax.broadcasted_iota(jnp.int32, (tm, LANES), 1)
    low_half = lane < ROPE_DIM
    nheads = MLA_HEADS

    kr_sq = jnp.where(low_half, kr2 * kr2, 0.0)
    k_roped2 = kr2 * kg_rope2 * cos2 + krs2 * kg_ropes2 * sin2
    for h in range(nheads):
        kn = knall[:, h * NOPE_DIM:(h + 1) * NOPE_DIM]
        ss = jnp.sum(kn * kn + kr_sq, axis=-1, keepdims=True)
        r = lax.rsqrt(ss * (1.0 / QK_DIM) + RMS_EPS)
        k_ref[0, h, :, 0:NOPE_DIM] = (kn * r * kg_nope).astype(BF16)
        k_ref[0, h, :, NOPE_DIM:QK_DIM] = (k_roped2 * r)[:, :ROPE_DIM].astype(BF16)

    t_in_seq = s * tm + lax.broadcasted_iota(jnp.int32, (tm, 1), 0)
    gd = pool_width // len(POOL_WINDOWS)
    pscale = pscale_ref[...]
    pair_w = 2 * NOPE_DIM + 2 * LANES

    def pool_project(g0, g1):
        return _dot(hn, win_ref[:, off_pool + g0 * gd:off_pool + g1 * gd])

    def pool_windows(u_all, g0, g1):
        pooled = []
        for gi in range(g0, g1):
            w = POOL_WINDOWS[gi]
            cols = slice(gi * gd, (gi + 1) * gd)
            u = u_all[:, (gi - g0) * gd:(gi - g0 + 1) * gd]
            prev = jnp.where(s == 0, 0.0, carry_ref[:, cols])
            carry_ref[:, cols] = u[tm - POOL_HISTORY:tm]
            e = jnp.concatenate([prev, u], axis=0)
            sh = 1
            while sh < w:
                e = e + pltpu.roll(e, sh, axis=0)
                sh *= 2
            inv_cnt = 1.0 / jnp.minimum(t_in_seq + 1, w).astype(F32)
            pooled.append((e[POOL_HISTORY:] * inv_cnt - u).astype(BF16))
        return pooled

    def pool_mix(pooled, g0):
        for k, pg in enumerate(pooled):
            cols = slice((g0 + k) * gd, (g0 + k + 1) * gd)
            b_ref[:, cols] = (_dot(pg, poolw_ref[g0 + k]) * pscale[:, cols]).astype(BF16)

    def query_project(p):
        return _dot(cqn, wuq_ref[:, p * pair_w:(p + 1) * pair_w])

    def query_finish(qp, p):
        qr2 = qp[:, 2 * NOPE_DIM:2 * NOPE_DIM + LANES]
        qrs2 = qp[:, 2 * NOPE_DIM + LANES:]
        sq = qr2 * qr2
        roped = qr2 * qg_rope2 * cos2 + qrs2 * qg_ropes2 * sin2
        for e in range(2):
            h = 2 * p + e
            qn = qp[:, e * NOPE_DIM:(e + 1) * NOPE_DIM]
            mine = low_half if e == 0 else jnp.logical_not(low_half)
            ss = jnp.sum(qn * qn + jnp.where(mine, sq, 0.0), axis=-1, keepdims=True)
            r = lax.rsqrt(ss * (1.0 / QK_DIM) + RMS_EPS) * scale
            q_ref[0, h, :, 0:NOPE_DIM] = (qn * r * qg_nope).astype(BF16)
            rp = roped * r
            if e == 1:
                rp = pltpu.roll(rp, ROPE_DIM, axis=1)
            q_ref[0, h, :, NOPE_DIM:QK_DIM] = rp[:, :ROPE_DIM].astype(BF16)

    assert len(POOL_WINDOWS) == 4 and nheads == 8
    u01 = pool_project(0, 2)
    qp0 = query_project(0)
    qp1 = query_project(1)
    pooled01 = pool_windows(u01, 0, 2)
    u23 = pool_project(2, 4)
    pool_mix(pooled01, 0)
    query_finish(qp0, 0)
    query_finish(qp1, 1)
    qp2 = query_project(2)
    qp3 = query_project(3)
    pooled23 = pool_windows(u23, 2, 4)
    pool_mix(pooled23, 2)
    vt_all = lax.dot_general(wuvt_ref[...], ckvn, (((1,), (1,)), ((), ())), preferred_element_type=F32)
    query_finish(qp2, 2)
    query_finish(qp3, 3)
    for h in range(nheads):
        vt_ref[0, h, 0, 0:V_DIM, :] = vt_all[h * V_DIM:(h + 1) * V_DIM].astype(BF16)
        vt_ref[0, h, 0, V_DIM:VT_ROWS, :] = jnp.ones((VT_ROWS - V_DIM, tm), BF16)


def _even_in_call(xf, g, win, qag, kvag, wuq, wuk, wuvt, gains, poolw, pscale, tab, e, *, B, S, tm, tk):
    T, D = xf.shape
    lora = qag.shape[1]
    pool_width = pscale.shape[1]
    n_s = S // tm
    H = MLA_HEADS
    per_kv_block = tk // tm
    tok = lambda b, s: (b * n_s + s, 0)
    const2 = lambda b, s: (0, 0)
    layer3 = lambda b, s: (e, 0, 0)
    kern = functools.partial(_even_in_kernel, tm=tm, lora=lora, pool_width=pool_width,
                             scale=QK_DIM ** -0.5 * LOG2_E)
    return pl.pallas_call(
        kern,
        grid=(B, n_s),
        in_specs=[
            pl.BlockSpec((tm, D), tok),
            _resident((1, D), const2),
            _resident((None,) + win.shape[1:], layer3),
            _resident((1, lora), const2),
            _resident((1, lora), const2),
            _resident((None,) + wuq.shape[1:], layer3),
            _resident((None,) + wuk.shape[1:], layer3),
            _resident((None,) + wuvt.shape[1:], layer3),
            _resident((None,) + gains.shape[1:], layer3),
            _resident((None,) + poolw.shape[1:], lambda b, s: (e, 0, 0, 0)),
            _resident((1, pool_width), const2),
            pl.BlockSpec((tm, 2 * LANES), tok),
        ],
        out_specs=[
            pl.BlockSpec((1, H, tm, QK_DIM), lambda b, s: (b, 0, s, 0)),
            pl.BlockSpec((1, H, tm, QK_DIM), lambda b, s: (b, 0, s, 0)),
            pl.BlockSpec((1, H, 1, VT_ROWS, tm),
                         lambda b, s: (b, 0, s // per_kv_block, 0, s % per_kv_block)),
            pl.BlockSpec((tm, pool_width), tok),
        ],
        out_shape=[
            jax.ShapeDtypeStruct((B, H, S, QK_DIM), BF16),
            jax.ShapeDtypeStruct((B, H, S, QK_DIM), BF16),
            jax.ShapeDtypeStruct((B, H, S // tk, VT_ROWS, tk), BF16),
            jax.ShapeDtypeStruct((T, pool_width), BF16),
        ],
        scratch_shapes=[pltpu.VMEM((POOL_HISTORY, pool_width), F32)],
        compiler_params=_params("arbitrary", "arbitrary"),
        name="even_in",
    )(xf, g, win, qag, kvag, wuq, wuk, wuvt, gains, poolw, pscale, tab)


ATTN_HEADS_PER_STEP = 4
ATTN_SLOTS = 2
ATTN_SCRATCH_PER_HEAD = 2 + 4 * ATTN_SLOTS


def _attn_kernel(q_ref, k_ref, vt_ref, o_ref, *scratch, tq):
    i = pl.program_id(2)
    heads = range(ATTN_HEADS_PER_STEP)
    ns = ATTN_SLOTS
    per_head = [scratch[h * ATTN_SCRATCH_PER_HEAD:(h + 1) * ATTN_SCRATCH_PER_HEAD] for h in heads]
    m_refs = [r[0] for r in per_head]
    acc_refs = [r[1] for r in per_head]
    s_refs = [r[2:2 + ns] for r in per_head]
    p_refs = [r[2 + ns:2 + 2 * ns] for r in per_head]
    a_refs = [r[2 + 2 * ns:2 + 3 * ns] for r in per_head]
    bm_refs = [r[2 + 3 * ns:2 + 4 * ns] for r in per_head]

    def scores(h, j, slot):
        start = pl.multiple_of(j * tq, tq)
        kb = k_ref[0, h, pl.ds(start, tq), :]
        s = lax.dot_general(kb, q_ref[0, h], (((1,), (1,)), ((), ())), preferred_element_type=F32)
        s_refs[h][slot][...] = s
        bm_refs[h][slot][...] = jnp.max(s, axis=0, keepdims=True)

    def softmax(h, slot, masked):
        s = s_refs[h][slot][...]
        if masked:
            key = lax.broadcasted_iota(jnp.int32, s.shape, 0)
            qry = lax.broadcasted_iota(jnp.int32, s.shape, 1)
            s = jnp.where(key <= qry, s, -1e30)
            block_max = jnp.max(s, axis=0, keepdims=True)
        else:
            block_max = bm_refs[h][slot][...]
        m_prev = m_refs[h][...]
        m_new = jnp.maximum(m_prev, block_max)
        m_refs[h][...] = m_new
        a_refs[h][slot][...] = jnp.exp2(m_prev - m_new)
        p_refs[h][slot][...] = jnp.exp2(s - m_new).astype(BF16)

    def accumulate(h, j, slot):
        acc_refs[h][...] = (a_refs[h][slot][...] * acc_refs[h][...]
                            + _dot(vt_ref[0, h, j], p_refs[h][slot][...]))

    def step(j, slot):
        for h in heads:
            scores(h, j + 1, (slot + 1) % ns)
            accumulate(h, jnp.maximum(j - 1, 0), (slot - 1) % ns)
            softmax(h, slot, False)

    def last(slot):
        for h in heads:
            accumulate(h, jnp.maximum(i - 1, 0), (slot - 1) % ns)
            softmax(h, slot, True)
            accumulate(h, i, slot)
            acc = acc_refs[h][...]
            out_t = acc[0:V_DIM] / acc[V_DIM:V_DIM + 1]
            o_ref[0, :, h * V_DIM:(h + 1) * V_DIM] = out_t.T.astype(BF16)

    for h in heads:
        m_refs[h][...] = jnp.full(m_refs[h].shape, -jnp.inf, F32)
        acc_refs[h][...] = jnp.zeros(acc_refs[h].shape, F32)
        scores(h, 0, 0)
        p_refs[h][ns - 1][...] = jnp.zeros(p_refs[h][ns - 1].shape, BF16)
        a_refs[h][ns - 1][...] = jnp.ones(a_refs[h][ns - 1].shape, F32)

    def group(g, carry):
        for slot in range(ns):
            step(ns * g + slot, slot)
        return carry

    lax.fori_loop(0, i // ns, group, 0)

    base = (i // ns) * ns
    for rem in range(ns):
        @pl.when(i % ns == rem)
        def _(rem=rem):
            for slot in range(rem):
                step(base + slot, slot)
            last(rem)


def _attn_call(q, k, vt, *, tq):
    B, H, S, _ = q.shape
    n_kv = vt.shape[2]
    assert vt.shape[4] == tq
    hps = ATTN_HEADS_PER_STEP
    head_scratch = ([pltpu.VMEM((1, tq), F32), pltpu.VMEM((VT_ROWS, tq), F32)]
                    + [pltpu.VMEM((tq, tq), F32)] * ATTN_SLOTS
                    + [pltpu.VMEM((tq, tq), BF16)] * ATTN_SLOTS
                    + [pltpu.VMEM((1, tq), F32)] * (2 * ATTN_SLOTS))
    assert len(head_scratch) == ATTN_SCRATCH_PER_HEAD
    return pl.pallas_call(
        functools.partial(_attn_kernel, tq=tq),
        grid=(B, H // hps, S // tq),
        in_specs=[
            pl.BlockSpec((1, hps, tq, QK_DIM), lambda b, h, i: (b, h, i, 0)),
            pl.BlockSpec((1, hps, S, QK_DIM), lambda b, h, i: (b, h, 0, 0)),
            pl.BlockSpec((1, hps, n_kv, VT_ROWS, tq), lambda b, h, i: (b, h, 0, 0, 0)),
        ],
        out_specs=pl.BlockSpec((1, tq, hps * V_DIM), lambda b, h, i: (b, i, h)),
        out_shape=jax.ShapeDtypeStruct((B, S, H * V_DIM), BF16),
        scratch_shapes=head_scratch * hps,
        compiler_params=_params("parallel", "parallel", "arbitrary"),
        name="attn",
    )(q, k, vt)


def _outproj_kernel(x_ref, a_ref, b_ref, wa_ref, wb_ref, o_ref):
    o_ref[...] = x_ref[...] + _dot(a_ref[...], wa_ref[...]) + _dot(b_ref[...], wb_ref[...])


def _outproj_call(xf, a, b, wout, e, *, tm):
    T, D = xf.shape
    wa_rows = a.shape[1]
    wb_rows = b.shape[1]
    assert wa_rows == wb_rows
    return pl.pallas_call(
        _outproj_kernel,
        grid=(T // tm,),
        in_specs=[
            pl.BlockSpec((tm, D), lambda i: (i, 0)),
            pl.BlockSpec((tm, wa_rows), lambda i: (i, 0)),
            pl.BlockSpec((tm, wb_rows), lambda i: (i, 0)),
            _resident((None, wa_rows, D), lambda i: (e, 0, 0)),
            _resident((None, wb_rows, D), lambda i: (e, 1, 0)),
        ],
        out_specs=pl.BlockSpec((tm, D), lambda i: (i, 0)),
        out_shape=jax.ShapeDtypeStruct((T, D), F32),
        compiler_params=_params("parallel"),
        name="outproj",
    )(xf, a, b, wout, wout)


def _conv_kernel(x_ref, g_ref, wb_ref, wc_ref, wu_ref, cw_ref, wo_ref, o_ref, r_ref, carry_ref,
                 *, tm, tiles_per_seq):
    i = pl.program_id(0)
    c = pl.program_id(1)
    first = c == 0

    @pl.when(first)
    def _():
        _store_inv_rms(x_ref, r_ref)

    x = x_ref[...]
    hn = _normed_operand(x, r_ref, g_ref)
    cw = cw_ref[...]
    first_tile = i % tiles_per_seq == 0
    tc = wo_ref.shape[0]
    gated = []
    for lo in range(0, tc, MXU_TILE):
        cols = slice(lo, lo + MXU_TILE)
        gate_b = _dot(hn, wb_ref[:, cols])
        v = _dot(hn, wc_ref[:, cols]) * _dot(hn, wu_ref[:, cols])
        prev = jnp.where(first_tile, 0.0, carry_ref[c, :, cols])
        carry_ref[c, :, cols] = v[tm - CONV_HISTORY:tm]
        ext = jnp.concatenate([prev, v], axis=0)
        conv = cw[CONV_WIDTH - 1:CONV_WIDTH, cols] * v
        for back in range(1, CONV_WIDTH):
            tap = CONV_WIDTH - 1 - back
            conv = conv + cw[tap:tap + 1, cols] * pltpu.roll(ext, back, axis=0)[CONV_HISTORY:]
        gated.append((gate_b * conv).astype(BF16))
    o_ref[...] = jnp.where(first, x, o_ref[...]) + _dot(jnp.concatenate(gated, axis=1), wo_ref[...])


def _conv_call(xf, g, win, convw, wout, o, *, S, tm, tc):
    T, D = xf.shape
    C = wout.shape[1]
    n_c = C // tc
    kern = functools.partial(_conv_kernel, tm=tm, tiles_per_seq=S // tm)
    return pl.pallas_call(
        kern,
        grid=(T // tm, n_c),
        in_specs=[
            pl.BlockSpec((tm, D), lambda i, c: (i, 0)),
            pl.BlockSpec((1, D), lambda i, c: (0, 0)),
            pl.BlockSpec((None, D, tc), lambda i, c: (o, 0, c)),
            pl.BlockSpec((None, D, tc), lambda i, c: (o, 0, n_c + c)),
            pl.BlockSpec((None, D, tc), lambda i, c: (o, 0, 2 * n_c + c)),
            pl.BlockSpec((None, CONV_WIDTH, tc), lambda i, c: (o, 0, c)),
            pl.BlockSpec((None, tc, D), lambda i, c: (o, c, 0)),
        ],
        out_specs=pl.BlockSpec((tm, D), lambda i, c: (i, 0)),
        out_shape=jax.ShapeDtypeStruct((T, D), F32),
        scratch_shapes=[pltpu.VMEM((tm, LANES), F32),
                        pltpu.VMEM((n_c, CONV_HISTORY, tc), F32)],
        compiler_params=_params("arbitrary", "arbitrary"),
        name="conv_mixer",
    )(xf, g, win, win, win, convw, wout)


def _mlp_kernel(x_ref, g_ref, wup_ref, wdn_ref, *rest, cast_next):
    if cast_next:
        next_up_ref, next_dn_ref, o_ref, next_up_out_ref, next_dn_out_ref, r_ref = rest
        next_up_out_ref[...] = next_up_ref[...].astype(BF16)
        next_dn_out_ref[...] = next_dn_ref[...].astype(BF16)
    else:
        o_ref, r_ref = rest
    first = pl.program_id(1) == 0

    @pl.when(first)
    def _():
        _store_inv_rms(x_ref, r_ref)

    x = x_ref[...]
    up = _dot(_normed_operand(x, r_ref, g_ref), wup_ref[...])
    act = jnp.square(jnp.maximum(up, 0.0)).astype(BF16)
    o_ref[...] = jnp.where(first, x, o_ref[...]) + _dot(act, wdn_ref[...])


BF16_SUBLANES = 16


def _mlp_call(xf, g, wup, wdn, *, tm, tf, next_weights=None):
    T, D = xf.shape
    F = wup.shape[1]
    nf = F // tf
    n_steps = (T // tm) * nf
    in_specs = [
        pl.BlockSpec((tm, D), lambda i, f: (i, 0)),
        pl.BlockSpec((1, D), lambda i, f: (0, 0)),
        pl.BlockSpec((D, tf), lambda i, f: (0, f)),
        pl.BlockSpec((tf, D), lambda i, f: (f, 0)),
    ]
    out_specs = [pl.BlockSpec((tm, D), lambda i, f: (i, 0))]
    out_shape = [jax.ShapeDtypeStruct((T, D), F32)]
    args = [xf, g, wup, wdn]
    if next_weights is not None:
        w_up32, w_dn32, layer = next_weights
        for w32 in (w_up32, w_dn32):
            rows, cols = w32.shape[1:]
            block_rows = max(BF16_SUBLANES, rows // n_steps)
            n_blocks = rows // block_rows
            assert rows % block_rows == 0 and n_steps % n_blocks == 0
            block_of = lambda i, f, n_blocks=n_blocks: ((i * nf + f) * n_blocks) // n_steps
            in_specs.append(pl.BlockSpec((None, block_rows, cols),
                                         lambda i, f, block_of=block_of: (layer, block_of(i, f), 0)))
            out_specs.append(pl.BlockSpec((block_rows, cols),
                                          lambda i, f, block_of=block_of: (block_of(i, f), 0)))
            out_shape.append(jax.ShapeDtypeStruct((rows, cols), BF16))
            args.append(w32)
    outs = pl.pallas_call(
        functools.partial(_mlp_kernel, cast_next=next_weights is not None),
        grid=(T // tm, nf),
        in_specs=in_specs,
        out_specs=out_specs,
        out_shape=out_shape,
        scratch_shapes=[pltpu.VMEM((tm, LANES), F32)],
        compiler_params=_params("arbitrary", "arbitrary"),
        name="mlp",
    )(*args)
    return outs if next_weights is not None else outs[0]


def _swap_halves(a):
    half = a.shape[-1] // 2
    return jnp.concatenate([a[..., half:], a[..., :half]], axis=-1)


def _prep_even_weights(even_w_in, even_w_uq, even_w_ukv, even_q_norm_g, even_k_norm_g, lora, pool_width):
    H = MLA_HEADS
    E = even_w_in.shape[0]
    off_kr = 2 * lora
    off_pool = off_kr + ROPE_DIM
    w = even_w_in.astype(BF16)
    kr = w[:, :, off_kr:off_pool]
    krs = _swap_halves(kr)
    win = jnp.concatenate([w[:, :, :off_kr], w[:, :, off_pool:off_pool + pool_width], kr, kr, krs, krs], axis=-1)

    wq = even_w_uq.astype(BF16).reshape(E, lora, H, QK_DIM)
    rope = wq[..., NOPE_DIM:]
    by_pair = lambda a: a.reshape(E, lora, H // 2, 2 * a.shape[-1])
    wuq = jnp.concatenate([by_pair(wq[..., :NOPE_DIM]), by_pair(rope), by_pair(_swap_halves(rope))], axis=-1)
    wuq = wuq.reshape(E, lora, -1)

    wkv = even_w_ukv.astype(BF16).reshape(E, even_w_ukv.shape[1], H, NOPE_DIM + V_DIM)
    wuk = wkv[..., :NOPE_DIM].reshape(E, wkv.shape[1], H * NOPE_DIM)
    wuvt = jnp.swapaxes(wkv[..., NOPE_DIM:].reshape(E, wkv.shape[1], H * V_DIM), 1, 2)

    def gain_rows(gvec):
        rope = gvec[:, NOPE_DIM:]
        ropes = _swap_halves(rope)
        return [gvec[:, :NOPE_DIM], jnp.concatenate([rope, rope], -1), jnp.concatenate([ropes, ropes], -1)]

    rows = gain_rows(even_q_norm_g) + gain_rows(even_k_norm_g)
    rows += [jnp.zeros_like(rows[0])] * 2
    gains = jnp.stack(rows, axis=1).astype(F32)
    return win, wuq, wuk, wuvt, gains


def _pick_tile(n, want):
    t = min(n, want)
    while n % t:
        t //= 2
    return t


def kernel(x, positions, mix_norm_g, mlp_norm_g, w_mlp_up, w_mlp_down, even_w_in, even_q_a_norm_g,
           even_kv_a_norm_g, even_w_uq, even_w_ukv, even_q_norm_g, even_k_norm_g, even_pool_w,
           even_pool_scale, even_w_out, odd_w_in, odd_conv_w, odd_w_out):
    B, S, D = x.shape
    T = B * S
    depth = mix_norm_g.shape[0]
    lora = even_q_a_norm_g.shape[1]
    pool_width = even_pool_scale.shape[1]

    tm_mlp = _pick_tile(S, 512)
    tf_mlp = _pick_tile(w_mlp_up.shape[2], 2048)
    tm_even = _pick_tile(S, 512)
    tq = _pick_tile(S, 512)
    tc_conv = _pick_tile(odd_w_out.shape[1], 1024)

    wup = w_mlp_up[0].astype(BF16)
    wdn = w_mlp_down[0].astype(BF16)
    win_e, wuq, wuk, wuvt, gains = _prep_even_weights(even_w_in, even_w_uq, even_w_ukv, even_q_norm_g,
                                                      even_k_norm_g, lora, pool_width)
    poolw = even_pool_w.astype(BF16)
    wout_e = even_w_out.astype(BF16)
    win_o = odd_w_in.astype(BF16)
    wout_o = odd_w_out.astype(BF16)

    tab = _rope_table_call(positions.reshape(T, 1), _pick_tile(T, 1024))

    xf = x.reshape(T, D)
    for layer in range(depth):
        g_mix = mix_norm_g[layer][None, :]
        if layer % 2 == 0:
            e = layer // 2
            q, k, vt, b = _even_in_call(
                xf, g_mix, win_e, even_q_a_norm_g[e][None, :], even_kv_a_norm_g[e][None, :],
                wuq, wuk, wuvt, gains, poolw, even_pool_scale[e][None, :], tab, e,
                B=B, S=S, tm=tm_even, tk=tq)
            a = _attn_call(q, k, vt, tq=tq).reshape(T, MLA_HEADS * V_DIM)
            xf = _outproj_call(xf, a, b, wout_e, e, tm=tm_mlp)
        else:
            o = layer // 2
            xf = _conv_call(xf, g_mix, win_o, odd_conv_w, wout_o, o, S=S, tm=tm_mlp, tc=tc_conv)
        g_mlp = mlp_norm_g[layer][None, :]
        if layer + 1 < depth:
            xf, wup, wdn = _mlp_call(xf, g_mlp, wup, wdn, tm=tm_mlp, tf=tf_mlp,
                                     next_weights=(w_mlp_up, w_mlp_down, layer + 1))
        else:
            xf = _mlp_call(xf, g_mlp, wup, wdn, tm=tm_mlp, tf=tf_mlp)
    return xf.reshape(B, S, D)
```

```python
import functools

import jax
import jax.numpy as jnp
from jax import lax
from jax.experimental import pallas as pl
from jax.experimental.pallas import tpu as pltpu

F32 = jnp.float32
BF16 = jnp.bfloat16

RMS_EPS = 1e-6
ROPE_THETA = 10000.0
MLA_HEADS = 8
NOPE_DIM = 128
ROPE_DIM = 64
QK_DIM = NOPE_DIM + ROPE_DIM
V_DIM = 128
VT_ROWS = V_DIM + 16
LOG2_E = 1.4426950408889634
POOL_WINDOWS = (2, 4, 8, 16)
POOL_HISTORY = 16
CONV_WIDTH = 3
CONV_HISTORY = 8
LANES = 128
MXU_TILE = 256

V7X_VMEM_BYTES = 64 * 1024 * 1024
VMEM_LIMIT_BYTES = V7X_VMEM_BYTES - 4 * 1024 * 1024


def _rms(xf, g):
    ms = jnp.mean(xf * xf, axis=-1, keepdims=True)
    return xf * lax.rsqrt(ms + RMS_EPS) * g


def _dot(a, b):
    return jnp.dot(a, b, preferred_element_type=F32)


def _store_inv_rms(x_ref, r_ref):
    x = x_ref[...]
    r = lax.rsqrt(jnp.mean(x * x, axis=-1, keepdims=True) + RMS_EPS)
    r_ref[...] = jnp.broadcast_to(r, r_ref.shape)


def _normed_operand(x, r_ref, g_ref):
    return (x * jnp.tile(r_ref[...], (1, x.shape[1] // LANES)) * g_ref[...]).astype(BF16)


def _params(*sem):
    return pltpu.CompilerParams(dimension_semantics=sem, vmem_limit_bytes=VMEM_LIMIT_BYTES)


def _resident(block_shape, index_map):
    return pl.BlockSpec(block_shape, index_map, pipeline_mode=pl.Buffered(1))


def _rope_table_kernel(pos_ref, freq_ref, tab_ref):
    ang = pos_ref[...].astype(F32) * freq_ref[...]
    lane = lax.broadcasted_iota(jnp.int32, ang.shape, 1)
    sign = jnp.where((lane % ROPE_DIM) < ROPE_DIM // 2, -1.0, 1.0)
    tab_ref[:, :LANES] = jnp.cos(ang)
    tab_ref[:, LANES:] = jnp.sin(ang) * sign


def _rope_table_call(pos_col, tm):
    T = pos_col.shape[0]
    inv_freq = 1.0 / (ROPE_THETA ** (jnp.arange(0, ROPE_DIM, 2, dtype=F32) / ROPE_DIM))
    freq = jnp.tile(inv_freq, LANES // (ROPE_DIM // 2))[None, :]
    return pl.pallas_call(
        _rope_table_kernel,
        grid=(T // tm,),
        in_specs=[pl.BlockSpec((tm, 1), lambda i: (i, 0)),
                  pl.BlockSpec((1, LANES), lambda i: (0, 0))],
        out_specs=pl.BlockSpec((tm, 2 * LANES), lambda i: (i, 0)),
        out_shape=jax.ShapeDtypeStruct((T, 2 * LANES), F32),
        compiler_params=_params("parallel"),
        name="rope_table",
    )(pos_col, freq)


def _even_in_kernel(x_ref, g_ref, win_ref, qag_ref, kvag_ref, wuq_ref, wuk_ref, wuvt_ref, gains_ref,
                    poolw_ref, pscale_ref, tab_ref,
                    q_ref, k_ref, vt_ref, b_ref, carry_ref, *, tm, lora, pool_width, scale):
    s = pl.program_id(1)
    hn = _rms(x_ref[...], g_ref[...]).astype(BF16)
    off_pool = 2 * lora
    off_kr = off_pool + pool_width
    ckvn = _rms(_dot(hn, win_ref[:, lora:2 * lora]), kvag_ref[...]).astype(BF16)
    kr_both = _dot(hn, win_ref[:, off_kr:off_kr + 2 * LANES])
    kr2 = kr_both[:, :LANES]
    krs2 = kr_both[:, LANES:]
    cqn = _rms(_dot(hn, win_ref[:, :lora]), qag_ref[...]).astype(BF16)
    knall = _dot(ckvn, wuk_ref[...])

    cos2 = tab_ref[:, :LANES]
    sin2 = tab_ref[:, LANES:]
    gains = gains_ref[...]
    qg_nope, qg_rope2, qg_ropes2 = gains[0:1], gains[1:2], gains[2:3]
    kg_nope, kg_rope2, kg_ropes2 = gains[3:4], gains[4:5], gains[5:6]
    lane = lax.broadcasted_iota(jnp.int32, (tm, LANES), 1)
    low_half = lane < ROPE_DIM
    nheads = MLA_HEADS

    kr_sq = jnp.where(low_half, kr2 * kr2, 0.0)
    k_roped2 = kr2 * kg_rope2 * cos2 + krs2 * kg_ropes2 * sin2
    for h in range(nheads):
        kn = knall[:, h * NOPE_DIM:(h + 1) * NOPE_DIM]
        ss = jnp.sum(kn * kn + kr_sq, axis=-1, keepdims=True)
        r = lax.rsqrt(ss * (1.0 / QK_DIM) + RMS_EPS)
        k_ref[0, h, :, 0:NOPE_DIM] = (kn * r * kg_nope).astype(BF16)
        k_ref[0, h, :, NOPE_DIM:QK_DIM] = (k_roped2 * r)[:, :ROPE_DIM].astype(BF16)

    t_in_seq = s * tm + lax.broadcasted_iota(jnp.int32, (tm, 1), 0)
    gd = pool_width // len(POOL_WINDOWS)
    pscale = pscale_ref[...]
    pair_w = 2 * NOPE_DIM + 2 * LANES

    def pool_project(g0, g1):
        return _dot(hn, win_ref[:, off_pool + g0 * gd:off_pool + g1 * gd])

    def pool_windows(u_all, g0, g1):
        pooled = []
        for gi in range(g0, g1):
            w = POOL_WINDOWS[gi]
            cols = slice(gi * gd, (gi + 1) * gd)
            u = u_all[:, (gi - g0) * gd:(gi - g0 + 1) * gd]
            prev = jnp.where(s == 0, 0.0, carry_ref[:, cols])
            carry_ref[:, cols] = u[tm - POOL_HISTORY:tm]
            e = jnp.concatenate([prev, u], axis=0)
            sh = 1
            while sh < w:
                e = e + pltpu.roll(e, sh, axis=0)
                sh *= 2
            inv_cnt = 1.0 / jnp.minimum(t_in_seq + 1, w).astype(F32)
            pooled.append((e[POOL_HISTORY:] * inv_cnt - u).astype(BF16))
        return pooled

    def pool_mix(pooled, g0):
        for k, pg in enumerate(pooled):
            cols = slice((g0 + k) * gd, (g0 + k + 1) * gd)
            b_ref[:, cols] = (_dot(pg, poolw_ref[g0 + k]) * pscale[:, cols]).astype(BF16)

    def query_project(p):
        return _dot(cqn, wuq_ref[:, p * pair_w:(p + 1) * pair_w])

    def query_finish(qp, p):
        qr2 = qp[:, 2 * NOPE_DIM:2 * NOPE_DIM + LANES]
        qrs2 = qp[:, 2 * NOPE_DIM + LANES:]
        sq = qr2 * qr2
        roped = qr2 * qg_rope2 * cos2 + qrs2 * qg_ropes2 * sin2
        for e in range(2):
            h = 2 * p + e
            qn = qp[:, e * NOPE_DIM:(e + 1) * NOPE_DIM]
            mine = low_half if e == 0 else jnp.logical_not(low_half)
            ss = jnp.sum(qn * qn + jnp.where(mine, sq, 0.0), axis=-1, keepdims=True)
            r = lax.rsqrt(ss * (1.0 / QK_DIM) + RMS_EPS) * scale
            q_ref[0, h, :, 0:NOPE_DIM] = (qn * r * qg_nope).astype(BF16)
            rp = roped * r
            if e == 1:
                rp = pltpu.roll(rp, ROPE_DIM, axis=1)
            q_ref[0, h, :, NOPE_DIM:QK_DIM] = rp[:, :ROPE_DIM].astype(BF16)

    assert len(POOL_WINDOWS) == 4 and nheads == 8
    u01 = pool_project(0, 2)
    qp0 = query_project(0)
    qp1 = query_project(1)
    pooled01 = pool_windows(u01, 0, 2)
    u23 = pool_project(2, 4)
    pool_mix(pooled01, 0)
    query_finish(qp0, 0)
    query_finish(qp1, 1)
    qp2 = query_project(2)
    qp3 = query_project(3)
    pooled23 = pool_windows(u23, 2, 4)
    pool_mix(pooled23, 2)
    vt_all = lax.dot_general(wuvt_ref[...], ckvn, (((1,), (1,)), ((), ())), preferred_element_type=F32)
    query_finish(qp2, 2)
    query_finish(qp3, 3)
    for h in range(nheads):
        vt_ref[0, h, 0, 0:V_DIM, :] = vt_all[h * V_DIM:(h + 1) * V_DIM].astype(BF16)
        vt_ref[0, h, 0, V_DIM:VT_ROWS, :] = jnp.ones((VT_ROWS - V_DIM, tm), BF16)


def _even_in_call(xf, g, win, qag, kvag, wuq, wuk, wuvt, gains, poolw, pscale, tab, e, *, B, S, tm, tk):
    T, D = xf.shape
    lora = qag.shape[1]
    pool_width = pscale.shape[1]
    n_s = S // tm
    H = MLA_HEADS
    per_kv_block = tk // tm
    tok = lambda b, s: (b * n_s + s, 0)
    const2 = lambda b, s: (0, 0)
    layer3 = lambda b, s: (e, 0, 0)
    kern = functools.partial(_even_in_kernel, tm=tm, lora=lora, pool_width=pool_width,
                             scale=QK_DIM ** -0.5 * LOG2_E)
    return pl.pallas_call(
        kern,
        grid=(B, n_s),
        in_specs=[
            pl.BlockSpec((tm, D), tok),
            _resident((1, D), const2),
            _resident((None,) + win.shape[1:], layer3),
            _resident((1, lora), const2),
            _resident((1, lora), const2),
            _resident((None,) + wuq.shape[1:], layer3),
            _resident((None,) + wuk.shape[1:], layer3),
            _resident((None,) + wuvt.shape[1:], layer3),
            _resident((None,) + gains.shape[1:], layer3),
            _resident((None,) + poolw.shape[1:], lambda b, s: (e, 0, 0, 0)),
            _resident((1, pool_width), const2),
            pl.BlockSpec((tm, 2 * LANES), tok),
        ],
        out_specs=[
            pl.BlockSpec((1, H, tm, QK_DIM), lambda b, s: (b, 0, s, 0)),
            pl.BlockSpec((1, H, tm, QK_DIM), lambda b, s: (b, 0, s, 0)),
            pl.BlockSpec((1, H, 1, VT_ROWS, tm),
                         lambda b, s: (b, 0, s // per_kv_block, 0, s % per_kv_block)),
            pl.BlockSpec((tm, pool_width), tok),
        ],
        out_shape=[
            jax.ShapeDtypeStruct((B, H, S, QK_DIM), BF16),
            jax.ShapeDtypeStruct((B, H, S, QK_DIM), BF16),
            jax.ShapeDtypeStruct((B, H, S // tk, VT_ROWS, tk), BF16),
            jax.ShapeDtypeStruct((T, pool_width), BF16),
        ],
        scratch_shapes=[pltpu.VMEM((POOL_HISTORY, pool_width), F32)],
        compiler_params=_params("arbitrary", "arbitrary"),
        name="even_in",
    )(xf, g, win, qag, kvag, wuq, wuk, wuvt, gains, poolw, pscale, tab)


ATTN_HEADS_PER_STEP = 4
ATTN_SLOTS = 2
ATTN_SCRATCH_PER_HEAD = 2 + 4 * ATTN_SLOTS


def _attn_kernel(q_ref, k_ref, vt_ref, o_ref, *scratch, tq):
    i = pl.program_id(2)
    heads = range(ATTN_HEADS_PER_STEP)
    ns = ATTN_SLOTS
    per_head = [scratch[h * ATTN_SCRATCH_PER_HEAD:(h + 1) * ATTN_SCRATCH_PER_HEAD] for h in heads]
    m_refs = [r[0] for r in per_head]
    acc_refs = [r[1] for r in per_head]
    s_refs = [r[2:2 + ns] for r in per_head]
    p_refs = [r[2 + ns:2 + 2 * ns] for r in per_head]
    a_refs = [r[2 + 2 * ns:2 + 3 * ns] for r in per_head]
    bm_refs = [r[2 + 3 * ns:2 + 4 * ns] for r in per_head]

    def scores(h, j, slot):
        start = pl.multiple_of(j * tq, tq)
        kb = k_ref[0, h, pl.ds(start, tq), :]
        s = lax.dot_general(kb, q_ref[0, h], (((1,), (1,)), ((), ())), preferred_element_type=F32)
        s_refs[h][slot][...] = s
        bm_refs[h][slot][...] = jnp.max(s, axis=0, keepdims=True)

    def softmax(h, slot, masked):
        s = s_refs[h][slot][...]
        if masked:
            key = lax.broadcasted_iota(jnp.int32, s.shape, 0)
            qry = lax.broadcasted_iota(jnp.int32, s.shape, 1)
            s = jnp.where(key <= qry, s, -1e30)
            block_max = jnp.max(s, axis=0, keepdims=True)
        else:
            block_max = bm_refs[h][slot][...]
        m_prev = m_refs[h][...]
        m_new = jnp.maximum(m_prev, block_max)
        m_refs[h][...] = m_new
        a_refs[h][slot][...] = jnp.exp2(m_prev - m_new)
        p_refs[h][slot][...] = jnp.exp2(s - m_new).astype(BF16)

    def accumulate(h, j, slot):
        acc_refs[h][...] = (a_refs[h][slot][...] * acc_refs[h][...]
                            + _dot(vt_ref[0, h, j], p_refs[h][slot][...]))

    def step(j, slot):
        for h in heads:
            scores(h, j + 1, (slot + 1) % ns)
            accumulate(h, jnp.maximum(j - 1, 0), (slot - 1) % ns)
            softmax(h, slot, False)

    def last(slot):
        for h in heads:
            accumulate(h, jnp.maximum(i - 1, 0), (slot - 1) % ns)
            softmax(h, slot, True)
            accumulate(h, i, slot)
            acc = acc_refs[h][...]
            out_t = acc[0:V_DIM] / acc[V_DIM:V_DIM + 1]
            o_ref[0, :, h * V_DIM:(h + 1) * V_DIM] = out_t.T.astype(BF16)

    for h in heads:
        m_refs[h][...] = jnp.full(m_refs[h].shape, -jnp.inf, F32)
        acc_refs[h][...] = jnp.zeros(acc_refs[h].shape, F32)
        scores(h, 0, 0)
        p_refs[h][ns - 1][...] = jnp.zeros(p_refs[h][ns - 1].shape, BF16)
        a_refs[h][ns - 1][...] = jnp.ones(a_refs[h][ns - 1].shape, F32)

    def group(g, carry):
        for slot in range(ns):
            step(ns * g + slot, slot)
        return carry

    lax.fori_loop(0, i // ns, group, 0)

    base = (i // ns) * ns
    for rem in range(ns):
        @pl.when(i % ns == rem)
        def _(rem=rem):
            for slot in range(rem):
                step(base + slot, slot)
            last(rem)


def _attn_call(q, k, vt, *, tq):
    B, H, S, _ = q.shape
    n_kv = vt.shape[2]
    assert vt.shape[4] == tq
    hps = ATTN_HEADS_PER_STEP
    head_scratch = ([pltpu.VMEM((1, tq), F32), pltpu.VMEM((VT_ROWS, tq), F32)]
                    + [pltpu.VMEM((tq, tq), F32)] * ATTN_SLOTS
                    + [pltpu.VMEM((tq, tq), BF16)] * ATTN_SLOTS
                    + [pltpu.VMEM((1, tq), F32)] * (2 * ATTN_SLOTS))
    assert len(head_scratch) == ATTN_SCRATCH_PER_HEAD
    return pl.pallas_call(
        functools.partial(_attn_kernel, tq=tq),
        grid=(B, H // hps, S // tq),
        in_specs=[
            pl.BlockSpec((1, hps, tq, QK_DIM), lambda b, h, i: (b, h, i, 0)),
            pl.BlockSpec((1, hps, S, QK_DIM), lambda b, h, i: (b, h, 0, 0)),
            pl.BlockSpec((1, hps, n_kv, VT_ROWS, tq), lambda b, h, i: (b, h, 0, 0, 0)),
        ],
        out_specs=pl.BlockSpec((1, tq, hps * V_DIM), lambda b, h, i: (b, i, h)),
        out_shape=jax.ShapeDtypeStruct((B, S, H * V_DIM), BF16),
        scratch_shapes=head_scratch * hps,
        compiler_params=_params("parallel", "parallel", "arbitrary"),
        name="attn",
    )(q, k, vt)


def _outproj_kernel(x_ref, a_ref, b_ref, wa_ref, wb_ref, o_ref):
    o_ref[...] = x_ref[...] + _dot(a_ref[...], wa_ref[...]) + _dot(b_ref[...], wb_ref[...])


def _outproj_call(xf, a, b, wout, e, *, tm):
    T, D = xf.shape
    wa_rows = a.shape[1]
    wb_rows = b.shape[1]
    assert wa_rows == wb_rows
    return pl.pallas_call(
        _outproj_kernel,
        grid=(T // tm,),
        in_specs=[
            pl.BlockSpec((tm, D), lambda i: (i, 0)),
            pl.BlockSpec((tm, wa_rows), lambda i: (i, 0)),
            pl.BlockSpec((tm, wb_rows), lambda i: (i, 0)),
            _resident((None, wa_rows, D), lambda i: (e, 0, 0)),
            _resident((None, wb_rows, D), lambda i: (e, 1, 0)),
        ],
        out_specs=pl.BlockSpec((tm, D), lambda i: (i, 0)),
        out_shape=jax.ShapeDtypeStruct((T, D), F32),
        compiler_params=_params("parallel"),
        name="outproj",
    )(xf, a, b, wout, wout)


def _conv_kernel(x_ref, g_ref, wb_ref, wc_ref, wu_ref, cw_ref, wo_ref, o_ref, r_ref, carry_ref,
                 *, tm, tiles_per_seq):
    i = pl.program_id(0)
    c = pl.program_id(1)
    first = c == 0

    @pl.when(first)
    def _():
        _store_inv_rms(x_ref, r_ref)

    x = x_ref[...]
    hn = _normed_operand(x, r_ref, g_ref)
    cw = cw_ref[...]
    first_tile = i % tiles_per_seq == 0
    tc = wo_ref.shape[0]
    gated = []
    for lo in range(0, tc, MXU_TILE):
        cols = slice(lo, lo + MXU_TILE)
        gate_b = _dot(hn, wb_ref[:, cols])
        v = _dot(hn, wc_ref[:, cols]) * _dot(hn, wu_ref[:, cols])
        prev = jnp.where(first_tile, 0.0, carry_ref[c, :, cols])
        carry_ref[c, :, cols] = v[tm - CONV_HISTORY:tm]
        ext = jnp.concatenate([prev, v], axis=0)
        conv = cw[CONV_WIDTH - 1:CONV_WIDTH, cols] * v
        for back in range(1, CONV_WIDTH):
            tap = CONV_WIDTH - 1 - back
            conv = conv + cw[tap:tap + 1, cols] * pltpu.roll(ext, back, axis=0)[CONV_HISTORY:]
        gated.append((gate_b * conv).astype(BF16))
    o_ref[...] = jnp.where(first, x, o_ref[...]) + _dot(jnp.concatenate(gated, axis=1), wo_ref[...])


def _conv_call(xf, g, win, convw, wout, o, *, S, tm, tc):
    T, D = xf.shape
    C = wout.shape[1]
    n_c = C // tc
    kern = functools.partial(_conv_kernel, tm=tm, tiles_per_seq=S // tm)
    return pl.pallas_call(
        kern,
        grid=(T // tm, n_c),
        in_specs=[
            pl.BlockSpec((tm, D), lambda i, c: (i, 0)),
            pl.BlockSpec((1, D), lambda i, c: (0, 0)),
            pl.BlockSpec((None, D, tc), lambda i, c: (o, 0, c)),
            pl.BlockSpec((None, D, tc), lambda i, c: (o, 0, n_c + c)),
            pl.BlockSpec((None, D, tc), lambda i, c: (o, 0, 2 * n_c + c)),
            pl.BlockSpec((None, CONV_WIDTH, tc), lambda i, c: (o, 0, c)),
            pl.BlockSpec((None, tc, D), lambda i, c: (o, c, 0)),
        ],
        out_specs=pl.BlockSpec((tm, D), lambda i, c: (i, 0)),
        out_shape=jax.ShapeDtypeStruct((T, D), F32),
        scratch_shapes=[pltpu.VMEM((tm, LANES), F32),
                        pltpu.VMEM((n_c, CONV_HISTORY, tc), F32)],
        compiler_params=_params("arbitrary", "arbitrary"),
        name="conv_mixer",
    )(xf, g, win, win, win, convw, wout)


def _mlp_pipelined_kernel(*refs, grid, in_specs, out_specs, cast_next):
    *hbm_refs, r_ref, step_ref = refs
    step_ref[0] = 0

    def body(*block_refs):
        step = step_ref[0]
        step_ref[0] = step + 1
        _mlp_step(step % grid[1] == 0, *block_refs, r_ref, cast_next=cast_next)

    pltpu.emit_pipeline(
        body, grid=grid, in_specs=in_specs, out_specs=out_specs,
        dimension_semantics=(pltpu.ARBITRARY, pltpu.ARBITRARY),
    )(*hbm_refs)


def _mlp_step(first, x_ref, g_ref, wup_ref, wdn_ref, *rest, cast_next):
    if cast_next:
        next_up_ref, next_dn_ref, o_ref, next_up_out_ref, next_dn_out_ref, r_ref = rest
        next_up_out_ref[...] = next_up_ref[...].astype(BF16)
        next_dn_out_ref[...] = next_dn_ref[...].astype(BF16)
    else:
        o_ref, r_ref = rest

    @pl.when(first)
    def _():
        _store_inv_rms(x_ref, r_ref)

    x = x_ref[...]
    up = _dot(_normed_operand(x, r_ref, g_ref), wup_ref[...])
    act = jnp.square(jnp.maximum(up, 0.0)).astype(BF16)
    o_ref[...] = jnp.where(first, x, o_ref[...]) + _dot(act, wdn_ref[...])


BF16_SUBLANES = 16


def _mlp_call(xf, g, wup, wdn, *, tm, tf, next_weights=None):
    T, D = xf.shape
    F = wup.shape[1]
    nf = F // tf
    n_steps = (T // tm) * nf
    in_specs = [
        pl.BlockSpec((tm, D), lambda i, f: (i, 0)),
        pl.BlockSpec((1, D), lambda i, f: (0, 0)),
        pl.BlockSpec((D, tf), lambda i, f: (0, f)),
        pl.BlockSpec((tf, D), lambda i, f: (f, 0)),
    ]
    out_specs = [pl.BlockSpec((tm, D), lambda i, f: (i, 0))]
    out_shape = [jax.ShapeDtypeStruct((T, D), F32)]
    args = [xf, g, wup, wdn]
    if next_weights is not None:
        w_up32, w_dn32, layer = next_weights
        for w32 in (w_up32, w_dn32):
            rows, cols = w32.shape[1:]
            block_rows = max(BF16_SUBLANES, rows // n_steps)
            n_blocks = rows // block_rows
            assert rows % block_rows == 0 and n_steps % n_blocks == 0
            block_of = lambda i, f, n_blocks=n_blocks: ((i * nf + f) * n_blocks) // n_steps
            in_specs.append(pl.BlockSpec((None, block_rows, cols),
                                         lambda i, f, block_of=block_of: (layer, block_of(i, f), 0)))
            out_specs.append(pl.BlockSpec((block_rows, cols),
                                          lambda i, f, block_of=block_of: (block_of(i, f), 0)))
            out_shape.append(jax.ShapeDtypeStruct((rows, cols), BF16))
            args.append(w32)
    whole = pl.BlockSpec(memory_space=pl.ANY)
    outs = pl.pallas_call(
        functools.partial(_mlp_pipelined_kernel, grid=(T // tm, nf), in_specs=in_specs,
                          out_specs=out_specs, cast_next=next_weights is not None),
        in_specs=[whole] * len(args),
        out_specs=[whole] * len(out_shape),
        out_shape=out_shape,
        scratch_shapes=[pltpu.VMEM((tm, LANES), F32), pltpu.SMEM((1,), jnp.int32)],
        compiler_params=pltpu.CompilerParams(vmem_limit_bytes=VMEM_LIMIT_BYTES),
        name="mlp",
    )(*args)
    return outs if next_weights is not None else outs[0]


def _swap_halves(a):
    half = a.shape[-1] // 2
    return jnp.concatenate([a[..., half:], a[..., :half]], axis=-1)


def _prep_even_weights(even_w_in, even_w_uq, even_w_ukv, even_q_norm_g, even_k_norm_g, lora, pool_width):
    H = MLA_HEADS
    E = even_w_in.shape[0]
    off_kr = 2 * lora
    off_pool = off_kr + ROPE_DIM
    w = even_w_in.astype(BF16)
    kr = w[:, :, off_kr:off_pool]
    krs = _swap_halves(kr)
    win = jnp.concatenate([w[:, :, :off_kr], w[:, :, off_pool:off_pool + pool_width], kr, kr, krs, krs], axis=-1)

    wq = even_w_uq.astype(BF16).reshape(E, lora, H, QK_DIM)
    rope = wq[..., NOPE_DIM:]
    by_pair = lambda a: a.reshape(E, lora, H // 2, 2 * a.shape[-1])
    wuq = jnp.concatenate([by_pair(wq[..., :NOPE_DIM]), by_pair(rope), by_pair(_swap_halves(rope))], axis=-1)
    wuq = wuq.reshape(E, lora, -1)

    wkv = even_w_ukv.astype(BF16).reshape(E, even_w_ukv.shape[1], H, NOPE_DIM + V_DIM)
    wuk = wkv[..., :NOPE_DIM].reshape(E, wkv.shape[1], H * NOPE_DIM)
    wuvt = jnp.swapaxes(wkv[..., NOPE_DIM:].reshape(E, wkv.shape[1], H * V_DIM), 1, 2)

    def gain_rows(gvec):
        rope = gvec[:, NOPE_DIM:]
        ropes = _swap_halves(rope)
        return [gvec[:, :NOPE_DIM], jnp.concatenate([rope, rope], -1), jnp.concatenate([ropes, ropes], -1)]

    rows = gain_rows(even_q_norm_g) + gain_rows(even_k_norm_g)
    rows += [jnp.zeros_like(rows[0])] * 2
    gains = jnp.stack(rows, axis=1).astype(F32)
    return win, wuq, wuk, wuvt, gains


def _pick_tile(n, want):
    t = min(n, want)
    while n % t:
        t //= 2
    return t


def kernel(x, positions, mix_norm_g, mlp_norm_g, w_mlp_up, w_mlp_down, even_w_in, even_q_a_norm_g,
           even_kv_a_norm_g, even_w_uq, even_w_ukv, even_q_norm_g, even_k_norm_g, even_pool_w,
           even_pool_scale, even_w_out, odd_w_in, odd_conv_w, odd_w_out):
    B, S, D = x.shape
    T = B * S
    depth = mix_norm_g.shape[0]
    lora = even_q_a_norm_g.shape[1]
    pool_width = even_pool_scale.shape[1]

    tm_mlp = _pick_tile(S, 512)
    tf_mlp = _pick_tile(w_mlp_up.shape[2], 2048)
    tm_even = _pick_tile(S, 512)
    tq = _pick_tile(S, 512)
    tc_conv = _pick_tile(odd_w_out.shape[1], 1024)

    wup = w_mlp_up[0].astype(BF16)
    wdn = w_mlp_down[0].astype(BF16)
    win_e, wuq, wuk, wuvt, gains = _prep_even_weights(even_w_in, even_w_uq, even_w_ukv, even_q_norm_g,
                                                      even_k_norm_g, lora, pool_width)
    poolw = even_pool_w.astype(BF16)
    wout_e = even_w_out.astype(BF16)
    win_o = odd_w_in.astype(BF16)
    wout_o = odd_w_out.astype(BF16)

    tab = _rope_table_call(positions.reshape(T, 1), _pick_tile(T, 1024))

    xf = x.reshape(T, D)
    for layer in range(depth):
        g_mix = mix_norm_g[layer][None, :]
        if layer % 2 == 0:
            e = layer // 2
            q, k, vt, b = _even_in_call(
                xf, g_mix, win_e, even_q_a_norm_g[e][None, :], even_kv_a_norm_g[e][None, :],
                wuq, wuk, wuvt, gains, poolw, even_pool_scale[e][None, :], tab, e,
                B=B, S=S, tm=tm_even, tk=tq)
            a = _attn_call(q, k, vt, tq=tq).reshape(T, MLA_HEADS * V_DIM)
            xf = _outproj_call(xf, a, b, wout_e, e, tm=tm_mlp)
        else:
            o = layer // 2
            xf = _conv_call(xf, g_mix, win_o, odd_conv_w, wout_o, o, S=S, tm=tm_mlp, tc=tc_conv)
        g_mlp = mlp_norm_g[layer][None, :]
        if layer + 1 < depth:
            xf, wup, wdn = _mlp_call(xf, g_mlp, wup, wdn, tm=tm_mlp, tf=tf_mlp,
                                     next_weights=(w_mlp_up, w_mlp_down, layer + 1))
        else:
            xf = _mlp_call(xf, g_mlp, wup, wdn, tm=tm_mlp, tf=tf_mlp)
    return xf.reshape(B, S, D)
```

```python
import functools

import jax
import jax.numpy as jnp
from jax import lax
from jax.experimental import pallas as pl
from jax.experimental.pallas import tpu as pltpu

F32 = jnp.float32
BF16 = jnp.bfloat16

RMS_EPS = 1e-6
ROPE_THETA = 10000.0
MLA_HEADS = 8
NOPE_DIM = 128
ROPE_DIM = 64
QK_DIM = NOPE_DIM + ROPE_DIM
V_DIM = 128
VT_ROWS = V_DIM + 16
LOG2_E = 1.4426950408889634
POOL_WINDOWS = (2, 4, 8, 16)
POOL_HISTORY = 16
CONV_WIDTH = 3
CONV_HISTORY = 8
LANES = 128
MXU_TILE = 256

V7X_VMEM_BYTES = 64 * 1024 * 1024
VMEM_LIMIT_BYTES = V7X_VMEM_BYTES - 4 * 1024 * 1024


def _rms(xf, g):
    ms = jnp.mean(xf * xf, axis=-1, keepdims=True)
    return xf * lax.rsqrt(ms + RMS_EPS) * g


def _dot(a, b):
    return jnp.dot(a, b, preferred_element_type=F32)


def _store_inv_rms(x_ref, r_ref):
    x = x_ref[...]
    r = lax.rsqrt(jnp.mean(x * x, axis=-1, keepdims=True) + RMS_EPS)
    r_ref[...] = jnp.broadcast_to(r, r_ref.shape)


def _normed_operand(x, r_ref, g_ref):
    return (x * jnp.tile(r_ref[...], (1, x.shape[1] // LANES)) * g_ref[...]).astype(BF16)


def _params(*sem):
    return pltpu.CompilerParams(dimension_semantics=sem, vmem_limit_bytes=VMEM_LIMIT_BYTES)


def _resident(block_shape, index_map):
    return pl.BlockSpec(block_shape, index_map, pipeline_mode=pl.Buffered(1))


def _rope_table_kernel(pos_ref, freq_ref, tab_ref):
    ang = pos_ref[...].astype(F32) * freq_ref[...]
    lane = lax.broadcasted_iota(jnp.int32, ang.shape, 1)
    sign = jnp.where((lane % ROPE_DIM) < ROPE_DIM // 2, -1.0, 1.0)
    tab_ref[:, :LANES] = jnp.cos(ang)
    tab_ref[:, LANES:] = jnp.sin(ang) * sign


def _rope_table_call(pos_col, tm):
    T = pos_col.shape[0]
    inv_freq = 1.0 / (ROPE_THETA ** (jnp.arange(0, ROPE_DIM, 2, dtype=F32) / ROPE_DIM))
    freq = jnp.tile(inv_freq, LANES // (ROPE_DIM // 2))[None, :]
    return pl.pallas_call(
        _rope_table_kernel,
        grid=(T // tm,),
        in_specs=[pl.BlockSpec((tm, 1), lambda i: (i, 0)),
                  pl.BlockSpec((1, LANES), lambda i: (0, 0))],
        out_specs=pl.BlockSpec((tm, 2 * LANES), lambda i: (i, 0)),
        out_shape=jax.ShapeDtypeStruct((T, 2 * LANES), F32),
        compiler_params=_params("parallel"),
        name="rope_table",
    )(pos_col, freq)


EVEN_IN_STREAMS = 2


def _even_in_kernel(x_ref, g_ref, win_ref, qag_ref, kvag_ref, wuq_ref, wuk_ref, wuvt_ref, gains_ref,
                    poolw_ref, pscale_ref, tab_ref,
                    q_ref, k_ref, vt_ref, b_ref, carry_ref, *, tm, lora, pool_width, scale):
    s = pl.program_id(1)
    off_pool = 2 * lora
    off_kr = off_pool + pool_width
    gains = gains_ref[...]
    qg_nope, qg_rope2, qg_ropes2 = gains[0:1], gains[1:2], gains[2:3]
    kg_nope, kg_rope2, kg_ropes2 = gains[3:4], gains[4:5], gains[5:6]
    nheads = MLA_HEADS
    gd = pool_width // len(POOL_WINDOWS)
    pscale = pscale_ref[...]
    pair_w = 2 * NOPE_DIM + 2 * LANES
    n = tm // EVEN_IN_STREAMS
    assert len(POOL_WINDOWS) == 4 and nheads == 8 and n % POOL_HISTORY == 0
    u_tails = {}

    def stream(k):
        rows = slice(k * n, (k + 1) * n)
        lane = lax.broadcasted_iota(jnp.int32, (n, LANES), 1)
        low_half = lane < ROPE_DIM
        t_in_seq = s * tm + k * n + lax.broadcasted_iota(jnp.int32, (n, 1), 0)

        hn = _rms(x_ref[rows, :], g_ref[...]).astype(BF16)
        c_kv = _dot(hn, win_ref[:, lora:2 * lora])
        kr_both = _dot(hn, win_ref[:, off_kr:off_kr + 2 * LANES])
        c_q = _dot(hn, win_ref[:, :lora])
        yield

        ckvn = _rms(c_kv, kvag_ref[...]).astype(BF16)
        cqn = _rms(c_q, qag_ref[...]).astype(BF16)
        kr2 = kr_both[:, :LANES]
        krs2 = kr_both[:, LANES:]
        cos2 = tab_ref[rows, :LANES]
        sin2 = tab_ref[rows, LANES:]
        knall = _dot(ckvn, wuk_ref[...])

        def pool_project(g0, g1):
            return _dot(hn, win_ref[:, off_pool + g0 * gd:off_pool + g1 * gd])

        def pool_windows(u_all, g0, g1):
            pooled = []
            for gi in range(g0, g1):
                w = POOL_WINDOWS[gi]
                cols = slice(gi * gd, (gi + 1) * gd)
                u = u_all[:, (gi - g0) * gd:(gi - g0 + 1) * gd]
                if k == 0:
                    prev = jnp.where(s == 0, 0.0, carry_ref[:, cols])
                else:
                    prev = u_tails[(k - 1, gi)]
                u_tails[(k, gi)] = u[n - POOL_HISTORY:n]
                if k == EVEN_IN_STREAMS - 1:
                    carry_ref[:, cols] = u[n - POOL_HISTORY:n]
                e = jnp.concatenate([prev, u], axis=0)
                sh = 1
                while sh < w:
                    e = e + pltpu.roll(e, sh, axis=0)
                    sh *= 2
                inv_cnt = 1.0 / jnp.minimum(t_in_seq + 1, w).astype(F32)
                pooled.append((e[POOL_HISTORY:] * inv_cnt - u).astype(BF16))
            return pooled

        def pool_mix(pooled, g0):
            for j, pg in enumerate(pooled):
                cols = slice((g0 + j) * gd, (g0 + j + 1) * gd)
                b_ref[rows, cols] = (_dot(pg, poolw_ref[g0 + j]) * pscale[:, cols]).astype(BF16)

        def query_project(p):
            return _dot(cqn, wuq_ref[:, p * pair_w:(p + 1) * pair_w])

        def query_finish(qp, p):
            qr2 = qp[:, 2 * NOPE_DIM:2 * NOPE_DIM + LANES]
            qrs2 = qp[:, 2 * NOPE_DIM + LANES:]
            sq = qr2 * qr2
            roped = qr2 * qg_rope2 * cos2 + qrs2 * qg_ropes2 * sin2
            for e in range(2):
                h = 2 * p + e
                qn = qp[:, e * NOPE_DIM:(e + 1) * NOPE_DIM]
                mine = low_half if e == 0 else jnp.logical_not(low_half)
                ss = jnp.sum(qn * qn + jnp.where(mine, sq, 0.0), axis=-1, keepdims=True)
                r = lax.rsqrt(ss * (1.0 / QK_DIM) + RMS_EPS) * scale
                q_ref[0, h, rows, 0:NOPE_DIM] = (qn * r * qg_nope).astype(BF16)
                rp = roped * r
                if e == 1:
                    rp = pltpu.roll(rp, ROPE_DIM, axis=1)
                q_ref[0, h, rows, NOPE_DIM:QK_DIM] = rp[:, :ROPE_DIM].astype(BF16)

        u01 = pool_project(0, 2)
        qp0 = query_project(0)
        qp1 = query_project(1)
        yield

        kr_sq = jnp.where(low_half, kr2 * kr2, 0.0)
        k_roped2 = kr2 * kg_rope2 * cos2 + krs2 * kg_ropes2 * sin2
        for h in range(nheads):
            kn = knall[:, h * NOPE_DIM:(h + 1) * NOPE_DIM]
            ss = jnp.sum(kn * kn + kr_sq, axis=-1, keepdims=True)
            r = lax.rsqrt(ss * (1.0 / QK_DIM) + RMS_EPS)
            k_ref[0, h, rows, 0:NOPE_DIM] = (kn * r * kg_nope).astype(BF16)
            k_ref[0, h, rows, NOPE_DIM:QK_DIM] = (k_roped2 * r)[:, :ROPE_DIM].astype(BF16)
        pooled01 = pool_windows(u01, 0, 2)
        u23 = pool_project(2, 4)
        pool_mix(pooled01, 0)
        yield

        query_finish(qp0, 0)
        query_finish(qp1, 1)
        qp2 = query_project(2)
        qp3 = query_project(3)
        pooled23 = pool_windows(u23, 2, 4)
        pool_mix(pooled23, 2)
        vt_all = lax.dot_general(wuvt_ref[...], ckvn, (((1,), (1,)), ((), ())), preferred_element_type=F32)
        yield

        query_finish(qp2, 2)
        query_finish(qp3, 3)
        for h in range(nheads):
            vt_ref[0, h, 0, 0:V_DIM, rows] = vt_all[h * V_DIM:(h + 1) * V_DIM].astype(BF16)
            vt_ref[0, h, 0, V_DIM:VT_ROWS, rows] = jnp.ones((VT_ROWS - V_DIM, n), BF16)
        yield

    streams = [stream(k) for k in range(EVEN_IN_STREAMS)]
    started = 0
    live = []
    while started < len(streams) or live:
        if started < len(streams):
            live.append(streams[started])
            started += 1
        for gen in list(live):
            try:
                next(gen)
            except StopIteration:
                live.remove(gen)


def _even_in_call(xf, g, win, qag, kvag, wuq, wuk, wuvt, gains, poolw, pscale, tab, e, *, B, S, tm, tk):
    T, D = xf.shape
    lora = qag.shape[1]
    pool_width = pscale.shape[1]
    n_s = S // tm
    H = MLA_HEADS
    per_kv_block = tk // tm
    tok = lambda b, s: (b * n_s + s, 0)
    const2 = lambda b, s: (0, 0)
    layer3 = lambda b, s: (e, 0, 0)
    kern = functools.partial(_even_in_kernel, tm=tm, lora=lora, pool_width=pool_width,
                             scale=QK_DIM ** -0.5 * LOG2_E)
    return pl.pallas_call(
        kern,
        grid=(B, n_s),
        in_specs=[
            pl.BlockSpec((tm, D), tok),
            _resident((1, D), const2),
            _resident((None,) + win.shape[1:], layer3),
            _resident((1, lora), const2),
            _resident((1, lora), const2),
            _resident((None,) + wuq.shape[1:], layer3),
            _resident((None,) + wuk.shape[1:], layer3),
            _resident((None,) + wuvt.shape[1:], layer3),
            _resident((None,) + gains.shape[1:], layer3),
            _resident((None,) + poolw.shape[1:], lambda b, s: (e, 0, 0, 0)),
            _resident((1, pool_width), const2),
            pl.BlockSpec((tm, 2 * LANES), tok),
        ],
        out_specs=[
            pl.BlockSpec((1, H, tm, QK_DIM), lambda b, s: (b, 0, s, 0)),
            pl.BlockSpec((1, H, tm, QK_DIM), lambda b, s: (b, 0, s, 0)),
            pl.BlockSpec((1, H, 1, VT_ROWS, tm),
                         lambda b, s: (b, 0, s // per_kv_block, 0, s % per_kv_block)),
            pl.BlockSpec((tm, pool_width), tok),
        ],
        out_shape=[
            jax.ShapeDtypeStruct((B, H, S, QK_DIM), BF16),
            jax.ShapeDtypeStruct((B, H, S, QK_DIM), BF16),
            jax.ShapeDtypeStruct((B, H, S // tk, VT_ROWS, tk), BF16),
            jax.ShapeDtypeStruct((T, pool_width), BF16),
        ],
        scratch_shapes=[pltpu.VMEM((POOL_HISTORY, pool_width), F32)],
        compiler_params=_params("arbitrary", "arbitrary"),
        name="even_in",
    )(xf, g, win, qag, kvag, wuq, wuk, wuvt, gains, poolw, pscale, tab)


ATTN_HEADS_PER_STEP = 4
ATTN_SLOTS = 2
ATTN_SCRATCH_PER_HEAD = 2 + 4 * ATTN_SLOTS
ATTN_PAD_LANES = LANES


def _attn_kernel(q_ref, k_ref, vt_ref, o_ref, *scratch, tq):
    i = pl.program_id(2)
    heads = range(ATTN_HEADS_PER_STEP)
    ns = ATTN_SLOTS
    per_head = [scratch[h * ATTN_SCRATCH_PER_HEAD:(h + 1) * ATTN_SCRATCH_PER_HEAD] for h in heads]
    used = lambda ref: ref.at[:, pl.ds(0, tq)]
    m_refs = [r[0] for r in per_head]
    acc_refs = [used(r[1]) for r in per_head]
    s_refs = [[used(ref) for ref in r[2:2 + ns]] for r in per_head]
    p_refs = [[used(ref) for ref in r[2 + ns:2 + 2 * ns]] for r in per_head]
    a_refs = [r[2 + 2 * ns:2 + 3 * ns] for r in per_head]
    bm_refs = [r[2 + 3 * ns:2 + 4 * ns] for r in per_head]

    def scores(h, j, slot):
        start = pl.multiple_of(j * tq, tq)
        kb = k_ref[0, h, pl.ds(start, tq), :]
        s = lax.dot_general(kb, q_ref[0, h], (((1,), (1,)), ((), ())), preferred_element_type=F32)
        s_refs[h][slot][...] = s
        bm_refs[h][slot][...] = jnp.max(s, axis=0, keepdims=True)

    def softmax(h, slot, masked):
        s = s_refs[h][slot][...]
        if masked:
            key = lax.broadcasted_iota(jnp.int32, s.shape, 0)
            qry = lax.broadcasted_iota(jnp.int32, s.shape, 1)
            s = jnp.where(key <= qry, s, -1e30)
            block_max = jnp.max(s, axis=0, keepdims=True)
        else:
            block_max = bm_refs[h][slot][...]
        m_prev = m_refs[h][...]
        m_new = jnp.maximum(m_prev, block_max)
        m_refs[h][...] = m_new
        a_refs[h][slot][...] = jnp.exp2(m_prev - m_new)
        p_refs[h][slot][...] = jnp.exp2(s - m_new).astype(BF16)

    def accumulate(h, j, slot):
        acc_refs[h][...] = (a_refs[h][slot][...] * acc_refs[h][...]
                            + _dot(vt_ref[0, h, j], p_refs[h][slot][...]))

    def step(j, slot):
        for h in heads:
            scores(h, j + 1, (slot + 1) % ns)
            accumulate(h, jnp.maximum(j - 1, 0), (slot - 1) % ns)
            softmax(h, slot, False)

    def last(slot):
        for h in heads:
            accumulate(h, jnp.maximum(i - 1, 0), (slot - 1) % ns)
            softmax(h, slot, True)
            accumulate(h, i, slot)
            acc = acc_refs[h][...]
            out_t = acc[0:V_DIM] / acc[V_DIM:V_DIM + 1]
            o_ref[0, :, h * V_DIM:(h + 1) * V_DIM] = out_t.T.astype(BF16)

    for h in heads:
        m_refs[h][...] = jnp.full(m_refs[h].shape, -jnp.inf, F32)
        acc_refs[h][...] = jnp.zeros(acc_refs[h].shape, F32)
        scores(h, 0, 0)
        p_refs[h][ns - 1][...] = jnp.zeros(p_refs[h][ns - 1].shape, BF16)
        a_refs[h][ns - 1][...] = jnp.ones(a_refs[h][ns - 1].shape, F32)

    def group(g, carry):
        for slot in range(ns):
            step(ns * g + slot, slot)
        return carry

    lax.fori_loop(0, i // ns, group, 0)

    base = (i // ns) * ns
    for rem in range(ns):
        @pl.when(i % ns == rem)
        def _(rem=rem):
            for slot in range(rem):
                step(base + slot, slot)
            last(rem)


def _attn_call(q, k, vt, *, tq):
    B, H, S, _ = q.shape
    n_kv = vt.shape[2]
    assert vt.shape[4] == tq
    hps = ATTN_HEADS_PER_STEP
    wide = tq + ATTN_PAD_LANES
    head_scratch = ([pltpu.VMEM((1, tq), F32), pltpu.VMEM((VT_ROWS, wide), F32)]
                    + [pltpu.VMEM((tq, wide), F32)] * ATTN_SLOTS
                    + [pltpu.VMEM((tq, wide), BF16)] * ATTN_SLOTS
                    + [pltpu.VMEM((1, tq), F32)] * (2 * ATTN_SLOTS))
    assert len(head_scratch) == ATTN_SCRATCH_PER_HEAD
    return pl.pallas_call(
        functools.partial(_attn_kernel, tq=tq),
        grid=(B, H // hps, S // tq),
        in_specs=[
            pl.BlockSpec((1, hps, tq, QK_DIM), lambda b, h, i: (b, h, i, 0)),
            pl.BlockSpec((1, hps, S, QK_DIM), lambda b, h, i: (b, h, 0, 0)),
            pl.BlockSpec((1, hps, n_kv, VT_ROWS, tq), lambda b, h, i: (b, h, 0, 0, 0)),
        ],
        out_specs=pl.BlockSpec((1, tq, hps * V_DIM), lambda b, h, i: (b, i, h)),
        out_shape=jax.ShapeDtypeStruct((B, S, H * V_DIM), BF16),
        scratch_shapes=head_scratch * hps,
        compiler_params=_params("parallel", "parallel", "arbitrary"),
        name="attn",
    )(q, k, vt)


def _outproj_kernel(x_ref, a_ref, b_ref, wa_ref, wb_ref, o_ref):
    o_ref[...] = x_ref[...] + _dot(a_ref[...], wa_ref[...]) + _dot(b_ref[...], wb_ref[...])


def _outproj_call(xf, a, b, wout, e, *, tm):
    T, D = xf.shape
    wa_rows = a.shape[1]
    wb_rows = b.shape[1]
    assert wa_rows == wb_rows
    return pl.pallas_call(
        _outproj_kernel,
        grid=(T // tm,),
        in_specs=[
            pl.BlockSpec((tm, D), lambda i: (i, 0)),
            pl.BlockSpec((tm, wa_rows), lambda i: (i, 0)),
            pl.BlockSpec((tm, wb_rows), lambda i: (i, 0)),
            _resident((None, wa_rows, D), lambda i: (e, 0, 0)),
            _resident((None, wb_rows, D), lambda i: (e, 1, 0)),
        ],
        out_specs=pl.BlockSpec((tm, D), lambda i: (i, 0)),
        out_shape=jax.ShapeDtypeStruct((T, D), F32),
        compiler_params=_params("parallel"),
        name="outproj",
    )(xf, a, b, wout, wout)


def _conv_kernel(x_ref, g_ref, wb_ref, wc_ref, wu_ref, cw_ref, wo_ref, o_ref, r_ref, carry_ref,
                 *, tm, tiles_per_seq):
    i = pl.program_id(0)
    c = pl.program_id(1)
    first = c == 0

    @pl.when(first)
    def _():
        _store_inv_rms(x_ref, r_ref)

    x = x_ref[...]
    hn = _normed_operand(x, r_ref, g_ref)
    cw = cw_ref[...]
    first_tile = i % tiles_per_seq == 0
    tc = wo_ref.shape[0]
    gated = []
    for lo in range(0, tc, MXU_TILE):
        cols = slice(lo, lo + MXU_TILE)
        gate_b = _dot(hn, wb_ref[:, cols])
        v = _dot(hn, wc_ref[:, cols]) * _dot(hn, wu_ref[:, cols])
        prev = jnp.where(first_tile, 0.0, carry_ref[c, :, cols])
        carry_ref[c, :, cols] = v[tm - CONV_HISTORY:tm]
        ext = jnp.concatenate([prev, v], axis=0)
        conv = cw[CONV_WIDTH - 1:CONV_WIDTH, cols] * v
        for back in range(1, CONV_WIDTH):
            tap = CONV_WIDTH - 1 - back
            conv = conv + cw[tap:tap + 1, cols] * pltpu.roll(ext, back, axis=0)[CONV_HISTORY:]
        gated.append((gate_b * conv).astype(BF16))
    o_ref[...] = jnp.where(first, x, o_ref[...]) + _dot(jnp.concatenate(gated, axis=1), wo_ref[...])


def _conv_call(xf, g, win, convw, wout, o, *, S, tm, tc):
    T, D = xf.shape
    C = wout.shape[1]
    n_c = C // tc
    kern = functools.partial(_conv_kernel, tm=tm, tiles_per_seq=S // tm)
    return pl.pallas_call(
        kern,
        grid=(T // tm, n_c),
        in_specs=[
            pl.BlockSpec((tm, D), lambda i, c: (i, 0)),
            pl.BlockSpec((1, D), lambda i, c: (0, 0)),
            pl.BlockSpec((None, D, tc), lambda i, c: (o, 0, c)),
            pl.BlockSpec((None, D, tc), lambda i, c: (o, 0, n_c + c)),
            pl.BlockSpec((None, D, tc), lambda i, c: (o, 0, 2 * n_c + c)),
            pl.BlockSpec((None, CONV_WIDTH, tc), lambda i, c: (o, 0, c)),
            pl.BlockSpec((None, tc, D), lambda i, c: (o, c, 0)),
        ],
        out_specs=pl.BlockSpec((tm, D), lambda i, c: (i, 0)),
        out_shape=jax.ShapeDtypeStruct((T, D), F32),
        scratch_shapes=[pltpu.VMEM((tm, LANES), F32),
                        pltpu.VMEM((n_c, CONV_HISTORY, tc), F32)],
        compiler_params=_params("arbitrary", "arbitrary"),
        name="conv_mixer",
    )(xf, g, win, win, win, convw, wout)


def _mlp_kernel(x_ref, g_ref, wup_ref, wdn_ref, *rest, cast_next):
    if cast_next:
        next_up_ref, next_dn_ref, o_ref, next_up_out_ref, next_dn_out_ref, r_ref = rest
        next_up_out_ref[...] = next_up_ref[...].astype(BF16)
        next_dn_out_ref[...] = next_dn_ref[...].astype(BF16)
    else:
        o_ref, r_ref = rest
    first = pl.program_id(1) == 0

    @pl.when(first)
    def _():
        _store_inv_rms(x_ref, r_ref)

    tm = x_ref.shape[0]
    n = tm // MLP_ROW_STREAMS
    for k in range(MLP_ROW_STREAMS):
        rows = slice(k * n, (k + 1) * n)
        x = x_ref[rows, :]
        hn = (x * jnp.tile(r_ref[rows, :], (1, x.shape[1] // LANES)) * g_ref[...]).astype(BF16)
        up = _dot(hn, wup_ref[...])
        act = jnp.square(jnp.maximum(up, 0.0)).astype(BF16)
        o_ref[rows, :] = jnp.where(first, x, o_ref[rows, :]) + _dot(act, wdn_ref[...])


BF16_SUBLANES = 16
MLP_ROW_STREAMS = 2


def _mlp_call(xf, g, wup, wdn, *, tm, tf, next_weights=None):
    T, D = xf.shape
    F = wup.shape[1]
    nf = F // tf
    n_steps = (T // tm) * nf
    in_specs = [
        pl.BlockSpec((tm, D), lambda i, f: (i, 0)),
        pl.BlockSpec((1, D), lambda i, f: (0, 0)),
        pl.BlockSpec((D, tf), lambda i, f: (0, f)),
        pl.BlockSpec((tf, D), lambda i, f: (f, 0)),
    ]
    out_specs = [pl.BlockSpec((tm, D), lambda i, f: (i, 0))]
    out_shape = [jax.ShapeDtypeStruct((T, D), F32)]
    args = [xf, g, wup, wdn]
    if next_weights is not None:
        w_up32, w_dn32, layer = next_weights
        for w32 in (w_up32, w_dn32):
            rows, cols = w32.shape[1:]
            block_rows = max(BF16_SUBLANES, rows // n_steps)
            n_blocks = rows // block_rows
            assert rows % block_rows == 0 and n_steps % n_blocks == 0
            block_of = lambda i, f, n_blocks=n_blocks: ((i * nf + f) * n_blocks) // n_steps
            in_specs.append(pl.BlockSpec((None, block_rows, cols),
                                         lambda i, f, block_of=block_of: (layer, block_of(i, f), 0)))
            out_specs.append(pl.BlockSpec((block_rows, cols),
                                          lambda i, f, block_of=block_of: (block_of(i, f), 0)))
            out_shape.append(jax.ShapeDtypeStruct((rows, cols), BF16))
            args.append(w32)
    outs = pl.pallas_call(
        functools.partial(_mlp_kernel, cast_next=next_weights is not None),
        grid=(T // tm, nf),
        in_specs=in_specs,
        out_specs=out_specs,
        out_shape=out_shape,
        scratch_shapes=[pltpu.VMEM((tm, LANES), F32)],
        compiler_params=_params("arbitrary", "arbitrary"),
        name="mlp",
    )(*args)
    return outs if next_weights is not None else outs[0]


def _swap_halves(a):
    half = a.shape[-1] // 2
    return jnp.concatenate([a[..., half:], a[..., :half]], axis=-1)


def _prep_even_weights(even_w_in, even_w_uq, even_w_ukv, even_q_norm_g, even_k_norm_g, lora, pool_width):
    H = MLA_HEADS
    E = even_w_in.shape[0]
    off_kr = 2 * lora
    off_pool = off_kr + ROPE_DIM
    w = even_w_in.astype(BF16)
    kr = w[:, :, off_kr:off_pool]
    krs = _swap_halves(kr)
    win = jnp.concatenate([w[:, :, :off_kr], w[:, :, off_pool:off_pool + pool_width], kr, kr, krs, krs], axis=-1)

    wq = even_w_uq.astype(BF16).reshape(E, lora, H, QK_DIM)
    rope = wq[..., NOPE_DIM:]
    by_pair = lambda a: a.reshape(E, lora, H // 2, 2 * a.shape[-1])
    wuq = jnp.concatenate([by_pair(wq[..., :NOPE_DIM]), by_pair(rope), by_pair(_swap_halves(rope))], axis=-1)
    wuq = wuq.reshape(E, lora, -1)

    wkv = even_w_ukv.astype(BF16).reshape(E, even_w_ukv.shape[1], H, NOPE_DIM + V_DIM)
    wuk = wkv[..., :NOPE_DIM].reshape(E, wkv.shape[1], H * NOPE_DIM)
    wuvt = jnp.swapaxes(wkv[..., NOPE_DIM:].reshape(E, wkv.shape[1], H * V_DIM), 1, 2)

    def gain_rows(gvec):
        rope = gvec[:, NOPE_DIM:]
        ropes = _swap_halves(rope)
        return [gvec[:, :NOPE_DIM], jnp.concatenate([rope, rope], -1), jnp.concatenate([ropes, ropes], -1)]

    rows = gain_rows(even_q_norm_g) + gain_rows(even_k_norm_g)
    rows += [jnp.zeros_like(rows[0])] * 2
    gains = jnp.stack(rows, axis=1).astype(F32)
    return win, wuq, wuk, wuvt, gains


def _pick_tile(n, want):
    t = min(n, want)
    while n % t:
        t //= 2
    return t


def kernel(x, positions, mix_norm_g, mlp_norm_g, w_mlp_up, w_mlp_down, even_w_in, even_q_a_norm_g,
           even_kv_a_norm_g, even_w_uq, even_w_ukv, even_q_norm_g, even_k_norm_g, even_pool_w,
           even_pool_scale, even_w_out, odd_w_in, odd_conv_w, odd_w_out):
    B, S, D = x.shape
    T = B * S
    depth = mix_norm_g.shape[0]
    lora = even_q_a_norm_g.shape[1]
    pool_width = even_pool_scale.shape[1]

    tm_mlp = _pick_tile(S, 512)
    tf_mlp = _pick_tile(w_mlp_up.shape[2], 2048)
    tm_even = _pick_tile(S, 512)
    tq = _pick_tile(S, 512)
    tc_conv = _pick_tile(odd_w_out.shape[1], 1024)

    wup = w_mlp_up[0].astype(BF16)
    wdn = w_mlp_down[0].astype(BF16)
    win_e, wuq, wuk, wuvt, gains = _prep_even_weights(even_w_in, even_w_uq, even_w_ukv, even_q_norm_g,
                                                      even_k_norm_g, lora, pool_width)
    poolw = even_pool_w.astype(BF16)
    wout_e = even_w_out.astype(BF16)
    win_o = odd_w_in.astype(BF16)
    wout_o = odd_w_out.astype(BF16)

    tab = _rope_table_call(positions.reshape(T, 1), _pick_tile(T, 1024))

    xf = x.reshape(T, D)
    for layer in range(depth):
        g_mix = mix_norm_g[layer][None, :]
        if layer % 2 == 0:
            e = layer // 2
            q, k, vt, b = _even_in_call(
                xf, g_mix, win_e, even_q_a_norm_g[e][None, :], even_kv_a_norm_g[e][None, :],
                wuq, wuk, wuvt, gains, poolw, even_pool_scale[e][None, :], tab, e,
                B=B, S=S, tm=tm_even, tk=tq)
            a = _attn_call(q, k, vt, tq=tq).reshape(T, MLA_HEADS * V_DIM)
            xf = _outproj_call(xf, a, b, wout_e, e, tm=tm_mlp)
        else:
            o = layer // 2
            xf = _conv_call(xf, g_mix, win_o, odd_conv_w, wout_o, o, S=S, tm=tm_mlp, tc=tc_conv)
        g_mlp = mlp_norm_g[layer][None, :]
        if layer + 1 < depth:
            xf, wup, wdn = _mlp_call(xf, g_mlp, wup, wdn, tm=tm_mlp, tf=tf_mlp,
                                     next_weights=(w_mlp_up, w_mlp_down, layer + 1))
        else:
            xf = _mlp_call(xf, g_mlp, wup, wdn, tm=tm_mlp, tf=tf_mlp)
    return xf.reshape(B, S, D)
```

```python
import functools

import jax
import jax.numpy as jnp
from jax import lax
from jax.experimental import pallas as pl
from jax.experimental.pallas import tpu as pltpu

F32 = jnp.float32
BF16 = jnp.bfloat16

RMS_EPS = 1e-6
ROPE_THETA = 10000.0
MLA_HEADS = 8
NOPE_DIM = 128
ROPE_DIM = 64
QK_DIM = NOPE_DIM + ROPE_DIM
V_DIM = 128
VT_ROWS = V_DIM + 16
LOG2_E = 1.4426950408889634
POOL_WINDOWS = (2, 4, 8, 16)
POOL_HISTORY = 16
CONV_WIDTH = 3
CONV_HISTORY = 8
LANES = 128
MXU_TILE = 256

V7X_VMEM_BYTES = 64 * 1024 * 1024
VMEM_LIMIT_BYTES = V7X_VMEM_BYTES - 4 * 1024 * 1024


def _rms(xf, g):
    ms = jnp.mean(xf * xf, axis=-1, keepdims=True)
    return xf * lax.rsqrt(ms + RMS_EPS) * g


def _dot(a, b):
    return jnp.dot(a, b, preferred_element_type=F32)


def _store_inv_rms(x_ref, r_ref):
    x = x_ref[...]
    r = lax.rsqrt(jnp.mean(x * x, axis=-1, keepdims=True) + RMS_EPS)
    r_ref[...] = jnp.broadcast_to(r, r_ref.shape)


def _normed_operand(x, r_ref, g_ref):
    return (x * jnp.tile(r_ref[...], (1, x.shape[1] // LANES)) * g_ref[...]).astype(BF16)


def _params(*sem):
    return pltpu.CompilerParams(dimension_semantics=sem, vmem_limit_bytes=VMEM_LIMIT_BYTES)


def _resident(block_shape, index_map):
    return pl.BlockSpec(block_shape, index_map, pipeline_mode=pl.Buffered(1))


def _rope_table_kernel(pos_ref, freq_ref, tab_ref):
    ang = pos_ref[...].astype(F32) * freq_ref[...]
    lane = lax.broadcasted_iota(jnp.int32, ang.shape, 1)
    sign = jnp.where((lane % ROPE_DIM) < ROPE_DIM // 2, -1.0, 1.0)
    tab_ref[:, :LANES] = jnp.cos(ang)
    tab_ref[:, LANES:] = jnp.sin(ang) * sign


def _rope_table_call(pos_col, tm):
    T = pos_col.shape[0]
    inv_freq = 1.0 / (ROPE_THETA ** (jnp.arange(0, ROPE_DIM, 2, dtype=F32) / ROPE_DIM))
    freq = jnp.tile(inv_freq, LANES // (ROPE_DIM // 2))[None, :]
    return pl.pallas_call(
        _rope_table_kernel,
        grid=(T // tm,),
        in_specs=[pl.BlockSpec((tm, 1), lambda i: (i, 0)),
                  pl.BlockSpec((1, LANES), lambda i: (0, 0))],
        out_specs=pl.BlockSpec((tm, 2 * LANES), lambda i: (i, 0)),
        out_shape=jax.ShapeDtypeStruct((T, 2 * LANES), F32),
        compiler_params=_params("parallel"),
        name="rope_table",
    )(pos_col, freq)


EVEN_IN_STREAMS = 2


def _even_in_kernel(x_ref, g_ref, win_ref, qag_ref, kvag_ref, wuq_ref, wuk_ref, wuvt_ref, gains_ref,
                    poolw_ref, pscale_ref, tab_ref,
                    q_ref, k_ref, vt_ref, b_ref, carry_ref, *, tm, lora, pool_width, scale):
    s = pl.program_id(1)
    off_pool = 2 * lora
    off_kr = off_pool + pool_width
    gains = gains_ref[...]
    qg_nope, qg_rope2, qg_ropes2 = gains[0:1], gains[1:2], gains[2:3]
    kg_nope, kg_rope2, kg_ropes2 = gains[3:4], gains[4:5], gains[5:6]
    nheads = MLA_HEADS
    gd = pool_width // len(POOL_WINDOWS)
    pscale = pscale_ref[...]
    pair_w = 2 * NOPE_DIM + 2 * LANES
    n = tm // EVEN_IN_STREAMS
    assert len(POOL_WINDOWS) == 4 and nheads == 8 and n % POOL_HISTORY == 0
    u_tails = {}

    def stream(k):
        rows = slice(k * n, (k + 1) * n)
        lane = lax.broadcasted_iota(jnp.int32, (n, LANES), 1)
        low_half = lane < ROPE_DIM
        t_in_seq = s * tm + k * n + lax.broadcasted_iota(jnp.int32, (n, 1), 0)

        hn = _rms(x_ref[rows, :], g_ref[...]).astype(BF16)
        c_kv = _dot(hn, win_ref[:, lora:2 * lora])
        kr_both = _dot(hn, win_ref[:, off_kr:off_kr + 2 * LANES])
        c_q = _dot(hn, win_ref[:, :lora])
        yield

        ckvn = _rms(c_kv, kvag_ref[...]).astype(BF16)
        cqn = _rms(c_q, qag_ref[...]).astype(BF16)
        kr2 = kr_both[:, :LANES]
        krs2 = kr_both[:, LANES:]
        cos2 = tab_ref[rows, :LANES]
        sin2 = tab_ref[rows, LANES:]
        knall = _dot(ckvn, wuk_ref[...])

        def pool_project(g0, g1):
            return _dot(hn, win_ref[:, off_pool + g0 * gd:off_pool + g1 * gd])

        def pool_windows(u_all, g0, g1):
            pooled = []
            for gi in range(g0, g1):
                w = POOL_WINDOWS[gi]
                cols = slice(gi * gd, (gi + 1) * gd)
                u = u_all[:, (gi - g0) * gd:(gi - g0 + 1) * gd]
                if k == 0:
                    prev = jnp.where(s == 0, 0.0, carry_ref[:, cols])
                else:
                    prev = u_tails[(k - 1, gi)]
                u_tails[(k, gi)] = u[n - POOL_HISTORY:n]
                if k == EVEN_IN_STREAMS - 1:
                    carry_ref[:, cols] = u[n - POOL_HISTORY:n]
                e = jnp.concatenate([prev, u], axis=0)
                sh = 1
                while sh < w:
                    e = e + pltpu.roll(e, sh, axis=0)
                    sh *= 2
                inv_cnt = 1.0 / jnp.minimum(t_in_seq + 1, w).astype(F32)
                pooled.append((e[POOL_HISTORY:] * inv_cnt - u).astype(BF16))
            return pooled

        def pool_mix(pooled, g0):
            for j, pg in enumerate(pooled):
                cols = slice((g0 + j) * gd, (g0 + j + 1) * gd)
                b_ref[rows, cols] = (_dot(pg, poolw_ref[g0 + j]) * pscale[:, cols]).astype(BF16)

        def query_project(p):
            return _dot(cqn, wuq_ref[:, p * pair_w:(p + 1) * pair_w])

        def query_finish(qp, p):
            qr2 = qp[:, 2 * NOPE_DIM:2 * NOPE_DIM + LANES]
            qrs2 = qp[:, 2 * NOPE_DIM + LANES:]
            sq = qr2 * qr2
            roped = qr2 * qg_rope2 * cos2 + qrs2 * qg_ropes2 * sin2
            for e in range(2):
                h = 2 * p + e
                qn = qp[:, e * NOPE_DIM:(e + 1) * NOPE_DIM]
                mine = low_half if e == 0 else jnp.logical_not(low_half)
                ss = jnp.sum(qn * qn + jnp.where(mine, sq, 0.0), axis=-1, keepdims=True)
                r = lax.rsqrt(ss * (1.0 / QK_DIM) + RMS_EPS) * scale
                q_ref[0, h, rows, 0:NOPE_DIM] = (qn * r * qg_nope).astype(BF16)
                rp = roped * r
                if e == 1:
                    rp = pltpu.roll(rp, ROPE_DIM, axis=1)
                q_ref[0, h, rows, NOPE_DIM:QK_DIM] = rp[:, :ROPE_DIM].astype(BF16)

        u01 = pool_project(0, 2)
        qp0 = query_project(0)
        qp1 = query_project(1)
        yield

        kr_sq = jnp.where(low_half, kr2 * kr2, 0.0)
        k_roped2 = kr2 * kg_rope2 * cos2 + krs2 * kg_ropes2 * sin2
        for h in range(nheads):
            kn = knall[:, h * NOPE_DIM:(h + 1) * NOPE_DIM]
            ss = jnp.sum(kn * kn + kr_sq, axis=-1, keepdims=True)
            r = lax.rsqrt(ss * (1.0 / QK_DIM) + RMS_EPS)
            k_ref[0, h, rows, 0:NOPE_DIM] = (kn * r * kg_nope).astype(BF16)
            k_ref[0, h, rows, NOPE_DIM:QK_DIM] = (k_roped2 * r)[:, :ROPE_DIM].astype(BF16)
        yield

        pooled01 = pool_windows(u01, 0, 2)
        u23 = pool_project(2, 4)
        pool_mix(pooled01, 0)
        yield

        query_finish(qp0, 0)
        query_finish(qp1, 1)
        qp2 = query_project(2)
        qp3 = query_project(3)
        yield

        pooled23 = pool_windows(u23, 2, 4)
        pool_mix(pooled23, 2)
        vt_all = lax.dot_general(wuvt_ref[...], ckvn, (((1,), (1,)), ((), ())), preferred_element_type=F32)
        yield

        query_finish(qp2, 2)
        query_finish(qp3, 3)
        for h in range(nheads):
            vt_ref[0, h, 0, 0:V_DIM, rows] = vt_all[h * V_DIM:(h + 1) * V_DIM].astype(BF16)
            vt_ref[0, h, 0, V_DIM:VT_ROWS, rows] = jnp.ones((VT_ROWS - V_DIM, n), BF16)
        yield

    streams = [stream(k) for k in range(EVEN_IN_STREAMS)]
    started = 0
    live = []
    while started < len(streams) or live:
        if started < len(streams):
            live.append(streams[started])
            started += 1
        for gen in list(live):
            try:
                next(gen)
            except StopIteration:
                live.remove(gen)


def _even_in_call(xf, g, win, qag, kvag, wuq, wuk, wuvt, gains, poolw, pscale, tab, e, *, B, S, tm, tk):
    T, D = xf.shape
    lora = qag.shape[1]
    pool_width = pscale.shape[1]
    n_s = S // tm
    H = MLA_HEADS
    per_kv_block = tk // tm
    tok = lambda b, s: (b * n_s + s, 0)
    const2 = lambda b, s: (0, 0)
    layer3 = lambda b, s: (e, 0, 0)
    kern = functools.partial(_even_in_kernel, tm=tm, lora=lora, pool_width=pool_width,
                             scale=QK_DIM ** -0.5 * LOG2_E)
    return pl.pallas_call(
        kern,
        grid=(B, n_s),
        in_specs=[
            pl.BlockSpec((tm, D), tok),
            _resident((1, D), const2),
            _resident((None,) + win.shape[1:], layer3),
            _resident((1, lora), const2),
            _resident((1, lora), const2),
            _resident((None,) + wuq.shape[1:], layer3),
            _resident((None,) + wuk.shape[1:], layer3),
            _resident((None,) + wuvt.shape[1:], layer3),
            _resident((None,) + gains.shape[1:], layer3),
            _resident((None,) + poolw.shape[1:], lambda b, s: (e, 0, 0, 0)),
            _resident((1, pool_width), const2),
            pl.BlockSpec((tm, 2 * LANES), tok),
        ],
        out_specs=[
            pl.BlockSpec((1, H, tm, QK_DIM), lambda b, s: (b, 0, s, 0)),
            pl.BlockSpec((1, H, tm, QK_DIM), lambda b, s: (b, 0, s, 0)),
            pl.BlockSpec((1, H, 1, VT_ROWS, tm),
                         lambda b, s: (b, 0, s // per_kv_block, 0, s % per_kv_block)),
            pl.BlockSpec((tm, pool_width), tok),
        ],
        out_shape=[
            jax.ShapeDtypeStruct((B, H, S, QK_DIM), BF16),
            jax.ShapeDtypeStruct((B, H, S, QK_DIM), BF16),
            jax.ShapeDtypeStruct((B, H, S // tk, VT_ROWS, tk), BF16),
            jax.ShapeDtypeStruct((T, pool_width), BF16),
        ],
        scratch_shapes=[pltpu.VMEM((POOL_HISTORY, pool_width), F32)],
        compiler_params=_params("arbitrary", "arbitrary"),
        name="even_in",
    )(xf, g, win, qag, kvag, wuq, wuk, wuvt, gains, poolw, pscale, tab)


ATTN_HEADS_PER_STEP = 4
ATTN_SLOTS = 2
ATTN_SCRATCH_PER_HEAD = 2 + 4 * ATTN_SLOTS


def _attn_kernel(q_ref, k_ref, vt_ref, o_ref, *scratch, tq):
    i = pl.program_id(2)
    heads = range(ATTN_HEADS_PER_STEP)
    ns = ATTN_SLOTS
    per_head = [scratch[h * ATTN_SCRATCH_PER_HEAD:(h + 1) * ATTN_SCRATCH_PER_HEAD] for h in heads]
    m_refs = [r[0] for r in per_head]
    acc_refs = [r[1] for r in per_head]
    s_refs = [r[2:2 + ns] for r in per_head]
    p_refs = [r[2 + ns:2 + 2 * ns] for r in per_head]
    a_refs = [r[2 + 2 * ns:2 + 3 * ns] for r in per_head]
    bm_refs = [r[2 + 3 * ns:2 + 4 * ns] for r in per_head]

    def scores(h, j, slot):
        start = pl.multiple_of(j * tq, tq)
        kb = k_ref[0, h, pl.ds(start, tq), :]
        s = lax.dot_general(kb, q_ref[0, h], (((1,), (1,)), ((), ())), preferred_element_type=F32)
        s_refs[h][slot][...] = s
        bm_refs[h][slot][...] = jnp.max(s, axis=0, keepdims=True)

    def softmax(h, slot, masked):
        s = s_refs[h][slot][...]
        if masked:
            key = lax.broadcasted_iota(jnp.int32, s.shape, 0)
            qry = lax.broadcasted_iota(jnp.int32, s.shape, 1)
            s = jnp.where(key <= qry, s, -1e30)
            block_max = jnp.max(s, axis=0, keepdims=True)
        else:
            block_max = bm_refs[h][slot][...]
        m_prev = m_refs[h][...]
        m_new = jnp.maximum(m_prev, block_max)
        m_refs[h][...] = m_new
        a_refs[h][slot][...] = jnp.exp2(m_prev - m_new)
        p_refs[h][slot][...] = jnp.exp2(s - m_new).astype(BF16)

    def accumulate(h, j, slot):
        acc_refs[h][...] = (a_refs[h][slot][...] * acc_refs[h][...]
                            + _dot(vt_ref[0, h, j], p_refs[h][slot][...]))

    def step(j, slot):
        for h in heads:
            scores(h, j + 1, (slot + 1) % ns)
            accumulate(h, jnp.maximum(j - 1, 0), (slot - 1) % ns)
            softmax(h, slot, False)

    def last(slot):
        for h in heads:
            accumulate(h, jnp.maximum(i - 1, 0), (slot - 1) % ns)
            softmax(h, slot, True)
            accumulate(h, i, slot)
            acc = acc_refs[h][...]
            out_t = acc[0:V_DIM] / acc[V_DIM:V_DIM + 1]
            o_ref[0, :, h * V_DIM:(h + 1) * V_DIM] = out_t.T.astype(BF16)

    for h in heads:
        m_refs[h][...] = jnp.full(m_refs[h].shape, -jnp.inf, F32)
        acc_refs[h][...] = jnp.zeros(acc_refs[h].shape, F32)
        scores(h, 0, 0)
        p_refs[h][ns - 1][...] = jnp.zeros(p_refs[h][ns - 1].shape, BF16)
        a_refs[h][ns - 1][...] = jnp.ones(a_refs[h][ns - 1].shape, F32)

    def group(g, carry):
        for slot in range(ns):
            step(ns * g + slot, slot)
        return carry

    lax.fori_loop(0, i // ns, group, 0)

    base = (i // ns) * ns
    for rem in range(ns):
        @pl.when(i % ns == rem)
        def _(rem=rem):
            for slot in range(rem):
                step(base + slot, slot)
            last(rem)


def _attn_call(q, k, vt, *, tq):
    B, H, S, _ = q.shape
    n_kv = vt.shape[2]
    assert vt.shape[4] == tq
    hps = ATTN_HEADS_PER_STEP
    head_scratch = ([pltpu.VMEM((1, tq), F32), pltpu.VMEM((VT_ROWS, tq), F32)]
                    + [pltpu.VMEM((tq, tq), F32)] * ATTN_SLOTS
                    + [pltpu.VMEM((tq, tq), BF16)] * ATTN_SLOTS
                    + [pltpu.VMEM((1, tq), F32)] * (2 * ATTN_SLOTS))
    assert len(head_scratch) == ATTN_SCRATCH_PER_HEAD
    return pl.pallas_call(
        functools.partial(_attn_kernel, tq=tq),
        grid=(B, H // hps, S // tq),
        in_specs=[
            pl.BlockSpec((1, hps, tq, QK_DIM), lambda b, h, i: (b, h, i, 0)),
            pl.BlockSpec((1, hps, S, QK_DIM), lambda b, h, i: (b, h, 0, 0)),
            pl.BlockSpec((1, hps, n_kv, VT_ROWS, tq), lambda b, h, i: (b, h, 0, 0, 0)),
        ],
        out_specs=pl.BlockSpec((1, tq, hps * V_DIM), lambda b, h, i: (b, i, h)),
        out_shape=jax.ShapeDtypeStruct((B, S, H * V_DIM), BF16),
        scratch_shapes=head_scratch * hps,
        compiler_params=_params("parallel", "parallel", "arbitrary"),
        name="attn",
    )(q, k, vt)


def _outproj_kernel(x_ref, a_ref, b_ref, wa_ref, wb_ref, o_ref):
    o_ref[...] = x_ref[...] + _dot(a_ref[...], wa_ref[...]) + _dot(b_ref[...], wb_ref[...])


def _outproj_call(xf, a, b, wout, e, *, tm):
    T, D = xf.shape
    wa_rows = a.shape[1]
    wb_rows = b.shape[1]
    assert wa_rows == wb_rows
    return pl.pallas_call(
        _outproj_kernel,
        grid=(T // tm,),
        in_specs=[
            pl.BlockSpec((tm, D), lambda i: (i, 0)),
            pl.BlockSpec((tm, wa_rows), lambda i: (i, 0)),
            pl.BlockSpec((tm, wb_rows), lambda i: (i, 0)),
            _resident((None, wa_rows, D), lambda i: (e, 0, 0)),
            _resident((None, wb_rows, D), lambda i: (e, 1, 0)),
        ],
        out_specs=pl.BlockSpec((tm, D), lambda i: (i, 0)),
        out_shape=jax.ShapeDtypeStruct((T, D), F32),
        compiler_params=_params("parallel"),
        name="outproj",
    )(xf, a, b, wout, wout)


def _conv_kernel(x_ref, g_ref, wb_ref, wc_ref, wu_ref, cw_ref, wo_ref, o_ref, r_ref, carry_ref,
                 *, tm, tiles_per_seq):
    i = pl.program_id(0)
    c = pl.program_id(1)
    first = c == 0

    @pl.when(first)
    def _():
        _store_inv_rms(x_ref, r_ref)

    x = x_ref[...]
    hn = _normed_operand(x, r_ref, g_ref)
    cw = cw_ref[...]
    first_tile = i % tiles_per_seq == 0
    tc = wo_ref.shape[0]
    gated = []
    for lo in range(0, tc, MXU_TILE):
        cols = slice(lo, lo + MXU_TILE)
        gate_b = _dot(hn, wb_ref[:, cols])
        v = _dot(hn, wc_ref[:, cols]) * _dot(hn, wu_ref[:, cols])
        prev = jnp.where(first_tile, 0.0, carry_ref[c, :, cols])
        carry_ref[c, :, cols] = v[tm - CONV_HISTORY:tm]
        ext = jnp.concatenate([prev, v], axis=0)
        conv = cw[CONV_WIDTH - 1:CONV_WIDTH, cols] * v
        for back in range(1, CONV_WIDTH):
            tap = CONV_WIDTH - 1 - back
            conv = conv + cw[tap:tap + 1, cols] * pltpu.roll(ext, back, axis=0)[CONV_HISTORY:]
        gated.append((gate_b * conv).astype(BF16))
    o_ref[...] = jnp.where(first, x, o_ref[...]) + _dot(jnp.concatenate(gated, axis=1), wo_ref[...])


def _conv_call(xf, g, win, convw, wout, o, *, S, tm, tc):
    T, D = xf.shape
    C = wout.shape[1]
    n_c = C // tc
    kern = functools.partial(_conv_kernel, tm=tm, tiles_per_seq=S // tm)
    return pl.pallas_call(
        kern,
        grid=(T // tm, n_c),
        in_specs=[
            pl.BlockSpec((tm, D), lambda i, c: (i, 0)),
            pl.BlockSpec((1, D), lambda i, c: (0, 0)),
            pl.BlockSpec((None, D, tc), lambda i, c: (o, 0, c)),
            pl.BlockSpec((None, D, tc), lambda i, c: (o, 0, n_c + c)),
            pl.BlockSpec((None, D, tc), lambda i, c: (o, 0, 2 * n_c + c)),
            pl.BlockSpec((None, CONV_WIDTH, tc), lambda i, c: (o, 0, c)),
            pl.BlockSpec((None, tc, D), lambda i, c: (o, c, 0)),
        ],
        out_specs=pl.BlockSpec((tm, D), lambda i, c: (i, 0)),
        out_shape=jax.ShapeDtypeStruct((T, D), F32),
        scratch_shapes=[pltpu.VMEM((tm, LANES), F32),
                        pltpu.VMEM((n_c, CONV_HISTORY, tc), F32)],
        compiler_params=_params("arbitrary", "arbitrary"),
        name="conv_mixer",
    )(xf, g, win, win, win, convw, wout)


def _mlp_kernel(x_ref, g_ref, wup_ref, wdn_ref, *rest, cast_next):
    if cast_next:
        next_up_ref, next_dn_ref, o_ref, next_up_out_ref, next_dn_out_ref, r_ref = rest
        next_up_out_ref[...] = next_up_ref[...].astype(BF16)
        next_dn_out_ref[...] = next_dn_ref[...].astype(BF16)
    else:
        o_ref, r_ref = rest
    first = pl.program_id(1) == 0

    @pl.when(first)
    def _():
        _store_inv_rms(x_ref, r_ref)

    x = x_ref[...]
    up = _dot(_normed_operand(x, r_ref, g_ref), wup_ref[...])
    act = jnp.square(jnp.maximum(up, 0.0)).astype(BF16)
    o_ref[...] = jnp.where(first, x, o_ref[...]) + _dot(act, wdn_ref[...])


BF16_SUBLANES = 16


def _mlp_call(xf, g, wup, wdn, *, tm, tf, next_weights=None):
    T, D = xf.shape
    F = wup.shape[1]
    nf = F // tf
    n_steps = (T // tm) * nf
    in_specs = [
        pl.BlockSpec((tm, D), lambda i, f: (i, 0)),
        pl.BlockSpec((1, D), lambda i, f: (0, 0)),
        pl.BlockSpec((D, tf), lambda i, f: (0, f)),
        pl.BlockSpec((tf, D), lambda i, f: (f, 0)),
    ]
    out_specs = [pl.BlockSpec((tm, D), lambda i, f: (i, 0))]
    out_shape = [jax.ShapeDtypeStruct((T, D), F32)]
    args = [xf, g, wup, wdn]
    if next_weights is not None:
        w_up32, w_dn32, layer = next_weights
        for w32 in (w_up32, w_dn32):
            rows, cols = w32.shape[1:]
            block_rows = max(BF16_SUBLANES, rows // n_steps)
            n_blocks = rows // block_rows
            assert rows % block_rows == 0 and n_steps % n_blocks == 0
            block_of = lambda i, f, n_blocks=n_blocks: ((i * nf + f) * n_blocks) // n_steps
            in_specs.append(pl.BlockSpec((None, block_rows, cols),
                                         lambda i, f, block_of=block_of: (layer, block_of(i, f), 0)))
            out_specs.append(pl.BlockSpec((block_rows, cols),
                                          lambda i, f, block_of=block_of: (block_of(i, f), 0)))
            out_shape.append(jax.ShapeDtypeStruct((rows, cols), BF16))
            args.append(w32)
    outs = pl.pallas_call(
        functools.partial(_mlp_kernel, cast_next=next_weights is not None),
        grid=(T // tm, nf),
        in_specs=in_specs,
        out_specs=out_specs,
        out_shape=out_shape,
        scratch_shapes=[pltpu.VMEM((tm, LANES), F32)],
        compiler_params=_params("arbitrary", "arbitrary"),
        name="mlp",
    )(*args)
    return outs if next_weights is not None else outs[0]


def _swap_halves(a):
    half = a.shape[-1] // 2
    return jnp.concatenate([a[..., half:], a[..., :half]], axis=-1)


def _prep_even_weights(even_w_in, even_w_uq, even_w_ukv, even_q_norm_g, even_k_norm_g, lora, pool_width):
    H = MLA_HEADS
    E = even_w_in.shape[0]
    off_kr = 2 * lora
    off_pool = off_kr + ROPE_DIM
    w = even_w_in.astype(BF16)
    kr = w[:, :, off_kr:off_pool]
    krs = _swap_halves(kr)
    win = jnp.concatenate([w[:, :, :off_kr], w[:, :, off_pool:off_pool + pool_width], kr, kr, krs, krs], axis=-1)

    wq = even_w_uq.astype(BF16).reshape(E, lora, H, QK_DIM)
    rope = wq[..., NOPE_DIM:]
    by_pair = lambda a: a.reshape(E, lora, H // 2, 2 * a.shape[-1])
    wuq = jnp.concatenate([by_pair(wq[..., :NOPE_DIM]), by_pair(rope), by_pair(_swap_halves(rope))], axis=-1)
    wuq = wuq.reshape(E, lora, -1)

    wkv = even_w_ukv.astype(BF16).reshape(E, even_w_ukv.shape[1], H, NOPE_DIM + V_DIM)
    wuk = wkv[..., :NOPE_DIM].reshape(E, wkv.shape[1], H * NOPE_DIM)
    wuvt = jnp.swapaxes(wkv[..., NOPE_DIM:].reshape(E, wkv.shape[1], H * V_DIM), 1, 2)

    def gain_rows(gvec):
        rope = gvec[:, NOPE_DIM:]
        ropes = _swap_halves(rope)
        return [gvec[:, :NOPE_DIM], jnp.concatenate([rope, rope], -1), jnp.concatenate([ropes, ropes], -1)]

    rows = gain_rows(even_q_norm_g) + gain_rows(even_k_norm_g)
    rows += [jnp.zeros_like(rows[0])] * 2
    gains = jnp.stack(rows, axis=1).astype(F32)
    return win, wuq, wuk, wuvt, gains


def _pick_tile(n, want):
    t = min(n, want)
    while n % t:
        t //= 2
    return t


def kernel(x, positions, mix_norm_g, mlp_norm_g, w_mlp_up, w_mlp_down, even_w_in, even_q_a_norm_g,
           even_kv_a_norm_g, even_w_uq, even_w_ukv, even_q_norm_g, even_k_norm_g, even_pool_w,
           even_pool_scale, even_w_out, odd_w_in, odd_conv_w, odd_w_out):
    B, S, D = x.shape
    T = B * S
    depth = mix_norm_g.shape[0]
    lora = even_q_a_norm_g.shape[1]
    pool_width = even_pool_scale.shape[1]

    tm_mlp = _pick_tile(S, 512)
    tf_mlp = _pick_tile(w_mlp_up.shape[2], 2048)
    tm_even = _pick_tile(S, 512)
    tq = _pick_tile(S, 512)
    tc_conv = _pick_tile(odd_w_out.shape[1], 1024)

    wup = w_mlp_up[0].astype(BF16)
    wdn = w_mlp_down[0].astype(BF16)
    win_e, wuq, wuk, wuvt, gains = _prep_even_weights(even_w_in, even_w_uq, even_w_ukv, even_q_norm_g,
                                                      even_k_norm_g, lora, pool_width)
    poolw = even_pool_w.astype(BF16)
    wout_e = even_w_out.astype(BF16)
    win_o = odd_w_in.astype(BF16)
    wout_o = odd_w_out.astype(BF16)

    tab = _rope_table_call(positions.reshape(T, 1), _pick_tile(T, 1024))

    xf = x.reshape(T, D)
    for layer in range(depth):
        g_mix = mix_norm_g[layer][None, :]
        if layer % 2 == 0:
            e = layer // 2
            q, k, vt, b = _even_in_call(
                xf, g_mix, win_e, even_q_a_norm_g[e][None, :], even_kv_a_norm_g[e][None, :],
                wuq, wuk, wuvt, gains, poolw, even_pool_scale[e][None, :], tab, e,
                B=B, S=S, tm=tm_even, tk=tq)
            a = _attn_call(q, k, vt, tq=tq).reshape(T, MLA_HEADS * V_DIM)
            xf = _outproj_call(xf, a, b, wout_e, e, tm=tm_mlp)
        else:
            o = layer // 2
            xf = _conv_call(xf, g_mix, win_o, odd_conv_w, wout_o, o, S=S, tm=tm_mlp, tc=tc_conv)
        g_mlp = mlp_norm_g[layer][None, :]
        if layer + 1 < depth:
            xf, wup, wdn = _mlp_call(xf, g_mlp, wup, wdn, tm=tm_mlp, tf=tf_mlp,
                                     next_weights=(w_mlp_up, w_mlp_down, layer + 1))
        else:
            xf = _mlp_call(xf, g_mlp, wup, wdn, tm=tm_mlp, tf=tf_mlp)
    return xf.reshape(B, S, D)
```

```python
import functools

import jax
import jax.numpy as jnp
from jax import lax
from jax.experimental import pallas as pl
from jax.experimental.pallas import tpu as pltpu

F32 = jnp.float32
BF16 = jnp.bfloat16

RMS_EPS = 1e-6
ROPE_THETA = 10000.0
MLA_HEADS = 8
NOPE_DIM = 128
ROPE_DIM = 64
QK_DIM = NOPE_DIM + ROPE_DIM
V_DIM = 128
VT_ROWS = V_DIM + 16
LOG2_E = 1.4426950408889634
POOL_WINDOWS = (2, 4, 8, 16)
POOL_HISTORY = 16
CONV_WIDTH = 3
CONV_HISTORY = 8
LANES = 128
MXU_TILE = 256

V7X_VMEM_BYTES = 64 * 1024 * 1024
VMEM_LIMIT_BYTES = V7X_VMEM_BYTES - 4 * 1024 * 1024


def _rms(xf, g):
    ms = jnp.mean(xf * xf, axis=-1, keepdims=True)
    return xf * lax.rsqrt(ms + RMS_EPS) * g


def _dot(a, b):
    return jnp.dot(a, b, preferred_element_type=F32)


def _store_inv_rms(x_ref, r_ref):
    x = x_ref[...]
    r = lax.rsqrt(jnp.mean(x * x, axis=-1, keepdims=True) + RMS_EPS)
    r_ref[...] = jnp.broadcast_to(r, r_ref.shape)


def _normed_operand(x, r_ref, g_ref):
    return (x * jnp.tile(r_ref[...], (1, x.shape[1] // LANES)) * g_ref[...]).astype(BF16)


def _params(*sem):
    return pltpu.CompilerParams(dimension_semantics=sem, vmem_limit_bytes=VMEM_LIMIT_BYTES)


def _resident(block_shape, index_map):
    return pl.BlockSpec(block_shape, index_map, pipeline_mode=pl.Buffered(1))


def _rope_table_kernel(pos_ref, freq_ref, tab_ref):
    ang = pos_ref[...].astype(F32) * freq_ref[...]
    lane = lax.broadcasted_iota(jnp.int32, ang.shape, 1)
    sign = jnp.where((lane % ROPE_DIM) < ROPE_DIM // 2, -1.0, 1.0)
    tab_ref[:, :LANES] = jnp.cos(ang)
    tab_ref[:, LANES:] = jnp.sin(ang) * sign


def _rope_table_call(pos_col, tm):
    T = pos_col.shape[0]
    inv_freq = 1.0 / (ROPE_THETA ** (jnp.arange(0, ROPE_DIM, 2, dtype=F32) / ROPE_DIM))
    freq = jnp.tile(inv_freq, LANES // (ROPE_DIM // 2))[None, :]
    return pl.pallas_call(
        _rope_table_kernel,
        grid=(T // tm,),
        in_specs=[pl.BlockSpec((tm, 1), lambda i: (i, 0)),
                  pl.BlockSpec((1, LANES), lambda i: (0, 0))],
        out_specs=pl.BlockSpec((tm, 2 * LANES), lambda i: (i, 0)),
        out_shape=jax.ShapeDtypeStruct((T, 2 * LANES), F32),
        compiler_params=_params("parallel"),
        name="rope_table",
    )(pos_col, freq)


EVEN_IN_STREAMS = 2


def _even_in_kernel(x_ref, g_ref, win_ref, qag_ref, kvag_ref, wuq_ref, wuk_ref, wuvt_ref, gains_ref,
                    poolw_ref, pscale_ref, tab_ref,
                    q_ref, k_ref, vt_ref, b_ref, carry_ref, *, tm, lora, pool_width, scale):
    s = pl.program_id(1)
    off_pool = 2 * lora
    off_kr = off_pool + pool_width
    gains = gains_ref[...]
    qg_nope, qg_rope2, qg_ropes2 = gains[0:1], gains[1:2], gains[2:3]
    kg_nope, kg_rope2, kg_ropes2 = gains[3:4], gains[4:5], gains[5:6]
    nheads = MLA_HEADS
    gd = pool_width // len(POOL_WINDOWS)
    pscale = pscale_ref[...]
    pair_w = 2 * NOPE_DIM + 2 * LANES
    n = tm // EVEN_IN_STREAMS
    assert len(POOL_WINDOWS) == 4 and nheads == 8 and n % POOL_HISTORY == 0
    u_tails = {}

    def stream(k):
        rows = slice(k * n, (k + 1) * n)
        lane = lax.broadcasted_iota(jnp.int32, (n, LANES), 1)
        low_half = lane < ROPE_DIM
        t_in_seq = s * tm + k * n + lax.broadcasted_iota(jnp.int32, (n, 1), 0)

        hn = _rms(x_ref[rows, :], g_ref[...]).astype(BF16)
        c_kv = _dot(hn, win_ref[:, lora:2 * lora])
        kr_both = _dot(hn, win_ref[:, off_kr:off_kr + 2 * LANES])
        c_q = _dot(hn, win_ref[:, :lora])
        yield

        ckvn = _rms(c_kv, kvag_ref[...]).astype(BF16)
        cqn = _rms(c_q, qag_ref[...]).astype(BF16)
        kr2 = kr_both[:, :LANES]
        krs2 = kr_both[:, LANES:]
        cos2 = tab_ref[rows, :LANES]
        sin2 = tab_ref[rows, LANES:]
        knall = _dot(ckvn, wuk_ref[...])

        def pool_project(g0, g1):
            return _dot(hn, win_ref[:, off_pool + g0 * gd:off_pool + g1 * gd])

        def pool_windows(u_all, g0, g1):
            pooled = []
            for gi in range(g0, g1):
                w = POOL_WINDOWS[gi]
                cols = slice(gi * gd, (gi + 1) * gd)
                u = u_all[:, (gi - g0) * gd:(gi - g0 + 1) * gd]
                if k == 0:
                    prev = jnp.where(s == 0, 0.0, carry_ref[:, cols])
                else:
                    prev = u_tails[(k - 1, gi)]
                u_tails[(k, gi)] = u[n - POOL_HISTORY:n]
                if k == EVEN_IN_STREAMS - 1:
                    carry_ref[:, cols] = u[n - POOL_HISTORY:n]
                e = jnp.concatenate([prev, u], axis=0)
                sh = 1
                while sh < w:
                    e = e + pltpu.roll(e, sh, axis=0)
                    sh *= 2
                inv_cnt = 1.0 / jnp.minimum(t_in_seq + 1, w).astype(F32)
                pooled.append((e[POOL_HISTORY:] * inv_cnt - u).astype(BF16))
            return pooled

        def pool_mix(pooled, g0):
            for j, pg in enumerate(pooled):
                cols = slice((g0 + j) * gd, (g0 + j + 1) * gd)
                b_ref[rows, cols] = (_dot(pg, poolw_ref[g0 + j]) * pscale[:, cols]).astype(BF16)

        def query_project(p):
            return _dot(cqn, wuq_ref[:, p * pair_w:(p + 1) * pair_w])

        def query_finish(qp, p):
            qr2 = qp[:, 2 * NOPE_DIM:2 * NOPE_DIM + LANES]
            qrs2 = qp[:, 2 * NOPE_DIM + LANES:]
            sq = qr2 * qr2
            roped = qr2 * qg_rope2 * cos2 + qrs2 * qg_ropes2 * sin2
            for e in range(2):
                h = 2 * p + e
                qn = qp[:, e * NOPE_DIM:(e + 1) * NOPE_DIM]
                mine = low_half if e == 0 else jnp.logical_not(low_half)
                ss = jnp.sum(qn * qn + jnp.where(mine, sq, 0.0), axis=-1, keepdims=True)
                r = lax.rsqrt(ss * (1.0 / QK_DIM) + RMS_EPS) * scale
                q_ref[0, h, rows, 0:NOPE_DIM] = (qn * r * qg_nope).astype(BF16)
                rp = roped * r
                if e == 1:
                    rp = pltpu.roll(rp, ROPE_DIM, axis=1)
                q_ref[0, h, rows, NOPE_DIM:QK_DIM] = rp[:, :ROPE_DIM].astype(BF16)

        u01 = pool_project(0, 2)
        qp0 = query_project(0)
        qp1 = query_project(1)
        yield

        kr_sq = jnp.where(low_half, kr2 * kr2, 0.0)
        k_roped2 = kr2 * kg_rope2 * cos2 + krs2 * kg_ropes2 * sin2
        for h in range(nheads):
            kn = knall[:, h * NOPE_DIM:(h + 1) * NOPE_DIM]
            ss = jnp.sum(kn * kn + kr_sq, axis=-1, keepdims=True)
            r = lax.rsqrt(ss * (1.0 / QK_DIM) + RMS_EPS)
            k_ref[0, h, rows, 0:NOPE_DIM] = (kn * r * kg_nope).astype(BF16)
            k_ref[0, h, rows, NOPE_DIM:QK_DIM] = (k_roped2 * r)[:, :ROPE_DIM].astype(BF16)
        pooled01 = pool_windows(u01, 0, 2)
        u23 = pool_project(2, 4)
        pool_mix(pooled01, 0)
        yield

        query_finish(qp0, 0)
        query_finish(qp1, 1)
        qp2 = query_project(2)
        qp3 = query_project(3)
        pooled23 = pool_windows(u23, 2, 4)
        pool_mix(pooled23, 2)
        vt_all = lax.dot_general(wuvt_ref[...], ckvn, (((1,), (1,)), ((), ())), preferred_element_type=F32)
        yield

        query_finish(qp2, 2)
        query_finish(qp3, 3)
        for h in range(nheads):
            vt_ref[0, h, 0, 0:V_DIM, rows] = vt_all[h * V_DIM:(h + 1) * V_DIM].astype(BF16)
            vt_ref[0, h, 0, V_DIM:VT_ROWS, rows] = jnp.ones((VT_ROWS - V_DIM, n), BF16)
        yield

    streams = [stream(k) for k in range(EVEN_IN_STREAMS)]
    started = 0
    live = []
    while started < len(streams) or live:
        if started < len(streams):
            live.append(streams[started])
            started += 1
        for gen in list(live):
            try:
                next(gen)
            except StopIteration:
                live.remove(gen)


def _even_in_call(xf, g, win, qag, kvag, wuq, wuk, wuvt, gains, poolw, pscale, tab, e, *, B, S, tm, tk):
    T, D = xf.shape
    lora = qag.shape[1]
    pool_width = pscale.shape[1]
    n_s = S // tm
    H = MLA_HEADS
    per_kv_block = tk // tm
    tok = lambda b, s: (b * n_s + s, 0)
    const2 = lambda b, s: (0, 0)
    layer3 = lambda b, s: (e, 0, 0)
    kern = functools.partial(_even_in_kernel, tm=tm, lora=lora, pool_width=pool_width,
                             scale=QK_DIM ** -0.5 * LOG2_E)
    return pl.pallas_call(
        kern,
        grid=(B, n_s),
        in_specs=[
            pl.BlockSpec((tm, D), tok),
            _resident((1, D), const2),
            _resident((None,) + win.shape[1:], layer3),
            _resident((1, lora), const2),
            _resident((1, lora), const2),
            _resident((None,) + wuq.shape[1:], layer3),
            _resident((None,) + wuk.shape[1:], layer3),
            _resident((None,) + wuvt.shape[1:], layer3),
            _resident((None,) + gains.shape[1:], layer3),
            _resident((None,) + poolw.shape[1:], lambda b, s: (e, 0, 0, 0)),
            _resident((1, pool_width), const2),
            pl.BlockSpec((tm, 2 * LANES), tok),
        ],
        out_specs=[
            pl.BlockSpec((1, H, tm, QK_DIM), lambda b, s: (b, 0, s, 0)),
            pl.BlockSpec((1, H, tm, QK_DIM), lambda b, s: (b, 0, s, 0)),
            pl.BlockSpec((1, H, 1, VT_ROWS, tm),
                         lambda b, s: (b, 0, s // per_kv_block, 0, s % per_kv_block)),
            pl.BlockSpec((tm, pool_width), tok),
        ],
        out_shape=[
            jax.ShapeDtypeStruct((B, H, S, QK_DIM), BF16),
            jax.ShapeDtypeStruct((B, H, S, QK_DIM), BF16),
            jax.ShapeDtypeStruct((B, H, S // tk, VT_ROWS, tk), BF16),
            jax.ShapeDtypeStruct((T, pool_width), BF16),
        ],
        scratch_shapes=[pltpu.VMEM((POOL_HISTORY, pool_width), F32)],
        compiler_params=_params("arbitrary", "arbitrary"),
        name="even_in",
    )(xf, g, win, qag, kvag, wuq, wuk, wuvt, gains, poolw, pscale, tab)


ATTN_HEADS_PER_STEP = 4
ATTN_SLOTS = 2
ATTN_SCRATCH_PER_HEAD = 2 + 4 * ATTN_SLOTS


def _attn_kernel(q_ref, k_ref, vt_ref, o_ref, *scratch, tq):
    i = pl.program_id(2)
    heads = range(ATTN_HEADS_PER_STEP)
    ns = ATTN_SLOTS
    per_head = [scratch[h * ATTN_SCRATCH_PER_HEAD:(h + 1) * ATTN_SCRATCH_PER_HEAD] for h in heads]
    m_refs = [r[0] for r in per_head]
    acc_refs = [r[1] for r in per_head]
    s_refs = [r[2:2 + ns] for r in per_head]
    p_refs = [r[2 + ns:2 + 2 * ns] for r in per_head]
    a_refs = [r[2 + 2 * ns:2 + 3 * ns] for r in per_head]
    bm_refs = [r[2 + 3 * ns:2 + 4 * ns] for r in per_head]

    def scores(h, j, slot):
        start = pl.multiple_of(j * tq, tq)
        kb = k_ref[0, h, pl.ds(start, tq), :]
        s = lax.dot_general(kb, q_ref[0, h], (((1,), (1,)), ((), ())), preferred_element_type=F32)
        s_refs[h][slot][...] = s
        bm_refs[h][slot][...] = jnp.max(s, axis=0, keepdims=True)

    def softmax(h, slot, masked):
        s = s_refs[h][slot][...]
        if masked:
            key = lax.broadcasted_iota(jnp.int32, s.shape, 0)
            qry = lax.broadcasted_iota(jnp.int32, s.shape, 1)
            s = jnp.where(key <= qry, s, -1e30)
            block_max = jnp.max(s, axis=0, keepdims=True)
        else:
            block_max = bm_refs[h][slot][...]
        m_prev = m_refs[h][...]
        m_new = jnp.maximum(m_prev, block_max)
        m_refs[h][...] = m_new
        a_refs[h][slot][...] = jnp.exp2(m_prev - m_new)
        p_refs[h][slot][...] = jnp.exp2(s - m_new).astype(BF16)

    def accumulate(h, j, slot):
        acc_refs[h][...] = (a_refs[h][slot][...] * acc_refs[h][...]
                            + _dot(vt_ref[0, h, j], p_refs[h][slot][...]))

    def step(j, slot):
        for h in heads:
            scores(h, j + 1, (slot + 1) % ns)
            accumulate(h, jnp.maximum(j - 1, 0), (slot - 1) % ns)
            softmax(h, slot, False)

    def last(slot):
        for h in heads:
            accumulate(h, jnp.maximum(i - 1, 0), (slot - 1) % ns)
            softmax(h, slot, True)
            accumulate(h, i, slot)
            acc = acc_refs[h][...]
            out_t = acc[0:V_DIM] / acc[V_DIM:V_DIM + 1]
            o_ref[0, :, h * V_DIM:(h + 1) * V_DIM] = out_t.T.astype(BF16)

    for h in heads:
        m_refs[h][...] = jnp.full(m_refs[h].shape, -jnp.inf, F32)
        acc_refs[h][...] = jnp.zeros(acc_refs[h].shape, F32)
        scores(h, 0, 0)
        p_refs[h][ns - 1][...] = jnp.zeros(p_refs[h][ns - 1].shape, BF16)
        a_refs[h][ns - 1][...] = jnp.ones(a_refs[h][ns - 1].shape, F32)

    def group(g, carry):
        for slot in range(ns):
            step(ns * g + slot, slot)
        return carry

    lax.fori_loop(0, i // ns, group, 0)

    base = (i // ns) * ns
    for rem in range(ns):
        @pl.when(i % ns == rem)
        def _(rem=rem):
            for slot in range(rem):
                step(base + slot, slot)
            last(rem)


def _attn_call(q, k, vt, *, tq):
    B, H, S, _ = q.shape
    n_kv = vt.shape[2]
    assert vt.shape[4] == tq
    hps = ATTN_HEADS_PER_STEP
    head_scratch = ([pltpu.VMEM((1, tq), F32), pltpu.VMEM((VT_ROWS, tq), F32)]
                    + [pltpu.VMEM((tq, tq), F32)] * ATTN_SLOTS
                    + [pltpu.VMEM((tq, tq), BF16)] * ATTN_SLOTS
                    + [pltpu.VMEM((1, tq), F32)] * (2 * ATTN_SLOTS))
    assert len(head_scratch) == ATTN_SCRATCH_PER_HEAD
    return pl.pallas_call(
        functools.partial(_attn_kernel, tq=tq),
        grid=(B, H // hps, S // tq),
        in_specs=[
            pl.BlockSpec((1, hps, tq, QK_DIM), lambda b, h, i: (b, h, i, 0)),
            pl.BlockSpec((1, hps, S, QK_DIM), lambda b, h, i: (b, h, 0, 0)),
            pl.BlockSpec((1, hps, n_kv, VT_ROWS, tq), lambda b, h, i: (b, h, 0, 0, 0)),
        ],
        out_specs=pl.BlockSpec((1, tq, hps * V_DIM), lambda b, h, i: (b, i, h)),
        out_shape=jax.ShapeDtypeStruct((B, S, H * V_DIM), BF16),
        scratch_shapes=head_scratch * hps,
        compiler_params=_params("parallel", "parallel", "arbitrary"),
        name="attn",
    )(q, k, vt)


def _outproj_kernel(x_ref, a_ref, b_ref, wa_ref, wb_ref, o_ref):
    o_ref[...] = x_ref[...] + _dot(a_ref[...], wa_ref[...]) + _dot(b_ref[...], wb_ref[...])


def _outproj_call(xf, a, b, wout, e, *, tm):
    T, D = xf.shape
    wa_rows = a.shape[1]
    wb_rows = b.shape[1]
    assert wa_rows == wb_rows
    return pl.pallas_call(
        _outproj_kernel,
        grid=(T // tm,),
        in_specs=[
            pl.BlockSpec((tm, D), lambda i: (i, 0)),
            pl.BlockSpec((tm, wa_rows), lambda i: (i, 0)),
            pl.BlockSpec((tm, wb_rows), lambda i: (i, 0)),
            _resident((None, wa_rows, D), lambda i: (e, 0, 0)),
            _resident((None, wb_rows, D), lambda i: (e, 1, 0)),
        ],
        out_specs=pl.BlockSpec((tm, D), lambda i: (i, 0)),
        out_shape=jax.ShapeDtypeStruct((T, D), F32),
        compiler_params=_params("parallel"),
        name="outproj",
    )(xf, a, b, wout, wout)


def _conv_kernel(x_ref, g_ref, wb_ref, wc_ref, wu_ref, cw_ref, wo_ref, o_ref, r_ref, carry_ref,
                 *, tm, tiles_per_seq):
    i = pl.program_id(0)
    c = pl.program_id(1)
    first = c == 0

    @pl.when(first)
    def _():
        _store_inv_rms(x_ref, r_ref)

    x = x_ref[...]
    hn = _normed_operand(x, r_ref, g_ref)
    cw = cw_ref[...]
    first_tile = i % tiles_per_seq == 0
    tc = wo_ref.shape[0]
    gated = []
    for lo in range(0, tc, MXU_TILE):
        cols = slice(lo, lo + MXU_TILE)
        gate_b = _dot(hn, wb_ref[:, cols])
        v = _dot(hn, wc_ref[:, cols]) * _dot(hn, wu_ref[:, cols])
        prev = jnp.where(first_tile, 0.0, carry_ref[c, :, cols])
        carry_ref[c, :, cols] = v[tm - CONV_HISTORY:tm]
        ext = jnp.concatenate([prev, v], axis=0)
        conv = cw[CONV_WIDTH - 1:CONV_WIDTH, cols] * v
        for back in range(1, CONV_WIDTH):
            tap = CONV_WIDTH - 1 - back
            conv = conv + cw[tap:tap + 1, cols] * pltpu.roll(ext, back, axis=0)[CONV_HISTORY:]
        gated.append((gate_b * conv).astype(BF16))
    o_ref[...] = jnp.where(first, x, o_ref[...]) + _dot(jnp.concatenate(gated, axis=1), wo_ref[...])


def _conv_call(xf, g, win, convw, wout, o, *, S, tm, tc):
    T, D = xf.shape
    C = wout.shape[1]
    n_c = C // tc
    kern = functools.partial(_conv_kernel, tm=tm, tiles_per_seq=S // tm)
    return pl.pallas_call(
        kern,
        grid=(T // tm, n_c),
        in_specs=[
            pl.BlockSpec((tm, D), lambda i, c: (i, 0)),
            pl.BlockSpec((1, D), lambda i, c: (0, 0)),
            pl.BlockSpec((None, D, tc), lambda i, c: (o, 0, c)),
            pl.BlockSpec((None, D, tc), lambda i, c: (o, 0, n_c + c)),
            pl.BlockSpec((None, D, tc), lambda i, c: (o, 0, 2 * n_c + c)),
            pl.BlockSpec((None, CONV_WIDTH, tc), lambda i, c: (o, 0, c)),
            pl.BlockSpec((None, tc, D), lambda i, c: (o, c, 0)),
        ],
        out_specs=pl.BlockSpec((tm, D), lambda i, c: (i, 0)),
        out_shape=jax.ShapeDtypeStruct((T, D), F32),
        scratch_shapes=[pltpu.VMEM((tm, LANES), F32),
                        pltpu.VMEM((n_c, CONV_HISTORY, tc), F32)],
        compiler_params=_params("arbitrary", "arbitrary"),
        name="conv_mixer",
    )(xf, g, win, win, win, convw, wout)


def _mlp_kernel(x_ref, g_ref, wup_ref, wdn_ref, *rest, cast_next):
    if cast_next:
        next_up_ref, next_dn_ref, o_ref, next_up_out_ref, next_dn_out_ref, r_ref = rest
        next_up_out_ref[...] = next_up_ref[...].astype(BF16)
        next_dn_out_ref[...] = next_dn_ref[...].astype(BF16)
    else:
        o_ref, r_ref = rest
    first = pl.program_id(1) == 0

    @pl.when(first)
    def _():
        _store_inv_rms(x_ref, r_ref)

    x = x_ref[...]
    up = _dot(_normed_operand(x, r_ref, g_ref), wup_ref[...])
    act = jnp.square(jnp.maximum(up, 0.0)).astype(BF16)
    o_ref[...] = jnp.where(first, x, o_ref[...]) + _dot(act, wdn_ref[...])


BF16_SUBLANES = 16


def _mlp_call(xf, g, wup, wdn, *, tm, tf, next_weights=None):
    T, D = xf.shape
    F = wup.shape[1]
    nf = F // tf
    n_steps = (T // tm) * nf
    in_specs = [
        pl.BlockSpec((tm, D), lambda i, f: (i, 0)),
        pl.BlockSpec((1, D), lambda i, f: (0, 0)),
        pl.BlockSpec((D, tf), lambda i, f: (0, f)),
        pl.BlockSpec((tf, D), lambda i, f: (f, 0)),
    ]
    out_specs = [pl.BlockSpec((tm, D), lambda i, f: (i, 0))]
    out_shape = [jax.ShapeDtypeStruct((T, D), F32)]
    args = [xf, g, wup, wdn]
    if next_weights is not None:
        w_up32, w_dn32, layer = next_weights
        for w32 in (w_up32, w_dn32):
            rows, cols = w32.shape[1:]
            block_rows = max(BF16_SUBLANES, rows // n_steps)
            n_blocks = rows // block_rows
            assert rows % block_rows == 0 and n_steps % n_blocks == 0
            block_of = lambda i, f, n_blocks=n_blocks: ((i * nf + f) * n_blocks) // n_steps
            in_specs.append(pl.BlockSpec((None, block_rows, cols),
                                         lambda i, f, block_of=block_of: (layer, block_of(i, f), 0)))
            out_specs.append(pl.BlockSpec((block_rows, cols),
                                          lambda i, f, block_of=block_of: (block_of(i, f), 0)))
            out_shape.append(jax.ShapeDtypeStruct((rows, cols), BF16))
            args.append(w32)
    outs = pl.pallas_call(
        functools.partial(_mlp_kernel, cast_next=next_weights is not None),
        grid=(T // tm, nf),
        in_specs=in_specs,
        out_specs=out_specs,
        out_shape=out_shape,
        scratch_shapes=[pltpu.VMEM((tm, LANES), F32)],
        compiler_params=_params("arbitrary", "arbitrary"),
        name="mlp",
    )(*args)
    return outs if next_weights is not None else outs[0]


def _swap_halves(a):
    half = a.shape[-1] // 2
    return jnp.concatenate([a[..., half:], a[..., :half]], axis=-1)


def _prep_even_weights(even_w_in, even_w_uq, even_w_ukv, even_q_norm_g, even_k_norm_g, lora, pool_width):
    H = MLA_HEADS
    E = even_w_in.shape[0]
    off_kr = 2 * lora
    off_pool = off_kr + ROPE_DIM
    w = even_w_in.astype(BF16)
    kr = w[:, :, off_kr:off_pool]
    krs = _swap_halves(kr)
    win = jnp.concatenate([w[:, :, :off_kr], w[:, :, off_pool:off_pool + pool_width], kr, kr, krs, krs], axis=-1)

    wq = even_w_uq.astype(BF16).reshape(E, lora, H, QK_DIM)
    rope = wq[..., NOPE_DIM:]
    by_pair = lambda a: a.reshape(E, lora, H // 2, 2 * a.shape[-1])
    wuq = jnp.concatenate([by_pair(wq[..., :NOPE_DIM]), by_pair(rope), by_pair(_swap_halves(rope))], axis=-1)
    wuq = wuq.reshape(E, lora, -1)

    wkv = even_w_ukv.astype(BF16).reshape(E, even_w_ukv.shape[1], H, NOPE_DIM + V_DIM)
    wuk = wkv[..., :NOPE_DIM].reshape(E, wkv.shape[1], H * NOPE_DIM)
    wuvt = jnp.swapaxes(wkv[..., NOPE_DIM:].reshape(E, wkv.shape[1], H * V_DIM), 1, 2)

    def gain_rows(gvec):
        rope = gvec[:, NOPE_DIM:]
        ropes = _swap_halves(rope)
        return [gvec[:, :NOPE_DIM], jnp.concatenate([rope, rope], -1), jnp.concatenate([ropes, ropes], -1)]

    rows = gain_rows(even_q_norm_g) + gain_rows(even_k_norm_g)
    rows += [jnp.zeros_like(rows[0])] * 2
    gains = jnp.stack(rows, axis=1).astype(F32)
    return win, wuq, wuk, wuvt, gains


def _pick_tile(n, want):
    t = min(n, want)
    while n % t:
        t //= 2
    return t


def kernel(x, positions, mix_norm_g, mlp_norm_g, w_mlp_up, w_mlp_down, even_w_in, even_q_a_norm_g,
           even_kv_a_norm_g, even_w_uq, even_w_ukv, even_q_norm_g, even_k_norm_g, even_pool_w,
           even_pool_scale, even_w_out, odd_w_in, odd_conv_w, odd_w_out):
    B, S, D = x.shape
    T = B * S
    depth = mix_norm_g.shape[0]
    lora = even_q_a_norm_g.shape[1]
    pool_width = even_pool_scale.shape[1]

    tm_mlp = _pick_tile(S, 512)
    tf_mlp = _pick_tile(w_mlp_up.shape[2], 2048)
    tm_even = _pick_tile(S, 512)
    tq = _pick_tile(S, 512)
    tc_conv = _pick_tile(odd_w_out.shape[1], 1024)

    wup = w_mlp_up[0].astype(BF16)
    wdn = w_mlp_down[0].astype(BF16)
    win_e, wuq, wuk, wuvt, gains = _prep_even_weights(even_w_in, even_w_uq, even_w_ukv, even_q_norm_g,
                                                      even_k_norm_g, lora, pool_width)
    poolw = even_pool_w.astype(BF16)
    wout_e = even_w_out.astype(BF16)
    win_o = odd_w_in.astype(BF16)
    wout_o = odd_w_out.astype(BF16)

    tab = _rope_table_call(positions.reshape(T, 1), _pick_tile(T, 1024))

    xf = x.reshape(T, D)
    for layer in range(depth):
        g_mix = mix_norm_g[layer][None, :]
        if layer % 2 == 0:
            e = layer // 2
            q, k, vt, b = _even_in_call(
                xf, g_mix, win_e, even_q_a_norm_g[e][None, :], even_kv_a_norm_g[e][None, :],
                wuq, wuk, wuvt, gains, poolw, even_pool_scale[e][None, :], tab, e,
                B=B, S=S, tm=tm_even, tk=tq)
            a = _attn_call(q, k, vt, tq=tq).reshape(T, MLA_HEADS * V_DIM)
            xf = _outproj_call(xf, a, b, wout_e, e, tm=tm_mlp)
        else:
            o = layer // 2
            xf = _conv_call(xf, g_mix, win_o, odd_conv_w, wout_o, o, S=S, tm=tm_mlp, tc=tc_conv)
        g_mlp = mlp_norm_g[layer][None, :]
        if layer + 1 < depth:
            xf, wup, wdn = _mlp_call(xf, g_mlp, wup, wdn, tm=tm_mlp, tf=tf_mlp,
                                     next_weights=(w_mlp_up, w_mlp_down, layer + 1))
        else:
            xf = _mlp_call(xf, g_mlp, wup, wdn, tm=tm_mlp, tf=tf_mlp)
    return xf.reshape(B, S, D)
```
